```python
import jax, jax.numpy as jnp
from jax import lax
import numpy as np

D_MODEL = 2048
BATCH = 2
SEQ = 16384
DEPTH = 2

EPS = 1e-6
CONV_W = 4
CHUNK = 64
GDN_HEADS = 8
GDN_DK = 128
GDN_DV = 128
GDN_QK = GDN_HEADS * GDN_DK
GDN_V = GDN_HEADS * GDN_DV
GLA_HEADS = 8
GLA_DK = 64
GLA_DV = 128
GLA_QK = GLA_HEADS * GLA_DK
GLA_V = GLA_HEADS * GLA_DV
GLA_RANK = 16
GLA_GATE_NORM = 16.0
IN0_SIZES = (GDN_QK, GDN_QK, GDN_V, GDN_V, GDN_HEADS, GDN_HEADS,
             GLA_QK, GLA_QK, GLA_V, GLA_V, GLA_RANK)
IN0_WIDTH = 2 * GDN_QK + 2 * GDN_V + 2 * GDN_HEADS + 2 * GLA_QK + 2 * GLA_V + GLA_RANK
MIX0_WIDTH = GDN_V + GLA_V
LRU_WIDTH = D_MODEL
LRU_BLOCKS = 8
LRU_BLOCK = LRU_WIDTH // LRU_BLOCKS
LRU_C = 8.0
N_EXPERTS = 32
TOP_K = 4
D_FF = D_MODEL // 2
SWIGLU_LIMIT = 7.0
SWIGLU_ALPHA = 1.702
MOE_BLOCK = 256
N_EVEN = (DEPTH + 1) // 2
N_ODD = DEPTH // 2

kernel_name = 'hybrid_gdn_gla_rglru_moe_adaln'


def rmsnorm(x, w):
    xf = x.astype(jnp.float32)
    y = xf * lax.rsqrt(jnp.mean(xf * xf, axis=-1, keepdims=True) + EPS)
    return (y * w.astype(jnp.float32)).astype(x.dtype)


def l2norm(x):
    xf = x.astype(jnp.float32)
    return xf * lax.rsqrt(jnp.sum(xf * xf, axis=-1, keepdims=True) + EPS)


def split_cols(p, sizes):
    outs, off = [], 0
    for s in sizes:
        outs.append(p[..., off:off + s])
        off += s
    return outs


def causal_dwconv(x, w):
    n_taps, seq = w.shape[0], x.shape[1]
    xp = jnp.pad(x, ((0, 0), (n_taps - 1, 0), (0, 0)))
    return sum(xp[:, i:i + seq] * w[i] for i in range(n_taps))


def to_chunks(t):
    bn, sn, h = t.shape[:3]
    t = t.reshape((bn, sn // CHUNK, CHUNK, h) + t.shape[3:])
    return jnp.moveaxis(t, 3, 1)


def from_chunks(t):
    bn, h, n, c, d = t.shape
    return jnp.moveaxis(t, 1, 3).reshape(bn, n * c, h, d)


def gated_delta_chunked(q, k, v, g, beta):
    dv = v.shape[-1]
    incl = jnp.tril(jnp.ones((CHUNK, CHUNK), bool))
    strict = jnp.tril(jnp.ones((CHUNK, CHUNK), bool), k=-1)
    gc = jnp.cumsum(g, axis=-1)
    decay = jnp.exp(jnp.where(incl, gc[..., :, None] - gc[..., None, :], -jnp.inf))
    kk = jnp.einsum('bhnid,bhnjd->bhnij', k, k)
    m = jnp.where(strict, beta[..., :, None] * kk * decay, 0.0)
    rhs = jnp.concatenate([v * beta[..., None], k * (beta * jnp.exp(gc))[..., None]], axis=-1)
    sol = lax.linalg.triangular_solve(m, rhs, left_side=True, lower=True, unit_diagonal=True)
    u, w = sol[..., :dv], sol[..., dv:]
    qk = jnp.where(incl, jnp.einsum('bhnid,bhnjd->bhnij', q, k) * decay, 0.0)
    q_s = q * jnp.exp(gc)[..., None]
    g_last = gc[..., -1]
    k_s = k * jnp.exp(g_last[..., None] - gc)[..., None]
    d_last = jnp.exp(g_last)

    def step(state, inp):
        qk_c, q_c, k_c, u_c, w_c, d_c = inp
        v_new = u_c - jnp.einsum('bhcd,bhdv->bhcv', w_c, state)
        o = jnp.einsum('bhcd,bhdv->bhcv', q_c, state) + jnp.einsum('bhij,bhjv->bhiv', qk_c, v_new)
        state = state * d_c[..., None, None] + jnp.einsum('bhcd,bhcv->bhdv', k_c, v_new)
        return state, o

    xs = tuple(jnp.moveaxis(t, 2, 0) for t in (qk, q_s, k_s, u, w, d_last))
    s0 = jnp.zeros(q.shape[:2] + (q.shape[-1], dv), jnp.float32)
    _, o = lax.scan(step, s0, xs)
    return jnp.moveaxis(o, 0, 2)


def gla_chunked(q, k, v, g):
    dv = v.shape[-1]
    incl = jnp.tril(jnp.ones((CHUNK, CHUNK), bool))
    b = jnp.cumsum(g, axis=3)
    b_mid = b[:, :, :, CHUNK // 2:CHUNK // 2 + 1]
    q_i = q * jnp.exp(b - b_mid)
    k_i = k * jnp.exp(b_mid - b)
    att = jnp.where(incl, jnp.einsum('bhnid,bhnjd->bhnij', q_i, k_i), 0.0)
    o_intra = jnp.einsum('bhnij,bhnjv->bhniv', att, v)
    b_last = b[:, :, :, -1:]
    q_s = q * jnp.exp(b)
    k_s = k * jnp.exp(b_last - b)
    d_last = jnp.exp(b_last[:, :, :, 0])

    def step(state, inp):
        q_c, k_c, v_c, d_c = inp
        o = jnp.einsum('bhcd,bhdv->bhcv', q_c, state)
        state = state * d_c[..., None] + jnp.einsum('bhcd,bhcv->bhdv', k_c, v_c)
        return state, o

    xs = tuple(jnp.moveaxis(t, 2, 0) for t in (q_s, k_s, v, d_last))
    s0 = jnp.zeros(q.shape[:2] + (q.shape[-1], dv), jnp.float32)
    _, o_inter = lax.scan(step, s0, xs)
    return o_intra + jnp.moveaxis(o_inter, 0, 2)


def even_mixer(h, w_in, conv_w, a_log, dt_bias, gdn_norm, gla_w_a2, gla_b_a2, gla_norm, w_out):
    bn, sn, _ = h.shape
    f32 = jnp.float32
    p = h @ w_in
    gq, gk, gv, gz, gb, ga, lq, lk, lv, lr, lg = split_cols(p, IN0_SIZES)
    qkv = jax.nn.silu(causal_dwconv(jnp.concatenate([gq, gk, gv], axis=-1), conv_w))
    cq, ck, cv = split_cols(qkv, (GDN_QK, GDN_QK, GDN_V))
    q = l2norm(cq.reshape(bn, sn, GDN_HEADS, GDN_DK)) * GDN_DK ** -0.5
    k = l2norm(ck.reshape(bn, sn, GDN_HEADS, GDN_DK))
    v = cv.reshape(bn, sn, GDN_HEADS, GDN_DV).astype(f32)
    beta = jax.nn.sigmoid(gb.astype(f32))
    g = -jnp.exp(a_log.astype(f32)) * jax.nn.softplus(ga.astype(f32) + dt_bias.astype(f32))
    o_a = gated_delta_chunked(to_chunks(q), to_chunks(k), to_chunks(v), to_chunks(g), to_chunks(beta))
    o_a = rmsnorm(from_chunks(o_a).astype(h.dtype), gdn_norm) * jax.nn.silu(gz.reshape(bn, sn, GDN_HEADS, GDN_DV))
    q = lq.reshape(bn, sn, GLA_HEADS, GLA_DK).astype(f32) * GLA_DK ** -0.5
    k = lk.reshape(bn, sn, GLA_HEADS, GLA_DK).astype(f32)
    v = lv.reshape(bn, sn, GLA_HEADS, GLA_DV).astype(f32)
    glog = jax.nn.log_sigmoid((lg @ gla_w_a2 + gla_b_a2).astype(f32)) / GLA_GATE_NORM
    glog = glog.reshape(bn, sn, GLA_HEADS, GLA_DK)
    o_b = gla_chunked(to_chunks(q), to_chunks(k), to_chunks(v), to_chunks(glog))
    o_b = rmsnorm(from_chunks(o_b).astype(h.dtype), gla_norm) * jax.nn.silu(lr.reshape(bn, sn, GLA_HEADS, GLA_DV))
    mix = jnp.concatenate([o_a.reshape(bn, sn, GDN_V), o_b.reshape(bn, sn, GLA_V)], axis=-1)
    return mix @ w_out


def odd_mixer(h, w_in, conv_w, conv_b, w_ra, b_ra, w_ri, b_ri, lam, w_out):
    bn, sn, _ = h.shape
    f32 = jnp.float32
    p = h @ w_in
    xb, gb = split_cols(p, (LRU_WIDTH, LRU_WIDTH))
    xb = causal_dwconv(xb, conv_w) + conv_b
    xblk = xb.reshape(bn, sn, LRU_BLOCKS, LRU_BLOCK)
    r = jax.nn.sigmoid((jnp.einsum('bsgi,gij->bsgj', xblk, w_ra) + b_ra).astype(f32)).reshape(bn, sn, LRU_WIDTH)
    i = jax.nn.sigmoid((jnp.einsum('bsgi,gij->bsgj', xblk, w_ri) + b_ri).astype(f32)).reshape(bn, sn, LRU_WIDTH)
    log_a = -LRU_C * r * jax.nn.softplus(-lam.astype(f32))
    a = jnp.exp(log_a)
    u = jnp.sqrt(-jnp.expm1(2.0 * log_a)) * (i * xb.astype(f32))

    def combine(e1, e2):
        a1, b1 = e1
        a2, b2 = e2
        return (a1 * a2, a2 * b1 + b2)

    _, hseq = lax.associative_scan(combine, (a, u), axis=1)
    return (jax.nn.gelu(gb) * hseq.astype(h.dtype)) @ w_out


def moe(h, router_w, router_b, w_gu, b_gu, w_dn, b_dn):
    bn, sn, d = h.shape
    n_tok = bn * sn
    n_asg = n_tok * TOP_K
    xf = h.reshape(n_tok, d)
    logits = (xf @ router_w + router_b).astype(jnp.float32)
    top_val, top_idx = lax.top_k(logits, TOP_K)
    gates = jax.nn.softmax(top_val, axis=-1)
    flat_e = top_idx.reshape(-1).astype(jnp.int32)
    flat_tok = jnp.arange(n_asg, dtype=jnp.int32) // TOP_K
    flat_w = gates.reshape(-1)
    order = jnp.argsort(flat_e)
    se, stok, sw = flat_e[order], flat_tok[order], flat_w[order]
    counts = jnp.zeros((N_EXPERTS,), jnp.int32).at[flat_e].add(1)
    padded = ((counts + MOE_BLOCK - 1) // MOE_BLOCK) * MOE_BLOCK
    start = jnp.cumsum(counts) - counts
    pend = jnp.cumsum(padded)
    pstart = pend - padded
    dest = pstart[se] + (jnp.arange(n_asg, dtype=jnp.int32) - start[se])
    n_buf = n_asg + N_EXPERTS * MOE_BLOCK
    n_blk = n_buf // MOE_BLOCK
    tok_buf = jnp.full((n_buf,), n_tok, jnp.int32).at[dest].set(stok)
    w_buf = jnp.zeros((n_buf,), jnp.float32).at[dest].set(sw)
    blk_e = jnp.minimum(jnp.searchsorted(pend, jnp.arange(n_blk, dtype=jnp.int32) * MOE_BLOCK, side='right'), N_EXPERTS - 1)
    xpad = jnp.concatenate([xf, jnp.zeros((1, d), xf.dtype)], axis=0)

    def body(out, blk):
        toks = lax.dynamic_slice(tok_buf, (blk * MOE_BLOCK,), (MOE_BLOCK,))
        wts = lax.dynamic_slice(w_buf, (blk * MOE_BLOCK,), (MOE_BLOCK,))
        e = blk_e[blk]
        gu = xpad[toks] @ w_gu[e] + b_gu[e]
        gate = jnp.minimum(gu[:, :D_FF], SWIGLU_LIMIT)
        up = jnp.clip(gu[:, D_FF:], -SWIGLU_LIMIT, SWIGLU_LIMIT)
        act = (up + 1.0) * gate * jax.nn.sigmoid(SWIGLU_ALPHA * gate)
        y = act @ w_dn[e] + b_dn[e]
        out = out.at[toks].add((y * wts[:, None]).astype(out.dtype))
        return out, None

    out, _ = lax.scan(body, jnp.zeros((n_tok + 1, d), h.dtype), jnp.arange(n_blk, dtype=jnp.int32))
    return out[:n_tok].reshape(bn, sn, d)


def setup_inputs(seed: int = 0) -> dict:
    key = jax.random.key(seed)
    ks = iter(jax.random.split(key, 40))
    f32 = jnp.float32

    def nrm(shape, scale):
        return jax.random.normal(next(ks), shape, f32) * scale

    def unif(shape, lo, hi):
        return jax.random.uniform(next(ks), shape, f32, lo, hi)

    d = D_MODEL
    dt = jnp.exp(unif((N_EVEN, GDN_HEADS), float(np.log(1e-3)), float(np.log(1e-1))))
    a0 = unif((N_ODD, LRU_WIDTH), 0.9, 0.999) ** (1.0 / LRU_C)
    return {
        'x': nrm((BATCH, SEQ, d), 1.0),
        'c': nrm((BATCH, d), 1.0),
        'ln_mix': 1.0 + nrm((DEPTH, d), 0.02),
        'ln_ffn': 1.0 + nrm((DEPTH, d), 0.02),
        'ln_final': 1.0 + nrm((d,), 0.02),
        'w_ada': nrm((DEPTH, d, 6 * d), 0.5 * d ** -0.5),
        'b_ada': nrm((DEPTH, 6 * d), 0.02),
        'e_w_in': nrm((N_EVEN, d, IN0_WIDTH), d ** -0.5),
        'e_conv_w': nrm((N_EVEN, CONV_W, GDN_QK * 2 + GDN_V), 0.5),
        'e_a_log': jnp.log(unif((N_EVEN, GDN_HEADS), 1.0, 16.0)),
        'e_dt_bias': dt + jnp.log(-jnp.expm1(-dt)),
        'e_gdn_norm': 1.0 + nrm((N_EVEN, GDN_DV), 0.02),
        'e_gla_w_a2': nrm((N_EVEN, GLA_RANK, GLA_QK), GLA_RANK ** -0.5),
        'e_gla_b_a2': 1.0 + nrm((N_EVEN, GLA_QK), 0.1),
        'e_gla_norm': 1.0 + nrm((N_EVEN, GLA_DV), 0.02),
        'e_w_out': nrm((N_EVEN, MIX0_WIDTH, d), MIX0_WIDTH ** -0.5),
        'o_w_in': nrm((N_ODD, d, 2 * LRU_WIDTH), d ** -0.5),
        'o_conv_w': nrm((N_ODD, CONV_W, LRU_WIDTH), 0.5),
        'o_conv_b': nrm((N_ODD, LRU_WIDTH), 0.02),
        'o_w_ra': nrm((N_ODD, LRU_BLOCKS, LRU_BLOCK, LRU_BLOCK), LRU_BLOCK ** -0.5),
        'o_b_ra': nrm((N_ODD, LRU_BLOCKS, LRU_BLOCK), 0.02),
        'o_w_ri': nrm((N_ODD, LRU_BLOCKS, LRU_BLOCK, LRU_BLOCK), LRU_BLOCK ** -0.5),
        'o_b_ri': nrm((N_ODD, LRU_BLOCKS, LRU_BLOCK), 0.02),
        'o_lam': jnp.log(a0) - jnp.log1p(-a0),
        'o_w_out': nrm((N_ODD, LRU_WIDTH, d), LRU_WIDTH ** -0.5),
        'router_w': nrm((DEPTH, d, N_EXPERTS), d ** -0.5),
        'router_b': nrm((DEPTH, N_EXPERTS), 0.01),
        'exp_w_gu': nrm((DEPTH, N_EXPERTS, d, 2 * D_FF), d ** -0.5),
        'exp_b_gu': nrm((DEPTH, N_EXPERTS, 2 * D_FF), 0.01),
        'exp_w_dn': nrm((DEPTH, N_EXPERTS, D_FF, d), D_FF ** -0.5),
        'exp_b_dn': nrm((DEPTH, N_EXPERTS, d), 0.01),
    }


def reference(x, c, ln_mix, ln_ffn, ln_final, w_ada, b_ada,
              e_w_in, e_conv_w, e_a_log, e_dt_bias, e_gdn_norm, e_gla_w_a2, e_gla_b_a2, e_gla_norm, e_w_out,
              o_w_in, o_conv_w, o_conv_b, o_w_ra, o_b_ra, o_w_ri, o_b_ri, o_lam, o_w_out,
              router_w, router_b, exp_w_gu, exp_b_gu, exp_w_dn, exp_b_dn):
    cs = jax.nn.silu(c)
    for layer in range(DEPTH):
        mod = cs @ w_ada[layer] + b_ada[layer]
        sh_m, sc_m, g_m, sh_f, sc_f, g_f = [m[:, None, :] for m in jnp.split(mod, 6, axis=-1)]
        h = rmsnorm(x, ln_mix[layer]) * (1.0 + sc_m) + sh_m
        j = layer // 2
        if layer % 2 == 0:
            y = even_mixer(h, e_w_in[j], e_conv_w[j], e_a_log[j], e_dt_bias[j], e_gdn_norm[j],
                           e_gla_w_a2[j], e_gla_b_a2[j], e_gla_norm[j], e_w_out[j])
        else:
            y = odd_mixer(h, o_w_in[j], o_conv_w[j], o_conv_b[j], o_w_ra[j], o_b_ra[j],
                          o_w_ri[j], o_b_ri[j], o_lam[j], o_w_out[j])
        x = x + g_m * y
        h = rmsnorm(x, ln_ffn[layer]) * (1.0 + sc_f) + sh_f
        x = x + g_f * moe(h, router_w[layer], router_b[layer], exp_w_gu[layer], exp_b_gu[layer],
                          exp_w_dn[layer], exp_b_dn[layer])
    return rmsnorm(x, ln_final)
```

```python
import functools

import jax
import jax.numpy as jnp
from jax import lax
from jax.experimental import pallas as pl
from jax.experimental.pallas import tpu as pltpu

F32 = jnp.float32
BF16 = jnp.bfloat16
HIGHEST = lax.Precision.HIGHEST

EPS = 1e-6
CONV_W = 4
CHUNK = 64
GDN_HEADS = 8
GDN_DK = 128
GDN_DV = 128
GLA_HEADS = 8
GLA_DK = 64
GLA_DV = 128
GLA_RANK = 16
GLA_GATE_NORM = 16.0
LRU_BLOCKS = 8
LRU_C = 8.0
N_EXPERTS = 32
TOP_K = 4
SWIGLU_LIMIT = 7.0
SWIGLU_ALPHA = 1.702
MOE_BLOCK = 256

LANES = 128
SUBLANES = 8
VMEM_LIMIT = 56 * 1024 * 1024


def _dot(a, b, precision=None):
    return jnp.dot(a, b, preferred_element_type=F32, precision=precision)


def _dot_nt(a, b, precision=None):
    return lax.dot_general(a, b, (((1,), (1,)), ((), ())), preferred_element_type=F32, precision=precision)


def _dot_tn(a, b):
    return lax.dot_general(a, b, (((0,), (0,)), ((), ())), preferred_element_type=F32)


def _sigmoid(x):
    return 1.0 / (1.0 + jnp.exp(-x))


def _softplus(x):
    return jnp.maximum(x, 0.0) + jnp.log(1.0 + jnp.exp(-jnp.abs(x)))


def _silu(x):
    return x * _sigmoid(x)


def _params(*sem):
    return pltpu.CompilerParams(dimension_semantics=sem, vmem_limit_bytes=VMEM_LIMIT)


def _ada_kernel(c_ref, w_ref, b_ref, o_ref):
    c = c_ref[...]
    o_ref[0] = _dot(_silu(c), w_ref[0], precision=HIGHEST) + b_ref[0]


def _ada_mod(c, w_ada, b_ada):
    depth, d, n6 = w_ada.shape
    bn = c.shape[0]
    rows = -(-bn // SUBLANES) * SUBLANES
    cp = jnp.zeros((rows, d), F32).at[:bn].set(c)
    tn = 768
    out = pl.pallas_call(
        _ada_kernel,
        grid=(depth, n6 // tn),
        in_specs=[
            pl.BlockSpec((rows, d), lambda l, j: (0, 0)),
            pl.BlockSpec((1, d, tn), lambda l, j: (l, 0, j)),
            pl.BlockSpec((1, 1, tn), lambda l, j: (l, 0, j)),
        ],
        out_specs=pl.BlockSpec((1, rows, tn), lambda l, j: (l, 0, j)),
        out_shape=jax.ShapeDtypeStruct((depth, rows, n6), F32),
        compiler_params=_params("arbitrary", "arbitrary"),
        name="ada_mod",
    )(cp, w_ada, b_ada.reshape(depth, 1, n6))
    return out[:, :bn].reshape(depth, bn, 6, d)


def _norm_mod(x, ln, shift, scale):
    y = x * lax.rsqrt(jnp.mean(x * x, axis=-1, keepdims=True) + EPS) * ln
    return y * (1.0 + scale) + shift


def _norm_proj_kernel(x_ref, ln_ref, mod_ref, w_ref, *rest, has_small):
    if has_small:
        ws_ref, o_ref, os_ref, h_scr = rest
    else:
        o_ref, h_scr = rest
    j = pl.program_id(1)

    @pl.when(j == 0)
    def _():
        m = mod_ref[0]
        h = _norm_mod(x_ref[...], ln_ref[...], m[0:1], m[1:2]).astype(BF16)
        h_scr[...] = h
        if has_small:
            os_ref[...] = _dot(h, ws_ref[...])

    o_ref[...] = _dot(h_scr[...], w_ref[...]).astype(o_ref.dtype)


def _norm_proj(x, ln, mod, w, w_small, seq):
    t, d = x.shape
    n = w.shape[1]
    tm, tn = 512, 1024
    per_b = seq // tm
    has_small = w_small is not None
    in_specs = [
        pl.BlockSpec((tm, d), lambda i, j: (i, 0)),
        pl.BlockSpec((1, d), lambda i, j: (0, 0)),
        pl.BlockSpec((1, 6, d), lambda i, j: (i // per_b, 0, 0)),
        pl.BlockSpec((d, tn), lambda i, j: (0, j)),
    ]
    out_specs = [pl.BlockSpec((tm, tn), lambda i, j: (i, j))]
    out_shape = [jax.ShapeDtypeStruct((t, n), BF16)]
    args = [x, ln.reshape(1, d), mod, w]
    if has_small:
        in_specs.append(pl.BlockSpec((d, LANES), lambda i, j: (0, 0)))
        out_specs.append(pl.BlockSpec((tm, LANES), lambda i, j: (i, 0)))
        out_shape.append(jax.ShapeDtypeStruct((t, LANES), F32))
        args.append(w_small)
    outs = pl.pallas_call(
        functools.partial(_norm_proj_kernel, has_small=has_small),
        grid=(t // tm, n // tn),
        in_specs=in_specs,
        out_specs=out_specs,
        out_shape=out_shape,
        scratch_shapes=[pltpu.VMEM((tm, d), BF16)],
        compiler_params=_params("arbitrary", "arbitrary"),
        name="norm_proj",
    )(*args)
    return outs if has_small else outs[0]


def _gdn_gate_kernel(ga_ref, gb_ref, alog_ref, dtb_ref, gc_ref, beta_ref, eg_ref, ek_ref):
    g = -jnp.exp(alog_ref[...]) * _softplus(ga_ref[...] + dtb_ref[...])
    ci = lax.broadcasted_iota(jnp.int32, (LANES, LANES), 0)
    cj = lax.broadcasted_iota(jnp.int32, (LANES, LANES), 1)
    same = (ci // CHUNK) == (cj // CHUNK)
    upper = jnp.where(same & (ci <= cj), 1.0, 0.0).astype(F32)
    allc = jnp.where(same, 1.0, 0.0).astype(F32)
    gc = _dot(g, upper, precision=HIGHEST)
    g_last = _dot(g, allc, precision=HIGHEST)
    gc_ref[...] = gc
    beta_ref[...] = _sigmoid(gb_ref[...])
    eg_ref[...] = jnp.exp(gc)
    ek_ref[...] = jnp.exp(g_last - gc)


def _gdn_gates(small, a_log, dt_bias):
    t = small.shape[0]
    nb = t // LANES
    rows = GDN_HEADS * nb
    gb_t = small[:, 0:GDN_HEADS].T.reshape(rows, LANES)
    ga_t = small[:, GDN_HEADS:2 * GDN_HEADS].T.reshape(rows, LANES)
    alog_c = jnp.repeat(a_log.astype(F32), nb).reshape(rows, 1)
    dtb_c = jnp.repeat(dt_bias.astype(F32), nb).reshape(rows, 1)
    full = pl.BlockSpec((rows, LANES), lambda: (0, 0))
    col1 = pl.BlockSpec((rows, 1), lambda: (0, 0))
    gc, beta, eg, ek = pl.pallas_call(
        _gdn_gate_kernel,
        in_specs=[full, full, col1, col1],
        out_specs=[full] * 4,
        out_shape=[jax.ShapeDtypeStruct((rows, LANES), F32)] * 4,
        compiler_params=pltpu.CompilerParams(vmem_limit_bytes=VMEM_LIMIT),
        name="gdn_gates",
    )(ga_t, gb_t, alog_c, dtb_c)
    to_col = lambda a: a.reshape(GDN_HEADS, t).T
    col = jnp.concatenate([to_col(beta), to_col(gc), to_col(eg), to_col(ek)], axis=1)
    row = gc.reshape(GDN_HEADS, t // CHUNK, CHUNK).transpose(1, 0, 2)
    return col, row


def _gdn_kernel(q_ref, k_ref, v_ref, z_ref, col_ref, row_ref, cw_ref, nw_ref, o_ref,
                s_scr, tail_scr, qkv_scr, *, ts):
    width = GDN_HEADS * GDN_DK

    @pl.when(pl.program_id(1) == 0)
    def _():
        s_scr[...] = jnp.zeros_like(s_scr)
        tail_scr[...] = jnp.zeros_like(tail_scr)

    strip = 256
    for part, ref in enumerate((q_ref, k_ref, v_ref)):
        for s0 in range(0, width, strip):
            c0 = part * width + s0
            cur = ref[:, s0:s0 + strip].astype(F32)
            full = jnp.concatenate([tail_scr[:, c0:c0 + strip], cur], axis=0)
            w = cw_ref[:, c0:c0 + strip]
            acc = full * w[CONV_W - 1:CONV_W]
            for i in range(CONV_W - 1):
                acc = acc + pltpu.roll(full, CONV_W - 1 - i, axis=0) * w[i:i + 1]
            y = acc[SUBLANES:]
            qkv_scr[:, c0:c0 + strip] = _silu(y)
            tail_scr[:, c0:c0 + strip] = cur[ts - SUBLANES:]

    ri = lax.broadcasted_iota(jnp.int32, (CHUNK, CHUNK), 0)
    rj = lax.broadcasted_iota(jnp.int32, (CHUNK, CHUNK), 1)
    incl = ri >= rj
    strict = ri > rj
    eye = jnp.where(ri == rj, 1.0, 0.0).astype(F32)
    nw = nw_ref[...]

    def chunk_body(c, carry):
        r0 = pl.multiple_of(c * CHUNK, CHUNK)
        colc = col_ref[pl.ds(r0, CHUNK), :]
        rowc = row_ref[c]
        for h in range(GDN_HEADS):
            lo, hi = h * GDN_DK, (h + 1) * GDN_DK
            q = qkv_scr[pl.ds(r0, CHUNK), lo:hi]
            k = qkv_scr[pl.ds(r0, CHUNK), width + lo:width + hi]
            v = qkv_scr[pl.ds(r0, CHUNK), 2 * width + lo:2 * width + hi]
            q = q * lax.rsqrt(jnp.sum(q * q, axis=-1, keepdims=True) + EPS) * (GDN_DK ** -0.5)
            k = k * lax.rsqrt(jnp.sum(k * k, axis=-1, keepdims=True) + EPS)
            beta = colc[:, h:h + 1]
            gc = colc[:, 8 + h:9 + h]
            eg = colc[:, 16 + h:17 + h]
            ek = colc[:, 24 + h:25 + h]
            gr = rowc[h:h + 1, :]
            decay = jnp.exp(jnp.where(incl, gc - gr, -jnp.inf))
            kb = k.astype(BF16)
            qb = q.astype(BF16)
            kk = _dot_nt(kb, kb)
            npow = jnp.where(strict, -(beta * kk * decay), 0.0)
            ainv = eye + npow
            for _ in range(5):
                npow = _dot(npow, npow, precision=HIGHEST)
                ainv = ainv + _dot(ainv, npow, precision=HIGHEST)
            ab = ainv.astype(BF16)
            u = _dot(ab, (v * beta).astype(BF16))
            w = _dot(ab, (k * (beta * eg)).astype(BF16))
            qk = jnp.where(incl, _dot_nt(qb, kb) * decay, 0.0)
            q_s = q * eg
            k_s = k * ek
            d_last = eg[CHUNK - 1:CHUNK, :]
            state = s_scr[h]
            sb = state.astype(BF16)
            v_new = u - _dot(w.astype(BF16), sb)
            vb = v_new.astype(BF16)
            o = _dot(q_s.astype(BF16), sb) + _dot(qk.astype(BF16), vb)
            s_scr[h] = state * d_last + _dot_tn(k_s.astype(BF16), vb)
            o = o * lax.rsqrt(jnp.mean(o * o, axis=-1, keepdims=True) + EPS) * nw
            z = z_ref[pl.ds(r0, CHUNK), lo:hi].astype(F32)
            o_ref[pl.ds(r0, CHUNK), lo:hi] = (o * _silu(z)).astype(o_ref.dtype)
        return carry

    lax.fori_loop(0, ts // CHUNK, chunk_body, 0)


def _gdn(p, col, row, conv_w, norm_w, bn, seq):
    t = p.shape[0]
    ts = 256
    per_b = seq // ts
    width = GDN_HEADS * GDN_DK
    tok = lambda b, s: b * per_b + s
    pspec = lambda cb: pl.BlockSpec((ts, width), lambda b, s: (tok(b, s), cb))
    return pl.pallas_call(
        functools.partial(_gdn_kernel, ts=ts),
        grid=(bn, per_b),
        in_specs=[
            pspec(0), pspec(1), pspec(2), pspec(3),
            pl.BlockSpec((ts, 4 * GDN_HEADS), lambda b, s: (tok(b, s), 0)),
            pl.BlockSpec((ts // CHUNK, GDN_HEADS, CHUNK), lambda b, s: (tok(b, s), 0, 0)),
            pl.BlockSpec((CONV_W, 3 * width), lambda b, s: (0, 0)),
            pl.BlockSpec((1, GDN_DV), lambda b, s: (0, 0)),
        ],
        out_specs=pl.BlockSpec((ts, width), lambda b, s: (tok(b, s), 0)),
        out_shape=jax.ShapeDtypeStruct((t, width), BF16),
        scratch_shapes=[
            pltpu.VMEM((GDN_HEADS, GDN_DK, GDN_DV), F32),
            pltpu.VMEM((SUBLANES, 3 * width), F32),
            pltpu.VMEM((ts, 3 * width), F32),
        ],
        compiler_params=_params("arbitrary", "arbitrary"),
        name="gdn",
    )(p, p, p, p, col, row, conv_w, norm_w.reshape(1, GDN_DV))


def _gla_kernel(q_ref, k_ref, v_ref, r_ref, sm_ref, wa_ref, ba_ref, nw_ref, o_ref, st_scr, g_scr, *, ts):
    pair_w = 2 * GLA_DK
    n_pairs = GLA_HEADS // 2

    @pl.when(pl.program_id(1) == 0)
    def _():
        st_scr[...] = jnp.zeros_like(st_scr)

    logits = _dot(sm_ref[...], wa_ref[...], precision=HIGHEST) + ba_ref[...]
    g_scr[...] = -_softplus(-logits) / GLA_GATE_NORM

    ri = lax.broadcasted_iota(jnp.int32, (CHUNK, CHUNK), 0)
    rj = lax.broadcasted_iota(jnp.int32, (CHUNK, CHUNK), 1)
    incl = ri >= rj
    lower = jnp.where(incl, 1.0, 0.0).astype(F32)
    lane = lax.broadcasted_iota(jnp.int32, (CHUNK, pair_w), 1)
    sr = lax.broadcasted_iota(jnp.int32, (2 * GLA_DV, pair_w), 0)
    sc = lax.broadcasted_iota(jnp.int32, (2 * GLA_DV, pair_w), 1)
    live = (sr // GLA_DV) == (sc // GLA_DK)
    nw = nw_ref[...]

    def chunk_body(c, carry):
        r0 = pl.multiple_of(c * CHUNK, CHUNK)
        for p in range(n_pairs):
            lo, hi = p * pair_w, (p + 1) * pair_w
            g = g_scr[pl.ds(r0, CHUNK), lo:hi]
            b = _dot(lower, g, precision=HIGHEST)
            b_mid = b[CHUNK // 2:CHUNK // 2 + 1, :]
            b_last = b[CHUNK - 1:CHUNK, :]
            q = q_ref[pl.ds(r0, CHUNK), lo:hi].astype(F32) * (GLA_DK ** -0.5)
            k = k_ref[pl.ds(r0, CHUNK), lo:hi].astype(F32)
            q_i = q * jnp.exp(b - b_mid)
            k_i = (k * jnp.exp(b_mid - b)).astype(BF16)
            q_s = (q * jnp.exp(b)).astype(BF16)
            k_s = (k * jnp.exp(b_last - b)).astype(BF16)
            d_last = jnp.exp(b_last)
            vp = v_ref[pl.ds(r0, CHUNK), 2 * p * GLA_DV:(2 * p + 2) * GLA_DV]
            st = st_scr[p]
            o_inter = _dot_nt(q_s, st.astype(BF16))
            upd = _dot_tn(vp, k_s)
            st_scr[p] = st * d_last + jnp.where(live, upd, 0.0)
            for hh in range(2):
                head = 2 * p + hh
                q_h = jnp.where((lane // GLA_DK) == hh, q_i, 0.0).astype(BF16)
                att = jnp.where(incl, _dot_nt(q_h, k_i), 0.0)
                v_h = vp[:, hh * GLA_DV:(hh + 1) * GLA_DV]
                o = _dot(att.astype(BF16), v_h) + o_inter[:, hh * GLA_DV:(hh + 1) * GLA_DV]
                o = o * lax.rsqrt(jnp.mean(o * o, axis=-1, keepdims=True) + EPS) * nw
                rg = r_ref[pl.ds(r0, CHUNK), head * GLA_DV:(head + 1) * GLA_DV].astype(F32)
                o_ref[pl.ds(r0, CHUNK), head * GLA_DV:(head + 1) * GLA_DV] = (o * _silu(rg)).astype(o_ref.dtype)
        return carry

    lax.fori_loop(0, ts // CHUNK, chunk_body, 0)


def _gla(p, small, w_a2, b_a2, norm_w, bn, seq):
    t = p.shape[0]
    ts = 256
    per_b = seq // ts
    qk_w = GLA_HEADS * GLA_DK
    v_w = GLA_HEADS * GLA_DV
    gdn_cols = 4 * GDN_HEADS * GDN_DK
    tok = lambda b, s: b * per_b + s
    wa_pad = jnp.zeros((LANES, qk_w), F32).at[2 * GDN_HEADS:2 * GDN_HEADS + GLA_RANK].set(w_a2.astype(F32))
    return pl.pallas_call(
        functools.partial(_gla_kernel, ts=ts),
        grid=(bn, per_b),
        in_specs=[
            pl.BlockSpec((ts, qk_w), lambda b, s: (tok(b, s), gdn_cols // qk_w)),
            pl.BlockSpec((ts, qk_w), lambda b, s: (tok(b, s), gdn_cols // qk_w + 1)),
            pl.BlockSpec((ts, v_w), lambda b, s: (tok(b, s), (gdn_cols + 2 * qk_w) // v_w)),
            pl.BlockSpec((ts, v_w), lambda b, s: (tok(b, s), (gdn_cols + 2 * qk_w) // v_w + 1)),
            pl.BlockSpec((ts, LANES), lambda b, s: (tok(b, s), 0)),
            pl.BlockSpec((LANES, qk_w), lambda b, s: (0, 0)),
            pl.BlockSpec((1, qk_w), lambda b, s: (0, 0)),
            pl.BlockSpec((1, GLA_DV), lambda b, s: (0, 0)),
        ],
        out_specs=pl.BlockSpec((ts, v_w), lambda b, s: (tok(b, s), 0)),
        out_shape=jax.ShapeDtypeStruct((t, v_w), BF16),
        scratch_shapes=[
            pltpu.VMEM((GLA_HEADS // 2, 2 * GLA_DV, 2 * GLA_DK), F32),
            pltpu.VMEM((ts, qk_w), F32),
        ],
        compiler_params=_params("arbitrary", "arbitrary"),
        name="gla",
    )(p, p, p, p, small, wa_pad, b_a2.reshape(1, qk_w).astype(F32), norm_w.reshape(1, GLA_DV))


def _gelu_tanh(x):
    return 0.5 * x * (1.0 + jnp.tanh(0.7978845608028654 * (x + 0.044715 * x * x * x)))


def _lru_kernel(x_ref, g_ref, cw_ref, cb_ref, wra_ref, bra_ref, wri_ref, bri_ref, lam_ref, o_ref,
                tail_scr, h_scr, a_scr, u_scr, *, ts):
    width = x_ref.shape[1]
    blk = width // LRU_BLOCKS

    @pl.when(pl.program_id(1) == 0)
    def _():
        tail_scr[...] = jnp.zeros_like(tail_scr)
        h_scr[...] = jnp.zeros_like(h_scr)

    for gi in range(LRU_BLOCKS):
        lo, hi = gi * blk, (gi + 1) * blk
        cur = x_ref[:, lo:hi].astype(F32)
        full = jnp.concatenate([tail_scr[:, lo:hi], cur], axis=0)
        w = cw_ref[:, lo:hi]
        acc = full * w[CONV_W - 1:CONV_W]
        for i in range(CONV_W - 1):
            acc = acc + pltpu.roll(full, CONV_W - 1 - i, axis=0) * w[i:i + 1]
        xc = acc[SUBLANES:] + cb_ref[:, lo:hi]
        tail_scr[:, lo:hi] = cur[ts - SUBLANES:]
        xb = xc.astype(BF16)
        r = _sigmoid(_dot(xb, wra_ref[gi]) + bra_ref[:, lo:hi])
        ig = _sigmoid(_dot(xb, wri_ref[gi]) + bri_ref[:, lo:hi])
        log_a = -LRU_C * r * _softplus(-lam_ref[:, lo:hi])
        a = jnp.exp(log_a)
        a_scr[:, lo:hi] = a
        u_scr[:, lo:hi] = jnp.sqrt(1.0 - a * a) * (ig * xc)

    row = lax.broadcasted_iota(jnp.int32, (SUBLANES, width), 0)

    def scan_body(rb, h_prev):
        r0 = pl.multiple_of(rb * SUBLANES, SUBLANES)
        a = a_scr[pl.ds(r0, SUBLANES), :]
        u = u_scr[pl.ds(r0, SUBLANES), :]
        for d in (1, 2, 4):
            a_sh = jnp.where(row >= d, pltpu.roll(a, d, axis=0), 1.0)
            u_sh = jnp.where(row >= d, pltpu.roll(u, d, axis=0), 0.0)
            u = a * u_sh + u
            a = a * a_sh
        h = u + a * h_prev
        gate = g_ref[pl.ds(r0, SUBLANES), :].astype(F32)
        u_scr[pl.ds(r0, SUBLANES), :] = _gelu_tanh(gate) * h
        return h[SUBLANES - 1:SUBLANES, :]

    h_last = lax.fori_loop(0, ts // SUBLANES, scan_body, h_scr[...])
    h_scr[...] = h_last
    o_ref[...] = u_scr[...].astype(o_ref.dtype)


def _lru(p, conv_w, conv_b, w_ra, b_ra, w_ri, b_ri, lam, bn, seq):
    t = p.shape[0]
    width = p.shape[1] // 2
    blk = width // LRU_BLOCKS
    ts = 256
    per_b = seq // ts
    tok = lambda b, s: b * per_b + s
    vec = lambda a: a.reshape(1, width).astype(F32)
    vspec = pl.BlockSpec((1, width), lambda b, s: (0, 0))
    wspec = pl.BlockSpec((LRU_BLOCKS, blk, blk), lambda b, s: (0, 0, 0))
    return pl.pallas_call(
        functools.partial(_lru_kernel, ts=ts),
        grid=(bn, per_b),
        in_specs=[
            pl.BlockSpec((ts, width), lambda b, s: (tok(b, s), 0)),
            pl.BlockSpec((ts, width), lambda b, s: (tok(b, s), 1)),
            pl.BlockSpec((CONV_W, width), lambda b, s: (0, 0)),
            vspec, wspec, vspec, wspec, vspec, vspec,
        ],
        out_specs=pl.BlockSpec((ts, width), lambda b, s: (tok(b, s), 0)),
        out_shape=jax.ShapeDtypeStruct((t, width), BF16),
        scratch_shapes=[
            pltpu.VMEM((SUBLANES, width), F32),
            pltpu.VMEM((1, width), F32),
            pltpu.VMEM((ts, width), F32),
            pltpu.VMEM((ts, width), F32),
        ],
        compiler_params=_params("arbitrary", "arbitrary"),
        name="lru",
    )(p, p, conv_w, vec(conv_b), w_ra.astype(BF16), vec(b_ra), w_ri.astype(BF16), vec(b_ri), vec(lam))


def _out_proj_kernel(*refs, n_in):
    a_refs = refs[:n_in]
    w_refs = refs[n_in:2 * n_in]
    x_ref, mod_ref, ln_ref, rw_ref, rb_ref, xo_ref, h_ref, ti_ref, tw_ref = refs[2 * n_in:]
    y = _dot(a_refs[0][...], w_refs[0][...])
    for a_ref, w_ref in zip(a_refs[1:], w_refs[1:]):
        y = y + _dot(a_ref[...], w_ref[...])
    m = mod_ref[0]
    xn = x_ref[...] + m[2:3] * y
    xo_ref[...] = xn
    h = _norm_mod(xn, ln_ref[...], m[3:4], m[4:5])
    h_ref[...] = h
    logits = _dot(h, rw_ref[...], precision=HIGHEST) + rb_ref[...]
    lane = lax.broadcasted_iota(jnp.int32, logits.shape, 1)
    cur = jnp.where(lane < N_EXPERTS, logits, -jnp.inf)
    vals, idxs = [], []
    for _ in range(TOP_K):
        mx = jnp.max(cur, axis=-1, keepdims=True)
        ix = jnp.min(jnp.where(cur == mx, lane, LANES), axis=-1, keepdims=True)
        vals.append(mx)
        idxs.append(ix)
        cur = jnp.where(lane == ix, -jnp.inf, cur)
    es = [jnp.exp(v - vals[0]) for v in vals]
    den = es[0]
    for e in es[1:]:
        den = den + e
    ti = jnp.zeros(logits.shape, jnp.int32)
    tw = jnp.zeros(logits.shape, F32)
    for kk in range(TOP_K):
        ti = jnp.where(lane == kk, idxs[kk], ti)
        tw = jnp.where(lane == kk, es[kk] / den, tw)
    ti_ref[...] = ti
    tw_ref[...] = tw


def _out_proj(acts, weights, x, mod, ln_ffn, router_w, router_b, seq):
    t, d = x.shape
    tm = 256
    per_b = seq // tm
    n_in = len(acts)
    rw = jnp.zeros((d, LANES), F32).at[:, :N_EXPERTS].set(router_w.astype(F32))
    rb = jnp.zeros((1, LANES), F32).at[0, :N_EXPERTS].set(router_b.astype(F32))
    in_specs = [pl.BlockSpec((tm, a.shape[1]), lambda i: (i, 0)) for a in acts]
    in_specs += [pl.BlockSpec(w.shape, lambda i: (0, 0)) for w in weights]
    in_specs += [
        pl.BlockSpec((tm, d), lambda i: (i, 0)),
        pl.BlockSpec((1, 6, d), lambda i: (i // per_b, 0, 0)),
        pl.BlockSpec((1, d), lambda i: (0, 0)),
        pl.BlockSpec((d, LANES), lambda i: (0, 0)),
        pl.BlockSpec((1, LANES), lambda i: (0, 0)),
    ]
    row_d = pl.BlockSpec((tm, d), lambda i: (i, 0))
    row_l = pl.BlockSpec((tm, LANES), lambda i: (i, 0))
    return pl.pallas_call(
        functools.partial(_out_proj_kernel, n_in=n_in),
        grid=(t // tm,),
        in_specs=in_specs,
        out_specs=[row_d, row_d, row_l, row_l],
        out_shape=[
            jax.ShapeDtypeStruct((t, d), F32),
            jax.ShapeDtypeStruct((t, d), F32),
            jax.ShapeDtypeStruct((t, LANES), jnp.int32),
            jax.ShapeDtypeStruct((t, LANES), F32),
        ],
        compiler_params=_params("arbitrary"),
        name="out_proj",
    )(*acts, *weights, x, mod, ln_ffn.reshape(1, d), rw, rb)


def _moe_kernel(blk_e_ref, tok_ref, dst_ref, h_hbm, wgu_ref, bgu_ref, wdn_ref, bdn_ref, y_hbm,
                xbuf, ybuf, sem_in, sem_out):
    d_ff = wdn_ref.shape[1]

    def gather_copy(r, tok):
        return pltpu.make_async_copy(h_hbm.at[pl.ds(tok, 1)], xbuf.at[pl.ds(r, 1)], sem_in.at[0])

    def scatter_copy(r, dst):
        return pltpu.make_async_copy(ybuf.at[pl.ds(r, 1)], y_hbm.at[pl.ds(dst, 1)], sem_out.at[0])

    def start_gather(r, c):
        gather_copy(r, tok_ref[0, 0, r]).start()
        return c

    def wait_gather(r, c):
        gather_copy(r, 0).wait()
        return c

    lax.fori_loop(0, MOE_BLOCK, start_gather, 0)
    lax.fori_loop(0, MOE_BLOCK, wait_gather, 0)

    x = xbuf[...].astype(BF16)
    gu = _dot(x, wgu_ref[0]) + bgu_ref[0]
    gate = jnp.minimum(gu[:, :d_ff], SWIGLU_LIMIT)
    up = jnp.clip(gu[:, d_ff:], -SWIGLU_LIMIT, SWIGLU_LIMIT)
    act = (up + 1.0) * gate * _sigmoid(SWIGLU_ALPHA * gate)
    ybuf[...] = _dot(act.astype(BF16), wdn_ref[0]) + bdn_ref[0]

    def start_scatter(r, c):
        scatter_copy(r, dst_ref[0, 0, r]).start()
        return c

    def wait_scatter(r, c):
        scatter_copy(r, 0).wait()
        return c

    lax.fori_loop(0, MOE_BLOCK, start_scatter, 0)
    lax.fori_loop(0, MOE_BLOCK, wait_scatter, 0)


def _moe_experts(h, tok_buf, dst_buf, blk_e, w_gu, b_gu, w_dn, b_dn):
    t, d = h.shape
    n_blk = blk_e.shape[0]
    d_ff = w_dn.shape[1]
    ids = pl.BlockSpec((1, 1, MOE_BLOCK), lambda b, e: (b, 0, 0), memory_space=pltpu.SMEM)
    grid_spec = pltpu.PrefetchScalarGridSpec(
        num_scalar_prefetch=1,
        grid=(n_blk,),
        in_specs=[
            ids, ids,
            pl.BlockSpec(memory_space=pl.ANY),
            pl.BlockSpec((1, d, 2 * d_ff), lambda b, e: (e[b], 0, 0)),
            pl.BlockSpec((1, 1, 2 * d_ff), lambda b, e: (e[b], 0, 0)),
            pl.BlockSpec((1, d_ff, d), lambda b, e: (e[b], 0, 0)),
            pl.BlockSpec((1, 1, d), lambda b, e: (e[b], 0, 0)),
        ],
        out_specs=pl.BlockSpec(memory_space=pl.ANY),
        scratch_shapes=[
            pltpu.VMEM((MOE_BLOCK, d), F32),
            pltpu.VMEM((MOE_BLOCK, d), F32),
            pltpu.SemaphoreType.DMA((1,)),
            pltpu.SemaphoreType.DMA((1,)),
        ],
    )
    return pl.pallas_call(
        _moe_kernel,
        grid_spec=grid_spec,
        out_shape=jax.ShapeDtypeStruct((TOP_K * t + MOE_BLOCK, d), F32),
        compiler_params=_params("arbitrary"),
        name="moe_experts",
    )(blk_e, tok_buf.reshape(n_blk, 1, MOE_BLOCK), dst_buf.reshape(n_blk, 1, MOE_BLOCK), h,
      w_gu.astype(BF16), b_gu.reshape(N_EXPERTS, 1, 2 * d_ff), w_dn.astype(BF16), b_dn.reshape(N_EXPERTS, 1, d))


def _moe_routing(top_idx, n_tok):
    n_asg = n_tok * TOP_K
    n_buf = n_asg + N_EXPERTS * MOE_BLOCK
    n_blk = n_buf // MOE_BLOCK
    flat_e = top_idx.reshape(-1).astype(jnp.int32)
    order = jnp.argsort(flat_e).astype(jnp.int32)
    se = flat_e[order]
    counts = jnp.zeros((N_EXPERTS,), jnp.int32).at[flat_e].add(1)
    padded = ((counts + MOE_BLOCK - 1) // MOE_BLOCK) * MOE_BLOCK
    start = jnp.cumsum(counts) - counts
    pend = jnp.cumsum(padded)
    pstart = pend - padded
    dest = pstart[se] + (jnp.arange(n_asg, dtype=jnp.int32) - start[se])
    slot = jnp.arange(n_buf, dtype=jnp.int32)
    tok_buf = jnp.zeros((n_buf,), jnp.int32).at[dest].set(order // TOP_K)
    dst_buf = (n_asg + slot % MOE_BLOCK).at[dest].set((order % TOP_K) * n_tok + order // TOP_K)
    blk_e = jnp.minimum(jnp.searchsorted(pend, jnp.arange(n_blk, dtype=jnp.int32) * MOE_BLOCK, side='right'),
                        N_EXPERTS - 1).astype(jnp.int32)
    return tok_buf, dst_buf, blk_e


def _combine_kernel(x_ref, y0_ref, y1_ref, y2_ref, y3_ref, tw_ref, mod_ref, lnf_ref, o_ref, *, final):
    tw = tw_ref[...]
    acc = tw[:, 0:1] * y0_ref[...]
    for kk, y_ref in enumerate((y1_ref, y2_ref, y3_ref), start=1):
        acc = acc + tw[:, kk:kk + 1] * y_ref[...]
    xn = x_ref[...] + mod_ref[0][5:6] * acc
    if final:
        xn = xn * lax.rsqrt(jnp.mean(xn * xn, axis=-1, keepdims=True) + EPS) * lnf_ref[...]
    o_ref[...] = xn


def _combine(x, y4, tw, mod, ln_final, seq, final):
    t, d = x.shape
    tm = 256
    per_b = seq // tm
    nt = t // tm
    yspec = lambda kk: pl.BlockSpec((tm, d), lambda i: (kk * nt + i, 0))
    return pl.pallas_call(
        functools.partial(_combine_kernel, final=final),
        grid=(nt,),
        in_specs=[
            pl.BlockSpec((tm, d), lambda i: (i, 0)),
            yspec(0), yspec(1), yspec(2), yspec(3),
            pl.BlockSpec((tm, LANES), lambda i: (i, 0)),
            pl.BlockSpec((1, 6, d), lambda i: (i // per_b, 0, 0)),
            pl.BlockSpec((1, d), lambda i: (0, 0)),
        ],
        out_specs=pl.BlockSpec((tm, d), lambda i: (i, 0)),
        out_shape=jax.ShapeDtypeStruct((t, d), F32),
        compiler_params=_params("arbitrary"),
        name="moe_combine",
    )(x, y4, y4, y4, y4, tw, mod, ln_final.reshape(1, d))


def _ffn(x, h, ti, tw, mod, w_gu, b_gu, w_dn, b_dn, ln_final, seq, final):
    t = x.shape[0]
    tok_buf, dst_buf, blk_e = _moe_routing(ti[:, :TOP_K], t)
    y4 = _moe_experts(h, tok_buf, dst_buf, blk_e, w_gu, b_gu, w_dn, b_dn)
    return _combine(x, y4, tw, mod, ln_final, seq, final)


def _even_inproj_weights(w_in):
    gdn_cols = 4 * GDN_HEADS * GDN_DK
    small0 = gdn_cols
    gla0 = small0 + 2 * GDN_HEADS
    gla_cols = 2 * GLA_HEADS * GLA_DK + 2 * GLA_HEADS * GLA_DV
    lg0 = gla0 + gla_cols
    w_big = jnp.concatenate([w_in[:, :gdn_cols], w_in[:, gla0:lg0]], axis=1).astype(BF16)
    w_small = jnp.zeros((w_in.shape[0], LANES), F32)
    w_small = w_small.at[:, :2 * GDN_HEADS].set(w_in[:, small0:gla0])
    w_small = w_small.at[:, 2 * GDN_HEADS:2 * GDN_HEADS + GLA_RANK].set(w_in[:, lg0:lg0 + GLA_RANK])
    return w_big, w_small.astype(BF16)


def _even_mixer(x, mod, ln_mix, w_in, conv_w, a_log, dt_bias, gdn_norm, w_a2, b_a2, gla_norm, bn, seq):
    w_big, w_small = _even_inproj_weights(w_in)
    p, small = _norm_proj(x, ln_mix, mod, w_big, w_small, seq)
    col, row = _gdn_gates(small, a_log, dt_bias)
    o_a = _gdn(p, col, row, conv_w, gdn_norm, bn, seq)
    o_b = _gla(p, small, w_a2, b_a2, gla_norm, bn, seq)
    return o_a, o_b


def kernel(x, c, ln_mix, ln_ffn, ln_final, w_ada, b_ada, e_w_in, e_conv_w, e_a_log, e_dt_bias, e_gdn_norm, e_gla_w_a2, e_gla_b_a2, e_gla_norm, e_w_out, o_w_in, o_conv_w, o_conv_b, o_w_ra, o_b_ra, o_w_ri, o_b_ri, o_lam, o_w_out, router_w, router_b, exp_w_gu, exp_b_gu, exp_w_dn, exp_b_dn):
    bn, seq, d = x.shape
    depth = w_ada.shape[0]
    mods = _ada_mod(c, w_ada, b_ada)
    xf = x.reshape(bn * seq, d)
    for layer in range(depth):
        mod = mods[layer]
        j = layer // 2
        if layer % 2 == 0:
            o_a, o_b = _even_mixer(xf, mod, ln_mix[layer], e_w_in[j], e_conv_w[j], e_a_log[j], e_dt_bias[j],
                                   e_gdn_norm[j], e_gla_w_a2[j], e_gla_b_a2[j], e_gla_norm[j], bn, seq)
            split = o_a.shape[1]
            acts = [o_a, o_b]
            weights = [e_w_out[j][:split].astype(BF16), e_w_out[j][split:].astype(BF16)]
        else:
            p = _norm_proj(xf, ln_mix[layer], mod, o_w_in[j].astype(BF16), None, seq)
            acts = [_lru(p, o_conv_w[j], o_conv_b[j], o_w_ra[j], o_b_ra[j], o_w_ri[j], o_b_ri[j], o_lam[j], bn, seq)]
            weights = [o_w_out[j].astype(BF16)]
        xf, h, ti, tw = _out_proj(acts, weights, xf, mod, ln_ffn[layer], router_w[layer], router_b[layer], seq)
        xf = _ffn(xf, h, ti, tw, mod, exp_w_gu[layer], exp_b_gu[layer], exp_w_dn[layer], exp_b_dn[layer],
                  ln_final, seq, final=(layer == depth - 1))
    return xf.reshape(bn, seq, d)
```

```python
import functools

import jax
import jax.numpy as jnp
from jax import lax
from jax.experimental import pallas as pl
from jax.experimental.pallas import tpu as pltpu

F32 = jnp.float32
BF16 = jnp.bfloat16
HIGHEST = lax.Precision.HIGHEST

EPS = 1e-6
CONV_W = 4
CHUNK = 64
GDN_HEADS = 8
GDN_DK = 128
GDN_DV = 128
GLA_HEADS = 8
GLA_DK = 64
GLA_DV = 128
GLA_RANK = 16
GLA_GATE_NORM = 16.0
LRU_BLOCKS = 8
LRU_C = 8.0
N_EXPERTS = 32
TOP_K = 4
SWIGLU_LIMIT = 7.0
SWIGLU_ALPHA = 1.702
MOE_BLOCK = 256

LANES = 128
SUBLANES = 8
VMEM_LIMIT = 56 * 1024 * 1024


def _dot(a, b, precision=None):
    return jnp.dot(a, b, preferred_element_type=F32, precision=precision)


def _dot_nt(a, b, precision=None):
    return lax.dot_general(a, b, (((1,), (1,)), ((), ())), preferred_element_type=F32, precision=precision)


def _dot_tn(a, b):
    return lax.dot_general(a, b, (((0,), (0,)), ((), ())), preferred_element_type=F32)


def _dot_split(a, b):
    a_hi = a.astype(BF16)
    b_hi = b.astype(BF16)
    a_lo = (a - a_hi.astype(F32)).astype(BF16)
    b_lo = (b - b_hi.astype(F32)).astype(BF16)
    return _dot(a_hi, b_hi) + (_dot(a_hi, b_lo) + _dot(a_lo, b_hi))


def _sigmoid(x):
    return 1.0 / (1.0 + jnp.exp(-x))


def _softplus(x):
    return jnp.maximum(x, 0.0) + jnp.log(1.0 + jnp.exp(-jnp.abs(x)))


def _silu(x):
    return x * _sigmoid(x)


def _params(*sem):
    return pltpu.CompilerParams(dimension_semantics=sem, vmem_limit_bytes=VMEM_LIMIT)


def _ada_kernel(c_ref, w_ref, b_ref, o_ref):
    c = c_ref[...]
    o_ref[0] = _dot(_silu(c), w_ref[0], precision=HIGHEST) + b_ref[0]


def _ada_mod(c, w_ada, b_ada):
    depth, d, n6 = w_ada.shape
    bn = c.shape[0]
    rows = -(-bn // SUBLANES) * SUBLANES
    cp = jnp.zeros((rows, d), F32).at[:bn].set(c)
    tn = 768
    out = pl.pallas_call(
        _ada_kernel,
        grid=(depth, n6 // tn),
        in_specs=[
            pl.BlockSpec((rows, d), lambda l, j: (0, 0)),
            pl.BlockSpec((1, d, tn), lambda l, j: (l, 0, j)),
            pl.BlockSpec((1, 1, tn), lambda l, j: (l, 0, j)),
        ],
        out_specs=pl.BlockSpec((1, rows, tn), lambda l, j: (l, 0, j)),
        out_shape=jax.ShapeDtypeStruct((depth, rows, n6), F32),
        compiler_params=_params("arbitrary", "arbitrary"),
        name="ada_mod",
    )(cp, w_ada, b_ada.reshape(depth, 1, n6))
    return out[:, :bn].reshape(depth, bn, 6, d)


def _norm_mod(x, ln, shift, scale):
    y = x * lax.rsqrt(jnp.mean(x * x, axis=-1, keepdims=True) + EPS) * ln
    return y * (1.0 + scale) + shift


def _norm_proj_kernel(x_ref, ln_ref, mod_ref, w_ref, *rest, has_small):
    if has_small:
        ws_ref, o_ref, os_ref, h_scr = rest
    else:
        o_ref, h_scr = rest
    j = pl.program_id(1)

    @pl.when(j == 0)
    def _():
        m = mod_ref[0]
        h = _norm_mod(x_ref[...], ln_ref[...], m[0:1], m[1:2]).astype(BF16)
        h_scr[...] = h
        if has_small:
            os_ref[...] = _dot(h, ws_ref[...])

    o_ref[...] = _dot(h_scr[...], w_ref[...]).astype(o_ref.dtype)


def _norm_proj(x, ln, mod, w, w_small, seq):
    t, d = x.shape
    n = w.shape[1]
    tm, tn = 512, 1024
    per_b = seq // tm
    has_small = w_small is not None
    in_specs = [
        pl.BlockSpec((tm, d), lambda i, j: (i, 0)),
        pl.BlockSpec((1, d), lambda i, j: (0, 0)),
        pl.BlockSpec((1, 6, d), lambda i, j: (i // per_b, 0, 0)),
        pl.BlockSpec((d, tn), lambda i, j: (0, j)),
    ]
    out_specs = [pl.BlockSpec((tm, tn), lambda i, j: (i, j))]
    out_shape = [jax.ShapeDtypeStruct((t, n), BF16)]
    args = [x, ln.reshape(1, d), mod, w]
    if has_small:
        in_specs.append(pl.BlockSpec((d, LANES), lambda i, j: (0, 0)))
        out_specs.append(pl.BlockSpec((tm, LANES), lambda i, j: (i, 0)))
        out_shape.append(jax.ShapeDtypeStruct((t, LANES), F32))
        args.append(w_small)
    outs = pl.pallas_call(
        functools.partial(_norm_proj_kernel, has_small=has_small),
        grid=(t // tm, n // tn),
        in_specs=in_specs,
        out_specs=out_specs,
        out_shape=out_shape,
        scratch_shapes=[pltpu.VMEM((tm, d), BF16)],
        compiler_params=_params("arbitrary", "arbitrary"),
        name="norm_proj",
    )(*args)
    return outs if has_small else outs[0]


def _gdn_gate_kernel(ga_ref, gb_ref, alog_ref, dtb_ref, gc_ref, beta_ref, eg_ref, ek_ref):
    g = -jnp.exp(alog_ref[...]) * _softplus(ga_ref[...] + dtb_ref[...])
    ci = lax.broadcasted_iota(jnp.int32, (LANES, LANES), 0)
    cj = lax.broadcasted_iota(jnp.int32, (LANES, LANES), 1)
    same = (ci // CHUNK) == (cj // CHUNK)
    upper = jnp.where(same & (ci <= cj), 1.0, 0.0).astype(F32)
    allc = jnp.where(same, 1.0, 0.0).astype(F32)
    gc = _dot(g, upper, precision=HIGHEST)
    g_last = _dot(g, allc, precision=HIGHEST)
    gc_ref[...] = gc
    beta_ref[...] = _sigmoid(gb_ref[...])
    eg_ref[...] = jnp.exp(gc)
    ek_ref[...] = jnp.exp(g_last - gc)


def _gdn_gates(small, a_log, dt_bias):
    t = small.shape[0]
    nb = t // LANES
    rows = GDN_HEADS * nb
    gb_t = small[:, 0:GDN_HEADS].T.reshape(rows, LANES)
    ga_t = small[:, GDN_HEADS:2 * GDN_HEADS].T.reshape(rows, LANES)
    alog_c = jnp.repeat(a_log.astype(F32), nb).reshape(rows, 1)
    dtb_c = jnp.repeat(dt_bias.astype(F32), nb).reshape(rows, 1)
    full = pl.BlockSpec((rows, LANES), lambda: (0, 0))
    col1 = pl.BlockSpec((rows, 1), lambda: (0, 0))
    gc, beta, eg, ek = pl.pallas_call(
        _gdn_gate_kernel,
        in_specs=[full, full, col1, col1],
        out_specs=[full] * 4,
        out_shape=[jax.ShapeDtypeStruct((rows, LANES), F32)] * 4,
        compiler_params=pltpu.CompilerParams(vmem_limit_bytes=VMEM_LIMIT),
        name="gdn_gates",
    )(ga_t, gb_t, alog_c, dtb_c)
    to_col = lambda a: a.reshape(GDN_HEADS, t).T
    col = jnp.concatenate([to_col(beta), to_col(gc), to_col(eg), to_col(ek)], axis=1)
    row = gc.reshape(GDN_HEADS, t // CHUNK, CHUNK).transpose(1, 0, 2)
    return col, row


def _gdn_kernel(q_ref, k_ref, v_ref, z_ref, col_ref, row_ref, cw_ref, nw_ref, o_ref,
                s_scr, tail_scr, qkv_scr, *, ts):
    width = GDN_HEADS * GDN_DK

    @pl.when(pl.program_id(1) == 0)
    def _():
        s_scr[...] = jnp.zeros_like(s_scr)
        tail_scr[...] = jnp.zeros_like(tail_scr)

    strip = 256
    for part, ref in enumerate((q_ref, k_ref, v_ref)):
        for s0 in range(0, width, strip):
            c0 = part * width + s0
            cur = ref[:, s0:s0 + strip].astype(F32)
            full = jnp.concatenate([tail_scr[:, c0:c0 + strip], cur], axis=0)
            w = cw_ref[:, c0:c0 + strip]
            acc = full * w[CONV_W - 1:CONV_W]
            for i in range(CONV_W - 1):
                acc = acc + pltpu.roll(full, CONV_W - 1 - i, axis=0) * w[i:i + 1]
            y = acc[SUBLANES:]
            qkv_scr[:, c0:c0 + strip] = _silu(y)
            tail_scr[:, c0:c0 + strip] = cur[ts - SUBLANES:]

    ri = lax.broadcasted_iota(jnp.int32, (CHUNK, CHUNK), 0)
    rj = lax.broadcasted_iota(jnp.int32, (CHUNK, CHUNK), 1)
    incl = ri >= rj
    strict = ri > rj
    eye = jnp.where(ri == rj, 1.0, 0.0).astype(F32)
    nw = nw_ref[...]

    def chunk_body(c, carry):
        r0 = pl.multiple_of(c * CHUNK, CHUNK)
        colc = col_ref[pl.ds(r0, CHUNK), :]
        rowc = row_ref[c]
        heads = range(GDN_HEADS)
        sl = [slice(h * GDN_DK, (h + 1) * GDN_DK) for h in heads]
        q = [qkv_scr[pl.ds(r0, CHUNK), sl[h]] for h in heads]
        k = [qkv_scr[pl.ds(r0, CHUNK), width + h * GDN_DK:width + (h + 1) * GDN_DK] for h in heads]
        v = [qkv_scr[pl.ds(r0, CHUNK), 2 * width + h * GDN_DK:2 * width + (h + 1) * GDN_DK] for h in heads]
        q = [a * lax.rsqrt(jnp.sum(a * a, axis=-1, keepdims=True) + EPS) * (GDN_DK ** -0.5) for a in q]
        k = [a * lax.rsqrt(jnp.sum(a * a, axis=-1, keepdims=True) + EPS) for a in k]
        beta = [colc[:, h:h + 1] for h in heads]
        gc = [colc[:, 8 + h:9 + h] for h in heads]
        eg = [colc[:, 16 + h:17 + h] for h in heads]
        ek = [colc[:, 24 + h:25 + h] for h in heads]
        decay = [jnp.exp(jnp.where(incl, gc[h] - rowc[h:h + 1, :], -jnp.inf)) for h in heads]
        kb = [a.astype(BF16) for a in k]
        qb = [a.astype(BF16) for a in q]
        kk = [_dot_nt(kb[h], kb[h]) for h in heads]
        qk = [_dot_nt(qb[h], kb[h]) for h in heads]
        npow = [jnp.where(strict, -(beta[h] * kk[h] * decay[h]), 0.0) for h in heads]
        ainv = [eye + npow[h] for h in heads]
        for _ in range(5):
            npow = [_dot_split(npow[h], npow[h]) for h in heads]
            ainv = [ainv[h] + _dot_split(ainv[h], npow[h]) for h in heads]
        ab = [a.astype(BF16) for a in ainv]
        u = [_dot(ab[h], (v[h] * beta[h]).astype(BF16)) for h in heads]
        w = [_dot(ab[h], (k[h] * (beta[h] * eg[h])).astype(BF16)) for h in heads]
        qk = [jnp.where(incl, qk[h] * decay[h], 0.0).astype(BF16) for h in heads]
        q_s = [(q[h] * eg[h]).astype(BF16) for h in heads]
        k_s = [(k[h] * ek[h]).astype(BF16) for h in heads]
        state = [s_scr[h] for h in heads]
        sb = [a.astype(BF16) for a in state]
        v_new = [(u[h] - _dot(w[h].astype(BF16), sb[h])).astype(BF16) for h in heads]
        o = [_dot(q_s[h], sb[h]) + _dot(qk[h], v_new[h]) for h in heads]
        for h in heads:
            s_scr[h] = state[h] * eg[h][CHUNK - 1:CHUNK, :] + _dot_tn(k_s[h], v_new[h])
        for h in heads:
            oh = o[h] * lax.rsqrt(jnp.mean(o[h] * o[h], axis=-1, keepdims=True) + EPS) * nw
            z = z_ref[pl.ds(r0, CHUNK), sl[h]].astype(F32)
            o_ref[pl.ds(r0, CHUNK), sl[h]] = (oh * _silu(z)).astype(o_ref.dtype)
        return carry

    lax.fori_loop(0, ts // CHUNK, chunk_body, 0)


def _gdn(p, col, row, conv_w, norm_w, bn, seq):
    t = p.shape[0]
    ts = 256
    per_b = seq // ts
    width = GDN_HEADS * GDN_DK
    tok = lambda b, s: b * per_b + s
    pspec = lambda cb: pl.BlockSpec((ts, width), lambda b, s: (tok(b, s), cb))
    return pl.pallas_call(
        functools.partial(_gdn_kernel, ts=ts),
        grid=(bn, per_b),
        in_specs=[
            pspec(0), pspec(1), pspec(2), pspec(3),
            pl.BlockSpec((ts, 4 * GDN_HEADS), lambda b, s: (tok(b, s), 0)),
            pl.BlockSpec((ts // CHUNK, GDN_HEADS, CHUNK), lambda b, s: (tok(b, s), 0, 0)),
            pl.BlockSpec((CONV_W, 3 * width), lambda b, s: (0, 0)),
            pl.BlockSpec((1, GDN_DV), lambda b, s: (0, 0)),
        ],
        out_specs=pl.BlockSpec((ts, width), lambda b, s: (tok(b, s), 0)),
        out_shape=jax.ShapeDtypeStruct((t, width), BF16),
        scratch_shapes=[
            pltpu.VMEM((GDN_HEADS, GDN_DK, GDN_DV), F32),
            pltpu.VMEM((SUBLANES, 3 * width), F32),
            pltpu.VMEM((ts, 3 * width), F32),
        ],
        compiler_params=_params("arbitrary", "arbitrary"),
        name="gdn",
    )(p, p, p, p, col, row, conv_w, norm_w.reshape(1, GDN_DV))


def _gla_kernel(q_ref, k_ref, v_ref, r_ref, sm_ref, wa_ref, ba_ref, nw_ref, o_ref, st_scr, g_scr, *, ts):
    pair_w = 2 * GLA_DK
    n_pairs = GLA_HEADS // 2

    @pl.when(pl.program_id(1) == 0)
    def _():
        st_scr[...] = jnp.zeros_like(st_scr)

    logits = _dot(sm_ref[...], wa_ref[...], precision=HIGHEST) + ba_ref[...]
    g_scr[...] = -_softplus(-logits) / GLA_GATE_NORM

    ri = lax.broadcasted_iota(jnp.int32, (CHUNK, CHUNK), 0)
    rj = lax.broadcasted_iota(jnp.int32, (CHUNK, CHUNK), 1)
    incl = ri >= rj
    lower = jnp.where(incl, 1.0, 0.0).astype(BF16)
    lane = lax.broadcasted_iota(jnp.int32, (CHUNK, pair_w), 1)
    sr = lax.broadcasted_iota(jnp.int32, (2 * GLA_DV, pair_w), 0)
    sc = lax.broadcasted_iota(jnp.int32, (2 * GLA_DV, pair_w), 1)
    live = (sr // GLA_DV) == (sc // GLA_DK)
    nw = nw_ref[...]

    def chunk_body(c, carry):
        r0 = pl.multiple_of(c * CHUNK, CHUNK)
        pairs = range(n_pairs)
        heads = range(GLA_HEADS)
        psl = [slice(p * pair_w, (p + 1) * pair_w) for p in pairs]
        hsl = [slice(h * GLA_DV, (h + 1) * GLA_DV) for h in heads]
        g = g_scr[pl.ds(r0, CHUNK), :]
        g1 = g.astype(BF16)
        rem = g - g1.astype(F32)
        g2 = rem.astype(BF16)
        g3 = (rem - g2.astype(F32)).astype(BF16)
        b_all = _dot(lower, g1) + (_dot(lower, g2) + _dot(lower, g3))
        b = [b_all[:, psl[p]] for p in pairs]
        b_mid = [a[CHUNK // 2:CHUNK // 2 + 1, :] for a in b]
        b_last = [a[CHUNK - 1:CHUNK, :] for a in b]
        q = [q_ref[pl.ds(r0, CHUNK), psl[p]].astype(F32) * (GLA_DK ** -0.5) for p in pairs]
        k = [k_ref[pl.ds(r0, CHUNK), psl[p]].astype(F32) for p in pairs]
        q_i = [q[p] * jnp.exp(b[p] - b_mid[p]) for p in pairs]
        k_i = [(k[p] * jnp.exp(b_mid[p] - b[p])).astype(BF16) for p in pairs]
        q_s = [(q[p] * jnp.exp(b[p])).astype(BF16) for p in pairs]
        k_s = [(k[p] * jnp.exp(b_last[p] - b[p])).astype(BF16) for p in pairs]
        vp = [v_ref[pl.ds(r0, CHUNK), 2 * p * GLA_DV:(2 * p + 2) * GLA_DV] for p in pairs]
        st = [st_scr[p] for p in pairs]
        o_inter = [_dot_nt(q_s[p], st[p].astype(BF16)) for p in pairs]
        upd = [_dot_tn(vp[p], k_s[p]) for p in pairs]
        for p in pairs:
            st_scr[p] = st[p] * jnp.exp(b_last[p]) + jnp.where(live, upd[p], 0.0)
        q_h = [jnp.where((lane // GLA_DK) == (h % 2), q_i[h // 2], 0.0).astype(BF16) for h in heads]
        att = [jnp.where(incl, _dot_nt(q_h[h], k_i[h // 2]), 0.0).astype(BF16) for h in heads]
        o = [_dot(att[h], vp[h // 2][:, (h % 2) * GLA_DV:(h % 2 + 1) * GLA_DV])
             + o_inter[h // 2][:, (h % 2) * GLA_DV:(h % 2 + 1) * GLA_DV] for h in heads]
        for h in heads:
            oh = o[h] * lax.rsqrt(jnp.mean(o[h] * o[h], axis=-1, keepdims=True) + EPS) * nw
            rg = r_ref[pl.ds(r0, CHUNK), hsl[h]].astype(F32)
            o_ref[pl.ds(r0, CHUNK), hsl[h]] = (oh * _silu(rg)).astype(o_ref.dtype)
        return carry

    lax.fori_loop(0, ts // CHUNK, chunk_body, 0)


def _gla(p, small, w_a2, b_a2, norm_w, bn, seq):
    t = p.shape[0]
    ts = 256
    per_b = seq // ts
    qk_w = GLA_HEADS * GLA_DK
    v_w = GLA_HEADS * GLA_DV
    gdn_cols = 4 * GDN_HEADS * GDN_DK
    tok = lambda b, s: b * per_b + s
    wa_pad = jnp.zeros((LANES, qk_w), F32).at[2 * GDN_HEADS:2 * GDN_HEADS + GLA_RANK].set(w_a2.astype(F32))
    return pl.pallas_call(
        functools.partial(_gla_kernel, ts=ts),
        grid=(bn, per_b),
        in_specs=[
            pl.BlockSpec((ts, qk_w), lambda b, s: (tok(b, s), gdn_cols // qk_w)),
            pl.BlockSpec((ts, qk_w), lambda b, s: (tok(b, s), gdn_cols // qk_w + 1)),
            pl.BlockSpec((ts, v_w), lambda b, s: (tok(b, s), (gdn_cols + 2 * qk_w) // v_w)),
            pl.BlockSpec((ts, v_w), lambda b, s: (tok(b, s), (gdn_cols + 2 * qk_w) // v_w + 1)),
            pl.BlockSpec((ts, LANES), lambda b, s: (tok(b, s), 0)),
            pl.BlockSpec((LANES, qk_w), lambda b, s: (0, 0)),
            pl.BlockSpec((1, qk_w), lambda b, s: (0, 0)),
            pl.BlockSpec((1, GLA_DV), lambda b, s: (0, 0)),
        ],
        out_specs=pl.BlockSpec((ts, v_w), lambda b, s: (tok(b, s), 0)),
        out_shape=jax.ShapeDtypeStruct((t, v_w), BF16),
        scratch_shapes=[
            pltpu.VMEM((GLA_HEADS // 2, 2 * GLA_DV, 2 * GLA_DK), F32),
            pltpu.VMEM((ts, qk_w), F32),
        ],
        compiler_params=_params("arbitrary", "arbitrary"),
        name="gla",
    )(p, p, p, p, small, wa_pad, b_a2.reshape(1, qk_w).astype(F32), norm_w.reshape(1, GLA_DV))


def _gelu_tanh(x):
    return 0.5 * x * (1.0 + jnp.tanh(0.7978845608028654 * (x + 0.044715 * x * x * x)))


def _lru_kernel(x_ref, g_ref, cw_ref, cb_ref, wra_ref, bra_ref, wri_ref, bri_ref, lam_ref, o_ref,
                tail_scr, h_scr, a_scr, u_scr, *, ts):
    width = x_ref.shape[1]
    blk = width // LRU_BLOCKS

    @pl.when(pl.program_id(1) == 0)
    def _():
        tail_scr[...] = jnp.zeros_like(tail_scr)
        h_scr[...] = jnp.zeros_like(h_scr)

    for gi in range(LRU_BLOCKS):
        lo, hi = gi * blk, (gi + 1) * blk
        cur = x_ref[:, lo:hi].astype(F32)
        full = jnp.concatenate([tail_scr[:, lo:hi], cur], axis=0)
        w = cw_ref[:, lo:hi]
        acc = full * w[CONV_W - 1:CONV_W]
        for i in range(CONV_W - 1):
            acc = acc + pltpu.roll(full, CONV_W - 1 - i, axis=0) * w[i:i + 1]
        xc = acc[SUBLANES:] + cb_ref[:, lo:hi]
        tail_scr[:, lo:hi] = cur[ts - SUBLANES:]
        xb = xc.astype(BF16)
        r = _sigmoid(_dot(xb, wra_ref[gi]) + bra_ref[:, lo:hi])
        ig = _sigmoid(_dot(xb, wri_ref[gi]) + bri_ref[:, lo:hi])
        log_a = -LRU_C * r * _softplus(-lam_ref[:, lo:hi])
        a = jnp.exp(log_a)
        a_scr[:, lo:hi] = a
        u_scr[:, lo:hi] = jnp.sqrt(1.0 - a * a) * (ig * xc)

    row = lax.broadcasted_iota(jnp.int32, (SUBLANES, width), 0)

    def scan_body(rb, h_prev):
        r0 = pl.multiple_of(rb * SUBLANES, SUBLANES)
        a = a_scr[pl.ds(r0, SUBLANES), :]
        u = u_scr[pl.ds(r0, SUBLANES), :]
        for d in (1, 2, 4):
            a_sh = jnp.where(row >= d, pltpu.roll(a, d, axis=0), 1.0)
            u_sh = jnp.where(row >= d, pltpu.roll(u, d, axis=0), 0.0)
            u = a * u_sh + u
            a = a * a_sh
        h = u + a * h_prev
        gate = g_ref[pl.ds(r0, SUBLANES), :].astype(F32)
        u_scr[pl.ds(r0, SUBLANES), :] = _gelu_tanh(gate) * h
        return h[SUBLANES - 1:SUBLANES, :]

    h_last = lax.fori_loop(0, ts // SUBLANES, scan_body, h_scr[...])
    h_scr[...] = h_last
    o_ref[...] = u_scr[...].astype(o_ref.dtype)


def _lru(p, conv_w, conv_b, w_ra, b_ra, w_ri, b_ri, lam, bn, seq):
    t = p.shape[0]
    width = p.shape[1] // 2
    blk = width // LRU_BLOCKS
    ts = 256
    per_b = seq // ts
    tok = lambda b, s: b * per_b + s
    vec = lambda a: a.reshape(1, width).astype(F32)
    vspec = pl.BlockSpec((1, width), lambda b, s: (0, 0))
    wspec = pl.BlockSpec((LRU_BLOCKS, blk, blk), lambda b, s: (0, 0, 0))
    return pl.pallas_call(
        functools.partial(_lru_kernel, ts=ts),
        grid=(bn, per_b),
        in_specs=[
            pl.BlockSpec((ts, width), lambda b, s: (tok(b, s), 0)),
            pl.BlockSpec((ts, width), lambda b, s: (tok(b, s), 1)),
            pl.BlockSpec((CONV_W, width), lambda b, s: (0, 0)),
            vspec, wspec, vspec, wspec, vspec, vspec,
        ],
        out_specs=pl.BlockSpec((ts, width), lambda b, s: (tok(b, s), 0)),
        out_shape=jax.ShapeDtypeStruct((t, width), BF16),
        scratch_shapes=[
            pltpu.VMEM((SUBLANES, width), F32),
            pltpu.VMEM((1, width), F32),
            pltpu.VMEM((ts, width), F32),
            pltpu.VMEM((ts, width), F32),
        ],
        compiler_params=_params("arbitrary", "arbitrary"),
        name="lru",
    )(p, p, conv_w, vec(conv_b), w_ra.astype(BF16), vec(b_ra), w_ri.astype(BF16), vec(b_ri), vec(lam))


def _pack_bf16_pairs(h):
    half = h.shape[1] // 2
    lo = lax.bitcast_convert_type(h[:, :half].astype(BF16).astype(F32), jnp.uint32)
    hi = lax.bitcast_convert_type(h[:, half:].astype(BF16).astype(F32), jnp.uint32)
    return (lo >> 16) | (hi & jnp.uint32(0xFFFF0000))


def _unpack_bf16_pairs(w):
    lo = lax.bitcast_convert_type(w << 16, F32).astype(BF16)
    hi = lax.bitcast_convert_type(w & jnp.uint32(0xFFFF0000), F32).astype(BF16)
    return lo, hi


def _out_proj_kernel(*refs, n_in):
    a_refs = refs[:n_in]
    w_refs = refs[n_in:2 * n_in]
    (x_ref, mod_ref, ln_ref, rwh_ref, rwl_ref, rb_ref,
     xo_ref, hp_ref, ti_ref, tw_ref, tr_ref, cnt_ref, cnt_scr) = refs[2 * n_in:]

    @pl.when(pl.program_id(0) == 0)
    def _():
        cnt_scr[...] = jnp.zeros_like(cnt_scr)

    y = _dot(a_refs[0][...], w_refs[0][...])
    for a_ref, w_ref in zip(a_refs[1:], w_refs[1:]):
        y = y + _dot(a_ref[...], w_ref[...])
    m = mod_ref[0]
    xn = x_ref[...] + m[2:3] * y
    xo_ref[...] = xn
    h = _norm_mod(xn, ln_ref[...], m[3:4], m[4:5])
    hp_ref[...] = _pack_bf16_pairs(h).reshape(hp_ref.shape)
    h_hi = h.astype(BF16)
    h_lo = (h - h_hi.astype(F32)).astype(BF16)
    logits = _dot(h_hi, rwh_ref[...]) + (_dot(h_hi, rwl_ref[...]) + _dot(h_lo, rwh_ref[...])) + rb_ref[...]
    lane = lax.broadcasted_iota(jnp.int32, logits.shape, 1)
    cur = jnp.where(lane < N_EXPERTS, logits, -jnp.inf)
    vals, idxs = [], []
    for _ in range(TOP_K):
        mx = jnp.max(cur, axis=-1, keepdims=True)
        ix = jnp.min(jnp.where(cur == mx, lane, LANES), axis=-1, keepdims=True)
        vals.append(mx)
        idxs.append(ix)
        cur = jnp.where(lane == ix, -jnp.inf, cur)
    es = [jnp.exp(v - vals[0]) for v in vals]
    den = es[0]
    for e in es[1:]:
        den = den + e
    tm = logits.shape[0]
    ri = lax.broadcasted_iota(jnp.int32, (tm, tm), 0)
    rj = lax.broadcasted_iota(jnp.int32, (tm, tm), 1)
    tri = jnp.where(rj < ri, 1.0, 0.0).astype(BF16)
    base = cnt_scr[...]
    ti = jnp.zeros(logits.shape, jnp.int32)
    tw = jnp.zeros(logits.shape, F32)
    tr = jnp.zeros(logits.shape, jnp.int32)
    for kk in range(TOP_K):
        sel = lane == idxs[kk]
        onehot = jnp.where(sel, 1.0, 0.0)
        before = _dot(tri, onehot.astype(BF16)) + base
        rank = jnp.sum(jnp.where(sel, before, 0.0), axis=-1, keepdims=True)
        base = base + jnp.sum(onehot, axis=0, keepdims=True)
        ti = jnp.where(lane == kk, idxs[kk], ti)
        tw = jnp.where(lane == kk, es[kk] / den, tw)
        tr = jnp.where(lane == kk, rank.astype(jnp.int32), tr)
    cnt_scr[...] = base
    ti_ref[...] = ti
    tw_ref[...] = tw
    tr_ref[...] = tr
    cnt_ref[...] = base.astype(jnp.int32)


def _out_proj(acts, weights, x, mod, ln_ffn, router_w, router_b, seq):
    t, d = x.shape
    tm = 256
    per_b = seq // tm
    n_in = len(acts)
    rw = jnp.zeros((d, LANES), F32).at[:, :N_EXPERTS].set(router_w.astype(F32))
    rw_hi = rw.astype(BF16)
    rw_lo = (rw - rw_hi.astype(F32)).astype(BF16)
    rb = jnp.zeros((1, LANES), F32).at[0, :N_EXPERTS].set(router_b.astype(F32))
    tiles = d // 2 // LANES
    in_specs = [pl.BlockSpec((tm, a.shape[1]), lambda i: (i, 0)) for a in acts]
    in_specs += [pl.BlockSpec(w.shape, lambda i: (0, 0)) for w in weights]
    in_specs += [
        pl.BlockSpec((tm, d), lambda i: (i, 0)),
        pl.BlockSpec((1, 6, d), lambda i: (i // per_b, 0, 0)),
        pl.BlockSpec((1, d), lambda i: (0, 0)),
        pl.BlockSpec((d, LANES), lambda i: (0, 0)),
        pl.BlockSpec((d, LANES), lambda i: (0, 0)),
        pl.BlockSpec((1, LANES), lambda i: (0, 0)),
    ]
    row_d = pl.BlockSpec((tm, d), lambda i: (i, 0))
    row_h = pl.BlockSpec((tm, tiles, LANES), lambda i: (i, 0, 0))
    row_l = pl.BlockSpec((tm, LANES), lambda i: (i, 0))
    return pl.pallas_call(
        functools.partial(_out_proj_kernel, n_in=n_in),
        grid=(t // tm,),
        in_specs=in_specs,
        out_specs=[row_d, row_h, row_l, row_l, row_l, pl.BlockSpec((1, LANES), lambda i: (0, 0))],
        out_shape=[
            jax.ShapeDtypeStruct((t, d), F32),
            jax.ShapeDtypeStruct((t, tiles, LANES), jnp.uint32),
            jax.ShapeDtypeStruct((t, LANES), jnp.int32),
            jax.ShapeDtypeStruct((t, LANES), F32),
            jax.ShapeDtypeStruct((t, LANES), jnp.int32),
            jax.ShapeDtypeStruct((1, LANES), jnp.int32),
        ],
        scratch_shapes=[pltpu.VMEM((1, LANES), F32)],
        compiler_params=_params("arbitrary"),
        name="out_proj",
    )(*acts, *weights, x, mod, ln_ffn.reshape(1, d), rw_hi, rw_lo, rb)


def _moe_routing(ti, tr, cnt, n_tok):
    n_blk = (n_tok * TOP_K) // MOE_BLOCK + N_EXPERTS
    counts = cnt[0, :N_EXPERTS]
    padded = ((counts + MOE_BLOCK - 1) // MOE_BLOCK) * MOE_BLOCK
    pend = jnp.cumsum(padded)
    pstart = pend - padded
    e = ti[:, :TOP_K]
    hit = e[:, :, None] == jnp.arange(N_EXPERTS, dtype=jnp.int32)
    dest = tr[:, :TOP_K] + jnp.sum(jnp.where(hit, pstart, 0), axis=-1)
    blk_e = jnp.minimum(jnp.searchsorted(pend, jnp.arange(n_blk, dtype=jnp.int32) * MOE_BLOCK, side='right'),
                        N_EXPERTS - 1).astype(jnp.int32)
    return dest.astype(jnp.int32), blk_e


DISPATCH_TOKENS = 512
COMBINE_TOKENS = 128
ISSUE_TOKENS = 4


def _dispatch_kernel(dest_ref, hp_hbm, xs_in_hbm, xs_hbm, sem):
    del xs_in_hbm
    base = pl.program_id(0) * DISPATCH_TOKENS

    def row_copy(tok, dst):
        return pltpu.make_async_copy(hp_hbm.at[tok], xs_hbm.at[dst], sem.at[0])

    def issue(g, c):
        t0 = g * ISSUE_TOKENS
        dsts = [dest_ref[0, 0, t0 * TOP_K + j] for j in range(ISSUE_TOKENS * TOP_K)]
        for j, dst in enumerate(dsts):
            row_copy(base + t0 + j // TOP_K, dst).start()
        return c

    lax.fori_loop(0, DISPATCH_TOKENS // ISSUE_TOKENS, issue, 0)
    n_rows = DISPATCH_TOKENS * TOP_K
    pltpu.make_async_copy(hp_hbm.at[pl.ds(0, n_rows)], xs_hbm.at[pl.ds(0, n_rows)], sem.at[0]).wait()


def _dispatch(hp, dest, n_buf):
    t = hp.shape[0]
    n_steps = t // DISPATCH_TOKENS
    ids = dest.reshape(n_steps, 1, DISPATCH_TOKENS * TOP_K)
    return pl.pallas_call(
        _dispatch_kernel,
        grid=(n_steps,),
        in_specs=[
            pl.BlockSpec((1, 1, DISPATCH_TOKENS * TOP_K), lambda i: (i, 0, 0), memory_space=pltpu.SMEM),
            pl.BlockSpec(memory_space=pl.ANY),
            pl.BlockSpec(memory_space=pl.ANY),
        ],
        out_specs=pl.BlockSpec(memory_space=pl.ANY),
        out_shape=jax.ShapeDtypeStruct((n_buf,) + hp.shape[1:], jnp.uint32),
        scratch_shapes=[pltpu.SemaphoreType.DMA((1,))],
        input_output_aliases={2: 0},
        compiler_params=_params("arbitrary"),
        name="moe_dispatch",
    )(ids, hp, jnp.zeros((n_buf,) + hp.shape[1:], jnp.uint32))


def _moe_kernel(blk_e_ref, xs_ref, wgu_ref, bgu_ref, wdn_ref, bdn_ref, y_ref):
    del blk_e_ref
    half = xs_ref.shape[1] * xs_ref.shape[2]
    d_ff = wdn_ref.shape[1]
    lo, hi = _unpack_bf16_pairs(xs_ref[...].reshape(MOE_BLOCK, half))
    gu = _dot(lo, wgu_ref[0, :half, :]) + _dot(hi, wgu_ref[0, half:, :]) + bgu_ref[0]
    gate = jnp.minimum(gu[:, :d_ff], SWIGLU_LIMIT)
    up = jnp.clip(gu[:, d_ff:], -SWIGLU_LIMIT, SWIGLU_LIMIT)
    act = (up + 1.0) * gate * _sigmoid(SWIGLU_ALPHA * gate)
    y = _dot(act.astype(BF16), wdn_ref[0]) + bdn_ref[0]
    y_ref[...] = y.reshape(y_ref.shape)


def _moe_experts(xs, blk_e, w_gu, b_gu, w_dn, b_dn):
    n_buf, tiles, _ = xs.shape
    d = 2 * tiles * LANES
    d_ff = w_dn.shape[1]
    grid_spec = pltpu.PrefetchScalarGridSpec(
        num_scalar_prefetch=1,
        grid=(n_buf // MOE_BLOCK,),
        in_specs=[
            pl.BlockSpec((MOE_BLOCK, tiles, LANES), lambda b, e: (b, 0, 0)),
            pl.BlockSpec((1, d, 2 * d_ff), lambda b, e: (e[b], 0, 0)),
            pl.BlockSpec((1, 1, 2 * d_ff), lambda b, e: (e[b], 0, 0)),
            pl.BlockSpec((1, d_ff, d), lambda b, e: (e[b], 0, 0)),
            pl.BlockSpec((1, 1, d), lambda b, e: (e[b], 0, 0)),
        ],
        out_specs=pl.BlockSpec((MOE_BLOCK, d // LANES, LANES), lambda b, e: (b, 0, 0)),
    )
    return pl.pallas_call(
        _moe_kernel,
        grid_spec=grid_spec,
        out_shape=jax.ShapeDtypeStruct((n_buf, d // LANES, LANES), F32),
        compiler_params=_params("arbitrary"),
        name="moe_experts",
    )(blk_e, xs, w_gu.astype(BF16), b_gu.reshape(N_EXPERTS, 1, 2 * d_ff), w_dn.astype(BF16),
      b_dn.reshape(N_EXPERTS, 1, d))


def _combine_kernel(dest_ref, dest_next_ref, x_ref, tw_ref, mod_ref, lnf_ref, y_hbm, o_ref, ybuf, sem, *, final):
    i = pl.program_id(0)
    n = pl.num_programs(0)
    slot = i % 2
    tm = COMBINE_TOKENS

    def issue(ids_ref, s):
        def body(g, c):
            t0 = g * ISSUE_TOKENS
            srcs = [ids_ref[0, 0, t0 * TOP_K + j] for j in range(ISSUE_TOKENS * TOP_K)]
            for j, src in enumerate(srcs):
                pltpu.make_async_copy(y_hbm.at[src], ybuf.at[s, (j % TOP_K) * tm + t0 + j // TOP_K],
                                      sem.at[s]).start()
            return c
        lax.fori_loop(0, tm // ISSUE_TOKENS, body, 0)

    @pl.when(i == 0)
    def _():
        issue(dest_ref, 0)

    @pl.when(i + 1 < n)
    def _():
        issue(dest_next_ref, 1 - slot)

    pltpu.make_async_copy(y_hbm.at[pl.ds(0, TOP_K * tm)], ybuf.at[slot], sem.at[slot]).wait()
    tw = tw_ref[...]
    acc = None
    for kk in range(TOP_K):
        gate = jnp.broadcast_to(tw[:, kk:kk + 1], (tm, LANES)).reshape(tm, 1, LANES)
        term = gate * ybuf[slot, kk * tm:(kk + 1) * tm]
        acc = term if acc is None else acc + term
    xn = x_ref[...] + mod_ref[0][5:6] * acc.reshape(x_ref.shape)
    if final:
        xn = xn * lax.rsqrt(jnp.mean(xn * xn, axis=-1, keepdims=True) + EPS) * lnf_ref[...]
    o_ref[...] = xn


def _combine(x, y, dest, tw, mod, ln_final, seq, final):
    t, d = x.shape
    tm = COMBINE_TOKENS
    per_b = seq // tm
    nt = t // tm
    ids = dest.reshape(nt, 1, tm * TOP_K)
    id_spec = lambda f: pl.BlockSpec((1, 1, tm * TOP_K), f, memory_space=pltpu.SMEM)
    return pl.pallas_call(
        functools.partial(_combine_kernel, final=final),
        grid=(nt,),
        in_specs=[
            id_spec(lambda i: (i, 0, 0)),
            id_spec(lambda i: (jnp.minimum(i + 1, nt - 1), 0, 0)),
            pl.BlockSpec((tm, d), lambda i: (i, 0)),
            pl.BlockSpec((tm, LANES), lambda i: (i, 0)),
            pl.BlockSpec((1, 6, d), lambda i: (i // per_b, 0, 0)),
            pl.BlockSpec((1, d), lambda i: (0, 0)),
            pl.BlockSpec(memory_space=pl.ANY),
        ],
        out_specs=pl.BlockSpec((tm, d), lambda i: (i, 0)),
        out_shape=jax.ShapeDtypeStruct((t, d), F32),
        scratch_shapes=[pltpu.VMEM((2, TOP_K * tm, d // LANES, LANES), F32), pltpu.SemaphoreType.DMA((2,))],
        compiler_params=_params("arbitrary"),
        name="moe_combine",
    )(ids, ids, x, tw, mod, ln_final.reshape(1, d), y)


def _ffn(x, hp, ti, tw, tr, cnt, mod, w_gu, b_gu, w_dn, b_dn, ln_final, seq, final):
    t = x.shape[0]
    dest, blk_e = _moe_routing(ti, tr, cnt, t)
    n_buf = blk_e.shape[0] * MOE_BLOCK
    xs = _dispatch(hp, dest, n_buf)
    y = _moe_experts(xs, blk_e, w_gu, b_gu, w_dn, b_dn)
    return _combine(x, y, dest, tw, mod, ln_final, seq, final)


def _even_inproj_weights(w_in):
    gdn_cols = 4 * GDN_HEADS * GDN_DK
    small0 = gdn_cols
    gla0 = small0 + 2 * GDN_HEADS
    gla_cols = 2 * GLA_HEADS * GLA_DK + 2 * GLA_HEADS * GLA_DV
    lg0 = gla0 + gla_cols
    w_big = jnp.concatenate([w_in[:, :gdn_cols], w_in[:, gla0:lg0]], axis=1).astype(BF16)
    w_small = jnp.zeros((w_in.shape[0], LANES), F32)
    w_small = w_small.at[:, :2 * GDN_HEADS].set(w_in[:, small0:gla0])
    w_small = w_small.at[:, 2 * GDN_HEADS:2 * GDN_HEADS + GLA_RANK].set(w_in[:, lg0:lg0 + GLA_RANK])
    return w_big, w_small.astype(BF16)


def _even_mixer(x, mod, ln_mix, w_in, conv_w, a_log, dt_bias, gdn_norm, w_a2, b_a2, gla_norm, bn, seq):
    w_big, w_small = _even_inproj_weights(w_in)
    p, small = _norm_proj(x, ln_mix, mod, w_big, w_small, seq)
    col, row = _gdn_gates(small, a_log, dt_bias)
    o_a = _gdn(p, col, row, conv_w, gdn_norm, bn, seq)
    o_b = _gla(p, small, w_a2, b_a2, gla_norm, bn, seq)
    return o_a, o_b


def kernel(x, c, ln_mix, ln_ffn, ln_final, w_ada, b_ada, e_w_in, e_conv_w, e_a_log, e_dt_bias, e_gdn_norm, e_gla_w_a2, e_gla_b_a2, e_gla_norm, e_w_out, o_w_in, o_conv_w, o_conv_b, o_w_ra, o_b_ra, o_w_ri, o_b_ri, o_lam, o_w_out, router_w, router_b, exp_w_gu, exp_b_gu, exp_w_dn, exp_b_dn):
    bn, seq, d = x.shape
    depth = w_ada.shape[0]
    mods = _ada_mod(c, w_ada, b_ada)
    xf = x.reshape(bn * seq, d)
    for layer in range(depth):
        mod = mods[layer]
        j = layer // 2
        if layer % 2 == 0:
            o_a, o_b = _even_mixer(xf, mod, ln_mix[layer], e_w_in[j], e_conv_w[j], e_a_log[j], e_dt_bias[j],
                                   e_gdn_norm[j], e_gla_w_a2[j], e_gla_b_a2[j], e_gla_norm[j], bn, seq)
            split = o_a.shape[1]
            acts = [o_a, o_b]
            weights = [e_w_out[j][:split].astype(BF16), e_w_out[j][split:].astype(BF16)]
        else:
            p = _norm_proj(xf, ln_mix[layer], mod, o_w_in[j].astype(BF16), None, seq)
            acts = [_lru(p, o_conv_w[j], o_conv_b[j], o_w_ra[j], o_b_ra[j], o_w_ri[j], o_b_ri[j], o_lam[j], bn, seq)]
            weights = [o_w_out[j].astype(BF16)]
        xf, hp, ti, tw, tr, cnt = _out_proj(acts, weights, xf, mod, ln_ffn[layer], router_w[layer], router_b[layer], seq)
        xf = _ffn(xf, hp, ti, tw, tr, cnt, mod, exp_w_gu[layer], exp_b_gu[layer], exp_w_dn[layer], exp_b_dn[layer],
                  ln_final, seq, final=(layer == depth - 1))
    return xf.reshape(bn, seq, d)
```

```python
import functools

import jax
import jax.numpy as jnp
from jax import lax
from jax.experimental import pallas as pl
from jax.experimental.pallas import tpu as pltpu

F32 = jnp.float32
BF16 = jnp.bfloat16
HIGHEST = lax.Precision.HIGHEST

EPS = 1e-6
CONV_W = 4
CHUNK = 64
GDN_HEADS = 8
GDN_DK = 128
GDN_DV = 128
GLA_HEADS = 8
GLA_DK = 64
GLA_DV = 128
GLA_RANK = 16
GLA_GATE_NORM = 16.0
LRU_BLOCKS = 8
LRU_C = 8.0
N_EXPERTS = 32
TOP_K = 4
SWIGLU_LIMIT = 7.0
SWIGLU_ALPHA = 1.702
MOE_BLOCK = 256

LANES = 128
SUBLANES = 8
VMEM_LIMIT = 56 * 1024 * 1024


def _dot(a, b, precision=None):
    return jnp.dot(a, b, preferred_element_type=F32, precision=precision)


def _dot_nt(a, b, precision=None):
    return lax.dot_general(a, b, (((1,), (1,)), ((), ())), preferred_element_type=F32, precision=precision)


def _dot_tn(a, b):
    return lax.dot_general(a, b, (((0,), (0,)), ((), ())), preferred_element_type=F32)


def _dot_split(a, b):
    a_hi = a.astype(BF16)
    b_hi = b.astype(BF16)
    a_lo = (a - a_hi.astype(F32)).astype(BF16)
    b_lo = (b - b_hi.astype(F32)).astype(BF16)
    return _dot(a_hi, b_hi) + (_dot(a_hi, b_lo) + _dot(a_lo, b_hi))


def _sigmoid(x):
    return 1.0 / (1.0 + jnp.exp(-x))


def _softplus(x):
    return jnp.maximum(x, 0.0) + jnp.log(1.0 + jnp.exp(-jnp.abs(x)))


def _silu(x):
    return x * _sigmoid(x)


def _params(*sem):
    return pltpu.CompilerParams(dimension_semantics=sem, vmem_limit_bytes=VMEM_LIMIT)


def _ada_kernel(c_ref, w_ref, b_ref, o_ref):
    c = c_ref[...]
    o_ref[0] = _dot(_silu(c), w_ref[0], precision=HIGHEST) + b_ref[0]


def _ada_mod(c, w_ada, b_ada):
    depth, d, n6 = w_ada.shape
    bn = c.shape[0]
    rows = -(-bn // SUBLANES) * SUBLANES
    cp = jnp.zeros((rows, d), F32).at[:bn].set(c)
    tn = 768
    out = pl.pallas_call(
        _ada_kernel,
        grid=(depth, n6 // tn),
        in_specs=[
            pl.BlockSpec((rows, d), lambda l, j: (0, 0)),
            pl.BlockSpec((1, d, tn), lambda l, j: (l, 0, j)),
            pl.BlockSpec((1, 1, tn), lambda l, j: (l, 0, j)),
        ],
        out_specs=pl.BlockSpec((1, rows, tn), lambda l, j: (l, 0, j)),
        out_shape=jax.ShapeDtypeStruct((depth, rows, n6), F32),
        compiler_params=_params("arbitrary", "arbitrary"),
        name="ada_mod",
    )(cp, w_ada, b_ada.reshape(depth, 1, n6))
    return out[:, :bn].reshape(depth, bn, 6, d)


def _norm_mod(x, ln, shift, scale):
    y = x * lax.rsqrt(jnp.mean(x * x, axis=-1, keepdims=True) + EPS) * ln
    return y * (1.0 + scale) + shift


def _norm_proj_kernel(x_ref, ln_ref, mod_ref, w_ref, *rest, has_small):
    if has_small:
        ws_ref, o_ref, os_ref, h_scr = rest
    else:
        o_ref, h_scr = rest
    j = pl.program_id(1)

    @pl.when(j == 0)
    def _():
        m = mod_ref[0]
        h = _norm_mod(x_ref[...], ln_ref[...], m[0:1], m[1:2]).astype(BF16)
        h_scr[...] = h
        if has_small:
            os_ref[...] = _dot(h, ws_ref[...])

    o_ref[...] = _dot(h_scr[...], w_ref[...]).astype(o_ref.dtype)


def _norm_proj(x, ln, mod, w, w_small, seq):
    t, d = x.shape
    n = w.shape[1]
    tm, tn = 512, 1024
    per_b = seq // tm
    has_small = w_small is not None
    in_specs = [
        pl.BlockSpec((tm, d), lambda i, j: (i, 0)),
        pl.BlockSpec((1, d), lambda i, j: (0, 0)),
        pl.BlockSpec((1, 6, d), lambda i, j: (i // per_b, 0, 0)),
        pl.BlockSpec((d, tn), lambda i, j: (0, j)),
    ]
    out_specs = [pl.BlockSpec((tm, tn), lambda i, j: (i, j))]
    out_shape = [jax.ShapeDtypeStruct((t, n), BF16)]
    args = [x, ln.reshape(1, d), mod, w]
    if has_small:
        in_specs.append(pl.BlockSpec((d, LANES), lambda i, j: (0, 0)))
        out_specs.append(pl.BlockSpec((tm, LANES), lambda i, j: (i, 0)))
        out_shape.append(jax.ShapeDtypeStruct((t, LANES), F32))
        args.append(w_small)
    outs = pl.pallas_call(
        functools.partial(_norm_proj_kernel, has_small=has_small),
        grid=(t // tm, n // tn),
        in_specs=in_specs,
        out_specs=out_specs,
        out_shape=out_shape,
        scratch_shapes=[pltpu.VMEM((tm, d), BF16)],
        compiler_params=_params("arbitrary", "arbitrary"),
        name="norm_proj",
    )(*args)
    return outs if has_small else outs[0]


def _gdn_gate_kernel(ga_ref, gb_ref, alog_ref, dtb_ref, gc_ref, beta_ref, eg_ref, ek_ref):
    g = -jnp.exp(alog_ref[...]) * _softplus(ga_ref[...] + dtb_ref[...])
    ci = lax.broadcasted_iota(jnp.int32, (LANES, LANES), 0)
    cj = lax.broadcasted_iota(jnp.int32, (LANES, LANES), 1)
    same = (ci // CHUNK) == (cj // CHUNK)
    upper = jnp.where(same & (ci <= cj), 1.0, 0.0).astype(F32)
    allc = jnp.where(same, 1.0, 0.0).astype(F32)
    gc = _dot(g, upper, precision=HIGHEST)
    g_last = _dot(g, allc, precision=HIGHEST)
    gc_ref[...] = gc
    beta_ref[...] = _sigmoid(gb_ref[...])
    eg_ref[...] = jnp.exp(gc)
    ek_ref[...] = jnp.exp(g_last - gc)


def _gdn_gates(small, a_log, dt_bias):
    t = small.shape[0]
    nb = t // LANES
    rows = GDN_HEADS * nb
    gb_t = small[:, 0:GDN_HEADS].T.reshape(rows, LANES)
    ga_t = small[:, GDN_HEADS:2 * GDN_HEADS].T.reshape(rows, LANES)
    alog_c = jnp.repeat(a_log.astype(F32), nb).reshape(rows, 1)
    dtb_c = jnp.repeat(dt_bias.astype(F32), nb).reshape(rows, 1)
    full = pl.BlockSpec((rows, LANES), lambda: (0, 0))
    col1 = pl.BlockSpec((rows, 1), lambda: (0, 0))
    gc, beta, eg, ek = pl.pallas_call(
        _gdn_gate_kernel,
        in_specs=[full, full, col1, col1],
        out_specs=[full] * 4,
        out_shape=[jax.ShapeDtypeStruct((rows, LANES), F32)] * 4,
        compiler_params=pltpu.CompilerParams(vmem_limit_bytes=VMEM_LIMIT),
        name="gdn_gates",
    )(ga_t, gb_t, alog_c, dtb_c)
    to_col = lambda a: a.reshape(GDN_HEADS, t).T
    col = jnp.concatenate([to_col(beta), to_col(gc), to_col(eg), to_col(ek)], axis=1)
    row = gc.reshape(GDN_HEADS, t // CHUNK, CHUNK).transpose(1, 0, 2)
    return col, row


def _gdn_kernel(q_ref, k_ref, v_ref, z_ref, col_ref, row_ref, cw_ref, nw_ref, o_ref,
                s_scr, tail_scr, qkv_scr, *, ts):
    width = GDN_HEADS * GDN_DK

    @pl.when(pl.program_id(1) == 0)
    def _():
        s_scr[...] = jnp.zeros_like(s_scr)
        tail_scr[...] = jnp.zeros_like(tail_scr)

    strip = 256
    for part, ref in enumerate((q_ref, k_ref, v_ref)):
        for s0 in range(0, width, strip):
            c0 = part * width + s0
            cur = ref[:, s0:s0 + strip].astype(F32)
            full = jnp.concatenate([tail_scr[:, c0:c0 + strip], cur], axis=0)
            w = cw_ref[:, c0:c0 + strip]
            acc = full * w[CONV_W - 1:CONV_W]
            for i in range(CONV_W - 1):
                acc = acc + pltpu.roll(full, CONV_W - 1 - i, axis=0) * w[i:i + 1]
            y = acc[SUBLANES:]
            qkv_scr[:, c0:c0 + strip] = _silu(y)
            tail_scr[:, c0:c0 + strip] = cur[ts - SUBLANES:]

    ri = lax.broadcasted_iota(jnp.int32, (CHUNK, CHUNK), 0)
    rj = lax.broadcasted_iota(jnp.int32, (CHUNK, CHUNK), 1)
    incl = ri >= rj
    strict = ri > rj
    eye = jnp.where(ri == rj, 1.0, 0.0).astype(F32)
    nw = nw_ref[...]

    def chunk_body(c, carry):
        r0 = pl.multiple_of(c * CHUNK, CHUNK)
        colc = col_ref[pl.ds(r0, CHUNK), :]
        rowc = row_ref[c]
        heads = range(GDN_HEADS)
        sl = [slice(h * GDN_DK, (h + 1) * GDN_DK) for h in heads]
        q = [qkv_scr[pl.ds(r0, CHUNK), sl[h]] for h in heads]
        k = [qkv_scr[pl.ds(r0, CHUNK), width + h * GDN_DK:width + (h + 1) * GDN_DK] for h in heads]
        v = [qkv_scr[pl.ds(r0, CHUNK), 2 * width + h * GDN_DK:2 * width + (h + 1) * GDN_DK] for h in heads]
        q = [a * lax.rsqrt(jnp.sum(a * a, axis=-1, keepdims=True) + EPS) * (GDN_DK ** -0.5) for a in q]
        k = [a * lax.rsqrt(jnp.sum(a * a, axis=-1, keepdims=True) + EPS) for a in k]
        beta = [colc[:, h:h + 1] for h in heads]
        gc = [colc[:, 8 + h:9 + h] for h in heads]
        eg = [colc[:, 16 + h:17 + h] for h in heads]
        ek = [colc[:, 24 + h:25 + h] for h in heads]
        decay = [jnp.exp(jnp.where(incl, gc[h] - rowc[h:h + 1, :], -jnp.inf)) for h in heads]
        kb = [a.astype(BF16) for a in k]
        qb = [a.astype(BF16) for a in q]
        kk = [_dot_nt(kb[h], kb[h]) for h in heads]
        qk = [_dot_nt(qb[h], kb[h]) for h in heads]
        npow = [jnp.where(strict, -(beta[h] * kk[h] * decay[h]), 0.0) for h in heads]
        ainv = [eye + npow[h] for h in heads]
        for _ in range(5):
            npow = [_dot_split(npow[h], npow[h]) for h in heads]
            ainv = [ainv[h] + _dot_split(ainv[h], npow[h]) for h in heads]
        ab = [a.astype(BF16) for a in ainv]
        u = [_dot(ab[h], (v[h] * beta[h]).astype(BF16)) for h in heads]
        w = [_dot(ab[h], (k[h] * (beta[h] * eg[h])).astype(BF16)) for h in heads]
        qk = [jnp.where(incl, qk[h] * decay[h], 0.0).astype(BF16) for h in heads]
        q_s = [(q[h] * eg[h]).astype(BF16) for h in heads]
        k_s = [(k[h] * ek[h]).astype(BF16) for h in heads]
        state = [s_scr[h] for h in heads]
        sb = [a.astype(BF16) for a in state]
        v_new = [(u[h] - _dot(w[h].astype(BF16), sb[h])).astype(BF16) for h in heads]
        o = [_dot(q_s[h], sb[h]) + _dot(qk[h], v_new[h]) for h in heads]
        for h in heads:
            s_scr[h] = state[h] * eg[h][CHUNK - 1:CHUNK, :] + _dot_tn(k_s[h], v_new[h])
        for h in heads:
            oh = o[h] * lax.rsqrt(jnp.mean(o[h] * o[h], axis=-1, keepdims=True) + EPS) * nw
            z = z_ref[pl.ds(r0, CHUNK), sl[h]].astype(F32)
            o_ref[pl.ds(r0, CHUNK), sl[h]] = (oh * _silu(z)).astype(o_ref.dtype)
        return carry

    lax.fori_loop(0, ts // CHUNK, chunk_body, 0)


def _gdn(p, col, row, conv_w, norm_w, bn, seq):
    t = p.shape[0]
    ts = 256
    per_b = seq // ts
    width = GDN_HEADS * GDN_DK
    tok = lambda b, s: b * per_b + s
    pspec = lambda cb: pl.BlockSpec((ts, width), lambda b, s: (tok(b, s), cb))
    return pl.pallas_call(
        functools.partial(_gdn_kernel, ts=ts),
        grid=(bn, per_b),
        in_specs=[
            pspec(0), pspec(1), pspec(2), pspec(3),
            pl.BlockSpec((ts, 4 * GDN_HEADS), lambda b, s: (tok(b, s), 0)),
            pl.BlockSpec((ts // CHUNK, GDN_HEADS, CHUNK), lambda b, s: (tok(b, s), 0, 0)),
            pl.BlockSpec((CONV_W, 3 * width), lambda b, s: (0, 0)),
            pl.BlockSpec((1, GDN_DV), lambda b, s: (0, 0)),
        ],
        out_specs=pl.BlockSpec((ts, width), lambda b, s: (tok(b, s), 0)),
        out_shape=jax.ShapeDtypeStruct((t, width), BF16),
        scratch_shapes=[
            pltpu.VMEM((GDN_HEADS, GDN_DK, GDN_DV), F32),
            pltpu.VMEM((SUBLANES, 3 * width), F32),
            pltpu.VMEM((ts, 3 * width), F32),
        ],
        compiler_params=_params("arbitrary", "arbitrary"),
        name="gdn",
    )(p, p, p, p, col, row, conv_w, norm_w.reshape(1, GDN_DV))


def _gla_kernel(q_ref, k_ref, v_ref, r_ref, sm_ref, wa_ref, ba_ref, nw_ref, o_ref, st_scr, g_scr, *, ts):
    pair_w = 2 * GLA_DK
    n_pairs = GLA_HEADS // 2

    @pl.when(pl.program_id(1) == 0)
    def _():
        st_scr[...] = jnp.zeros_like(st_scr)

    logits = _dot(sm_ref[...], wa_ref[...], precision=HIGHEST) + ba_ref[...]
    g_scr[...] = -_softplus(-logits) / GLA_GATE_NORM

    ri = lax.broadcasted_iota(jnp.int32, (CHUNK, CHUNK), 0)
    rj = lax.broadcasted_iota(jnp.int32, (CHUNK, CHUNK), 1)
    incl = ri >= rj
    lower = jnp.where(incl, 1.0, 0.0).astype(BF16)
    lane = lax.broadcasted_iota(jnp.int32, (CHUNK, pair_w), 1)
    sr = lax.broadcasted_iota(jnp.int32, (2 * GLA_DV, pair_w), 0)
    sc = lax.broadcasted_iota(jnp.int32, (2 * GLA_DV, pair_w), 1)
    live = (sr // GLA_DV) == (sc // GLA_DK)
    nw = nw_ref[...]

    def chunk_body(c, carry):
        r0 = pl.multiple_of(c * CHUNK, CHUNK)
        pairs = range(n_pairs)
        heads = range(GLA_HEADS)
        psl = [slice(p * pair_w, (p + 1) * pair_w) for p in pairs]
        hsl = [slice(h * GLA_DV, (h + 1) * GLA_DV) for h in heads]
        g = g_scr[pl.ds(r0, CHUNK), :]
        g1 = g.astype(BF16)
        rem = g - g1.astype(F32)
        g2 = rem.astype(BF16)
        g3 = (rem - g2.astype(F32)).astype(BF16)
        b_all = _dot(lower, g1) + (_dot(lower, g2) + _dot(lower, g3))
        b = [b_all[:, psl[p]] for p in pairs]
        b_mid = [a[CHUNK // 2:CHUNK // 2 + 1, :] for a in b]
        b_last = [a[CHUNK - 1:CHUNK, :] for a in b]
        q = [q_ref[pl.ds(r0, CHUNK), psl[p]].astype(F32) * (GLA_DK ** -0.5) for p in pairs]
        k = [k_ref[pl.ds(r0, CHUNK), psl[p]].astype(F32) for p in pairs]
        q_i = [q[p] * jnp.exp(b[p] - b_mid[p]) for p in pairs]
        k_i = [(k[p] * jnp.exp(b_mid[p] - b[p])).astype(BF16) for p in pairs]
        q_s = [(q[p] * jnp.exp(b[p])).astype(BF16) for p in pairs]
        k_s = [(k[p] * jnp.exp(b_last[p] - b[p])).astype(BF16) for p in pairs]
        vp = [v_ref[pl.ds(r0, CHUNK), 2 * p * GLA_DV:(2 * p + 2) * GLA_DV] for p in pairs]
        st = [st_scr[p] for p in pairs]
        o_inter = [_dot_nt(q_s[p], st[p].astype(BF16)) for p in pairs]
        upd = [_dot_tn(vp[p], k_s[p]) for p in pairs]
        for p in pairs:
            st_scr[p] = st[p] * jnp.exp(b_last[p]) + jnp.where(live, upd[p], 0.0)
        q_h = [jnp.where((lane // GLA_DK) == (h % 2), q_i[h // 2], 0.0).astype(BF16) for h in heads]
        att = [jnp.where(incl, _dot_nt(q_h[h], k_i[h // 2]), 0.0).astype(BF16) for h in heads]
        o = [_dot(att[h], vp[h // 2][:, (h % 2) * GLA_DV:(h % 2 + 1) * GLA_DV])
             + o_inter[h // 2][:, (h % 2) * GLA_DV:(h % 2 + 1) * GLA_DV] for h in heads]
        for h in heads:
            oh = o[h] * lax.rsqrt(jnp.mean(o[h] * o[h], axis=-1, keepdims=True) + EPS) * nw
            rg = r_ref[pl.ds(r0, CHUNK), hsl[h]].astype(F32)
            o_ref[pl.ds(r0, CHUNK), hsl[h]] = (oh * _silu(rg)).astype(o_ref.dtype)
        return carry

    lax.fori_loop(0, ts // CHUNK, chunk_body, 0)


def _gla(p, small, w_a2, b_a2, norm_w, bn, seq):
    t = p.shape[0]
    ts = 256
    per_b = seq // ts
    qk_w = GLA_HEADS * GLA_DK
    v_w = GLA_HEADS * GLA_DV
    gdn_cols = 4 * GDN_HEADS * GDN_DK
    tok = lambda b, s: b * per_b + s
    wa_pad = jnp.zeros((LANES, qk_w), F32).at[2 * GDN_HEADS:2 * GDN_HEADS + GLA_RANK].set(w_a2.astype(F32))
    return pl.pallas_call(
        functools.partial(_gla_kernel, ts=ts),
        grid=(bn, per_b),
        in_specs=[
            pl.BlockSpec((ts, qk_w), lambda b, s: (tok(b, s), gdn_cols // qk_w)),
            pl.BlockSpec((ts, qk_w), lambda b, s: (tok(b, s), gdn_cols // qk_w + 1)),
            pl.BlockSpec((ts, v_w), lambda b, s: (tok(b, s), (gdn_cols + 2 * qk_w) // v_w)),
            pl.BlockSpec((ts, v_w), lambda b, s: (tok(b, s), (gdn_cols + 2 * qk_w) // v_w + 1)),
            pl.BlockSpec((ts, LANES), lambda b, s: (tok(b, s), 0)),
            pl.BlockSpec((LANES, qk_w), lambda b, s: (0, 0)),
            pl.BlockSpec((1, qk_w), lambda b, s: (0, 0)),
            pl.BlockSpec((1, GLA_DV), lambda b, s: (0, 0)),
        ],
        out_specs=pl.BlockSpec((ts, v_w), lambda b, s: (tok(b, s), 0)),
        out_shape=jax.ShapeDtypeStruct((t, v_w), BF16),
        scratch_shapes=[
            pltpu.VMEM((GLA_HEADS // 2, 2 * GLA_DV, 2 * GLA_DK), F32),
            pltpu.VMEM((ts, qk_w), F32),
        ],
        compiler_params=_params("arbitrary", "arbitrary"),
        name="gla",
    )(p, p, p, p, small, wa_pad, b_a2.reshape(1, qk_w).astype(F32), norm_w.reshape(1, GLA_DV))


def _gelu_tanh(x):
    return 0.5 * x * (1.0 + jnp.tanh(0.7978845608028654 * (x + 0.044715 * x * x * x)))


def _lru_kernel(x_ref, g_ref, cw_ref, cb_ref, wra_ref, bra_ref, wri_ref, bri_ref, lam_ref, o_ref,
                tail_scr, h_scr, a_scr, u_scr, *, ts):
    width = x_ref.shape[1]
    blk = width // LRU_BLOCKS

    @pl.when(pl.program_id(1) == 0)
    def _():
        tail_scr[...] = jnp.zeros_like(tail_scr)
        h_scr[...] = jnp.zeros_like(h_scr)

    for gi in range(LRU_BLOCKS):
        lo, hi = gi * blk, (gi + 1) * blk
        cur = x_ref[:, lo:hi].astype(F32)
        full = jnp.concatenate([tail_scr[:, lo:hi], cur], axis=0)
        w = cw_ref[:, lo:hi]
        acc = full * w[CONV_W - 1:CONV_W]
        for i in range(CONV_W - 1):
            acc = acc + pltpu.roll(full, CONV_W - 1 - i, axis=0) * w[i:i + 1]
        xc = acc[SUBLANES:] + cb_ref[:, lo:hi]
        tail_scr[:, lo:hi] = cur[ts - SUBLANES:]
        xb = xc.astype(BF16)
        r = _sigmoid(_dot(xb, wra_ref[gi]) + bra_ref[:, lo:hi])
        ig = _sigmoid(_dot(xb, wri_ref[gi]) + bri_ref[:, lo:hi])
        log_a = -LRU_C * r * _softplus(-lam_ref[:, lo:hi])
        a = jnp.exp(log_a)
        a_scr[:, lo:hi] = a
        u_scr[:, lo:hi] = jnp.sqrt(1.0 - a * a) * (ig * xc)

    row = lax.broadcasted_iota(jnp.int32, (SUBLANES, width), 0)

    def scan_body(rb, h_prev):
        r0 = pl.multiple_of(rb * SUBLANES, SUBLANES)
        a = a_scr[pl.ds(r0, SUBLANES), :]
        u = u_scr[pl.ds(r0, SUBLANES), :]
        for d in (1, 2, 4):
            a_sh = jnp.where(row >= d, pltpu.roll(a, d, axis=0), 1.0)
            u_sh = jnp.where(row >= d, pltpu.roll(u, d, axis=0), 0.0)
            u = a * u_sh + u
            a = a * a_sh
        h = u + a * h_prev
        gate = g_ref[pl.ds(r0, SUBLANES), :].astype(F32)
        u_scr[pl.ds(r0, SUBLANES), :] = _gelu_tanh(gate) * h
        return h[SUBLANES - 1:SUBLANES, :]

    h_last = lax.fori_loop(0, ts // SUBLANES, scan_body, h_scr[...])
    h_scr[...] = h_last
    o_ref[...] = u_scr[...].astype(o_ref.dtype)


def _lru(p, conv_w, conv_b, w_ra, b_ra, w_ri, b_ri, lam, bn, seq):
    t = p.shape[0]
    width = p.shape[1] // 2
    blk = width // LRU_BLOCKS
    ts = 256
    per_b = seq // ts
    tok = lambda b, s: b * per_b + s
    vec = lambda a: a.reshape(1, width).astype(F32)
    vspec = pl.BlockSpec((1, width), lambda b, s: (0, 0))
    wspec = pl.BlockSpec((LRU_BLOCKS, blk, blk), lambda b, s: (0, 0, 0))
    return pl.pallas_call(
        functools.partial(_lru_kernel, ts=ts),
        grid=(bn, per_b),
        in_specs=[
            pl.BlockSpec((ts, width), lambda b, s: (tok(b, s), 0)),
            pl.BlockSpec((ts, width), lambda b, s: (tok(b, s), 1)),
            pl.BlockSpec((CONV_W, width), lambda b, s: (0, 0)),
            vspec, wspec, vspec, wspec, vspec, vspec,
        ],
        out_specs=pl.BlockSpec((ts, width), lambda b, s: (tok(b, s), 0)),
        out_shape=jax.ShapeDtypeStruct((t, width), BF16),
        scratch_shapes=[
            pltpu.VMEM((SUBLANES, width), F32),
            pltpu.VMEM((1, width), F32),
            pltpu.VMEM((ts, width), F32),
            pltpu.VMEM((ts, width), F32),
        ],
        compiler_params=_params("arbitrary", "arbitrary"),
        name="lru",
    )(p, p, conv_w, vec(conv_b), w_ra.astype(BF16), vec(b_ra), w_ri.astype(BF16), vec(b_ri), vec(lam))


def _pack_bf16_pairs(h):
    half = h.shape[1] // 2
    lo = lax.bitcast_convert_type(h[:, :half].astype(BF16).astype(F32), jnp.uint32)
    hi = lax.bitcast_convert_type(h[:, half:].astype(BF16).astype(F32), jnp.uint32)
    return (lo >> 16) | (hi & jnp.uint32(0xFFFF0000))


def _unpack_bf16_pairs(w):
    lo = lax.bitcast_convert_type(w << 16, F32).astype(BF16)
    hi = lax.bitcast_convert_type(w & jnp.uint32(0xFFFF0000), F32).astype(BF16)
    return lo, hi


def _out_proj_kernel(*refs, n_in):
    a_refs = refs[:n_in]
    w_refs = refs[n_in:2 * n_in]
    (x_ref, mod_ref, ln_ref, rwh_ref, rwl_ref, rb_ref,
     xo_ref, hp_ref, ti_ref, tw_ref, tr_ref, cnt_ref, cnt_scr) = refs[2 * n_in:]

    @pl.when(pl.program_id(0) == 0)
    def _():
        cnt_scr[...] = jnp.zeros_like(cnt_scr)

    y = _dot(a_refs[0][...], w_refs[0][...])
    for a_ref, w_ref in zip(a_refs[1:], w_refs[1:]):
        y = y + _dot(a_ref[...], w_ref[...])
    m = mod_ref[0]
    xn = x_ref[...] + m[2:3] * y
    xo_ref[...] = xn
    h = _norm_mod(xn, ln_ref[...], m[3:4], m[4:5])
    hp_ref[...] = _pack_bf16_pairs(h).reshape(hp_ref.shape)
    h_hi = h.astype(BF16)
    h_lo = (h - h_hi.astype(F32)).astype(BF16)
    logits = _dot(h_hi, rwh_ref[...]) + (_dot(h_hi, rwl_ref[...]) + _dot(h_lo, rwh_ref[...])) + rb_ref[...]
    lane = lax.broadcasted_iota(jnp.int32, logits.shape, 1)
    cur = jnp.where(lane < N_EXPERTS, logits, -jnp.inf)
    vals, idxs = [], []
    for _ in range(TOP_K):
        mx = jnp.max(cur, axis=-1, keepdims=True)
        ix = jnp.min(jnp.where(cur == mx, lane, LANES), axis=-1, keepdims=True)
        vals.append(mx)
        idxs.append(ix)
        cur = jnp.where(lane == ix, -jnp.inf, cur)
    es = [jnp.exp(v - vals[0]) for v in vals]
    den = es[0]
    for e in es[1:]:
        den = den + e
    tm = logits.shape[0]
    ri = lax.broadcasted_iota(jnp.int32, (tm, tm), 0)
    rj = lax.broadcasted_iota(jnp.int32, (tm, tm), 1)
    tri = jnp.where(rj < ri, 1.0, 0.0).astype(BF16)
    base = cnt_scr[...]
    ti = jnp.zeros(logits.shape, jnp.int32)
    tw = jnp.zeros(logits.shape, F32)
    tr = jnp.zeros(logits.shape, jnp.int32)
    for kk in range(TOP_K):
        sel = lane == idxs[kk]
        onehot = jnp.where(sel, 1.0, 0.0)
        before = _dot(tri, onehot.astype(BF16)) + base
        rank = jnp.sum(jnp.where(sel, before, 0.0), axis=-1, keepdims=True)
        base = base + jnp.sum(onehot, axis=0, keepdims=True)
        ti = jnp.where(lane == kk, idxs[kk], ti)
        tw = jnp.where(lane == kk, es[kk] / den, tw)
        tr = jnp.where(lane == kk, rank.astype(jnp.int32), tr)
    cnt_scr[...] = base
    ti_ref[...] = ti
    tw_ref[...] = tw
    tr_ref[...] = tr
    cnt_ref[...] = base.astype(jnp.int32)


def _out_proj(acts, weights, x, mod, ln_ffn, router_w, router_b, seq):
    t, d = x.shape
    tm = 256
    per_b = seq // tm
    n_in = len(acts)
    rw = jnp.zeros((d, LANES), F32).at[:, :N_EXPERTS].set(router_w.astype(F32))
    rw_hi = rw.astype(BF16)
    rw_lo = (rw - rw_hi.astype(F32)).astype(BF16)
    rb = jnp.zeros((1, LANES), F32).at[0, :N_EXPERTS].set(router_b.astype(F32))
    tiles = d // 2 // LANES
    in_specs = [pl.BlockSpec((tm, a.shape[1]), lambda i: (i, 0)) for a in acts]
    in_specs += [pl.BlockSpec(w.shape, lambda i: (0, 0)) for w in weights]
    in_specs += [
        pl.BlockSpec((tm, d), lambda i: (i, 0)),
        pl.BlockSpec((1, 6, d), lambda i: (i // per_b, 0, 0)),
        pl.BlockSpec((1, d), lambda i: (0, 0)),
        pl.BlockSpec((d, LANES), lambda i: (0, 0)),
        pl.BlockSpec((d, LANES), lambda i: (0, 0)),
        pl.BlockSpec((1, LANES), lambda i: (0, 0)),
    ]
    row_d = pl.BlockSpec((tm, d), lambda i: (i, 0))
    row_h = pl.BlockSpec((tm, tiles, LANES), lambda i: (i, 0, 0))
    row_l = pl.BlockSpec((tm, LANES), lambda i: (i, 0))
    return pl.pallas_call(
        functools.partial(_out_proj_kernel, n_in=n_in),
        grid=(t // tm,),
        in_specs=in_specs,
        out_specs=[row_d, row_h, row_l, row_l, row_l, pl.BlockSpec((1, LANES), lambda i: (0, 0))],
        out_shape=[
            jax.ShapeDtypeStruct((t, d), F32),
            jax.ShapeDtypeStruct((t, tiles, LANES), jnp.uint32),
            jax.ShapeDtypeStruct((t, LANES), jnp.int32),
            jax.ShapeDtypeStruct((t, LANES), F32),
            jax.ShapeDtypeStruct((t, LANES), jnp.int32),
            jax.ShapeDtypeStruct((1, LANES), jnp.int32),
        ],
        scratch_shapes=[pltpu.VMEM((1, LANES), F32)],
        compiler_params=_params("arbitrary"),
        name="out_proj",
    )(*acts, *weights, x, mod, ln_ffn.reshape(1, d), rw_hi, rw_lo, rb)


def _moe_routing(ti, tr, cnt, n_tok):
    n_blk = (n_tok * TOP_K) // MOE_BLOCK + N_EXPERTS
    counts = cnt[0, :N_EXPERTS]
    padded = ((counts + MOE_BLOCK - 1) // MOE_BLOCK) * MOE_BLOCK
    pend = jnp.cumsum(padded)
    pstart = pend - padded
    e = ti[:, :TOP_K]
    hit = e[:, :, None] == jnp.arange(N_EXPERTS, dtype=jnp.int32)
    dest = tr[:, :TOP_K] + jnp.sum(jnp.where(hit, pstart, 0), axis=-1)
    blk_start = jnp.arange(n_blk, dtype=jnp.int32) * MOE_BLOCK
    blk_e = jnp.minimum(jnp.sum(pend[None, :] <= blk_start[:, None], axis=1), N_EXPERTS - 1).astype(jnp.int32)
    return dest.astype(jnp.int32), blk_e


DISPATCH_TOKENS = 512
COMBINE_TOKENS = 128
ISSUE_TOKENS = 4


def _dispatch_kernel(dest_ref, hp_ref, xs_in_hbm, xs_hbm, sem):
    del xs_in_hbm

    def row_copy(tl, dst):
        return pltpu.make_async_copy(hp_ref.at[tl], xs_hbm.at[dst], sem.at[0])

    def issue(g, c):
        t0 = g * ISSUE_TOKENS
        dsts = [dest_ref[0, 0, t0 * TOP_K + j] for j in range(ISSUE_TOKENS * TOP_K)]
        for j, dst in enumerate(dsts):
            row_copy(t0 + j // TOP_K, dst).start()
        return c

    lax.fori_loop(0, DISPATCH_TOKENS // ISSUE_TOKENS, issue, 0)

    for _ in range(TOP_K):
        pltpu.make_async_copy(hp_ref, xs_hbm.at[pl.ds(0, DISPATCH_TOKENS)], sem.at[0]).wait()


def _dispatch(hp, dest, n_buf):
    t = hp.shape[0]
    n_steps = t // DISPATCH_TOKENS
    ids = dest.reshape(n_steps, 1, DISPATCH_TOKENS * TOP_K)
    return pl.pallas_call(
        _dispatch_kernel,
        grid=(n_steps,),
        in_specs=[
            pl.BlockSpec((1, 1, DISPATCH_TOKENS * TOP_K), lambda i: (i, 0, 0), memory_space=pltpu.SMEM),
            pl.BlockSpec((DISPATCH_TOKENS,) + hp.shape[1:], lambda i: (i, 0, 0)),
            pl.BlockSpec(memory_space=pl.ANY),
        ],
        out_specs=pl.BlockSpec(memory_space=pl.ANY),
        out_shape=jax.ShapeDtypeStruct((n_buf,) + hp.shape[1:], jnp.uint32),
        scratch_shapes=[pltpu.SemaphoreType.DMA((1,))],
        input_output_aliases={2: 0},
        compiler_params=_params("arbitrary"),
        name="moe_dispatch",
    )(ids, hp, jnp.zeros((n_buf,) + hp.shape[1:], jnp.uint32))


def _moe_kernel(blk_e_ref, xs_ref, wgu_ref, bgu_ref, wdn_ref, bdn_ref, y_ref):
    del blk_e_ref
    half = xs_ref.shape[1] * xs_ref.shape[2]
    d_ff = wdn_ref.shape[1]
    lo, hi = _unpack_bf16_pairs(xs_ref[...].reshape(MOE_BLOCK, half))
    gu = _dot(lo, wgu_ref[0, :half, :]) + _dot(hi, wgu_ref[0, half:, :]) + bgu_ref[0]
    gate = jnp.minimum(gu[:, :d_ff], SWIGLU_LIMIT)
    up = jnp.clip(gu[:, d_ff:], -SWIGLU_LIMIT, SWIGLU_LIMIT)
    act = (up + 1.0) * gate * _sigmoid(SWIGLU_ALPHA * gate)
    y = _dot(act.astype(BF16), wdn_ref[0]) + bdn_ref[0]
    y_ref[...] = y.reshape(y_ref.shape)


def _moe_experts(xs, blk_e, w_gu, b_gu, w_dn, b_dn):
    n_buf, tiles, _ = xs.shape
    d = 2 * tiles * LANES
    d_ff = w_dn.shape[1]
    n_w = w_gu.shape[0]
    grid_spec = pltpu.PrefetchScalarGridSpec(
        num_scalar_prefetch=1,
        grid=(n_buf // MOE_BLOCK,),
        in_specs=[
            pl.BlockSpec((MOE_BLOCK, tiles, LANES), lambda b, e: (b, 0, 0)),
            pl.BlockSpec((1, d, 2 * d_ff), lambda b, e: (e[b], 0, 0)),
            pl.BlockSpec((1, 1, 2 * d_ff), lambda b, e: (e[b], 0, 0)),
            pl.BlockSpec((1, d_ff, d), lambda b, e: (e[b], 0, 0)),
            pl.BlockSpec((1, 1, d), lambda b, e: (e[b], 0, 0)),
        ],
        out_specs=pl.BlockSpec((MOE_BLOCK, d // LANES, LANES), lambda b, e: (b, 0, 0)),
    )
    return pl.pallas_call(
        _moe_kernel,
        grid_spec=grid_spec,
        out_shape=jax.ShapeDtypeStruct((n_buf, d // LANES, LANES), F32),
        compiler_params=_params("arbitrary"),
        name="moe_experts",
    )(blk_e, xs, w_gu, b_gu.reshape(n_w, 1, 2 * d_ff), w_dn, b_dn.reshape(n_w, 1, d))


def _combine_kernel(dest_ref, dest_next_ref, x_ref, tw_ref, mod_ref, lnf_ref, y_hbm, o_ref, ybuf, sem, *, final):
    i = pl.program_id(0)
    n = pl.num_programs(0)
    slot = i % 2
    tm = COMBINE_TOKENS

    def issue(ids_ref, s):
        def body(g, c):
            t0 = g * ISSUE_TOKENS
            srcs = [ids_ref[0, 0, t0 * TOP_K + j] for j in range(ISSUE_TOKENS * TOP_K)]
            for j, src in enumerate(srcs):
                pltpu.make_async_copy(y_hbm.at[src], ybuf.at[s, (j % TOP_K) * tm + t0 + j // TOP_K],
                                      sem.at[s]).start()
            return c
        lax.fori_loop(0, tm // ISSUE_TOKENS, body, 0)

    @pl.when(i == 0)
    def _():
        issue(dest_ref, 0)

    @pl.when(i + 1 < n)
    def _():
        issue(dest_next_ref, 1 - slot)

    pltpu.make_async_copy(y_hbm.at[pl.ds(0, TOP_K * tm)], ybuf.at[slot], sem.at[slot]).wait()
    tw = tw_ref[...]
    acc = None
    for kk in range(TOP_K):
        gate = jnp.broadcast_to(tw[:, kk:kk + 1], (tm, LANES)).reshape(tm, 1, LANES)
        term = gate * ybuf[slot, kk * tm:(kk + 1) * tm]
        acc = term if acc is None else acc + term
    xn = x_ref[...] + mod_ref[0][5:6] * acc.reshape(x_ref.shape)
    if final:
        xn = xn * lax.rsqrt(jnp.mean(xn * xn, axis=-1, keepdims=True) + EPS) * lnf_ref[...]
    o_ref[...] = xn


def _combine(x, y, dest, tw, mod, ln_final, seq, final):
    t, d = x.shape
    tm = COMBINE_TOKENS
    per_b = seq // tm
    nt = t // tm
    ids = dest.reshape(nt, 1, tm * TOP_K)
    id_spec = lambda f: pl.BlockSpec((1, 1, tm * TOP_K), f, memory_space=pltpu.SMEM)
    return pl.pallas_call(
        functools.partial(_combine_kernel, final=final),
        grid=(nt,),
        in_specs=[
            id_spec(lambda i: (i, 0, 0)),
            id_spec(lambda i: (jnp.minimum(i + 1, nt - 1), 0, 0)),
            pl.BlockSpec((tm, d), lambda i: (i, 0)),
            pl.BlockSpec((tm, LANES), lambda i: (i, 0)),
            pl.BlockSpec((1, 6, d), lambda i: (i // per_b, 0, 0)),
            pl.BlockSpec((1, d), lambda i: (0, 0)),
            pl.BlockSpec(memory_space=pl.ANY),
        ],
        out_specs=pl.BlockSpec((tm, d), lambda i: (i, 0)),
        out_shape=jax.ShapeDtypeStruct((t, d), F32),
        scratch_shapes=[pltpu.VMEM((2, TOP_K * tm, d // LANES, LANES), F32), pltpu.SemaphoreType.DMA((2,))],
        compiler_params=_params("arbitrary"),
        name="moe_combine",
    )(ids, ids, x, tw, mod, ln_final.reshape(1, d), y)


def _ffn(x, hp, ti, tw, tr, cnt, mod, layer, w_gu, b_gu, w_dn, b_dn, ln_final, seq, final):
    t = x.shape[0]
    dest, blk_e = _moe_routing(ti, tr, cnt, t)
    n_buf = blk_e.shape[0] * MOE_BLOCK
    xs = _dispatch(hp, dest, n_buf)
    y = _moe_experts(xs, blk_e + layer * N_EXPERTS, w_gu, b_gu, w_dn, b_dn)
    return _combine(x, y, dest, tw, mod, ln_final, seq, final)


def _even_inproj_weights(w_in):
    gdn_cols = 4 * GDN_HEADS * GDN_DK
    small0 = gdn_cols
    gla0 = small0 + 2 * GDN_HEADS
    gla_cols = 2 * GLA_HEADS * GLA_DK + 2 * GLA_HEADS * GLA_DV
    lg0 = gla0 + gla_cols
    w_big = jnp.concatenate([w_in[:, :gdn_cols], w_in[:, gla0:lg0]], axis=1).astype(BF16)
    w_small = jnp.zeros((w_in.shape[0], LANES), F32)
    w_small = w_small.at[:, :2 * GDN_HEADS].set(w_in[:, small0:gla0])
    w_small = w_small.at[:, 2 * GDN_HEADS:2 * GDN_HEADS + GLA_RANK].set(w_in[:, lg0:lg0 + GLA_RANK])
    return w_big, w_small.astype(BF16)


def _even_mixer(x, mod, ln_mix, w_in, conv_w, a_log, dt_bias, gdn_norm, w_a2, b_a2, gla_norm, bn, seq):
    w_big, w_small = _even_inproj_weights(w_in)
    p, small = _norm_proj(x, ln_mix, mod, w_big, w_small, seq)
    col, row = _gdn_gates(small, a_log, dt_bias)
    o_a = _gdn(p, col, row, conv_w, gdn_norm, bn, seq)
    o_b = _gla(p, small, w_a2, b_a2, gla_norm, bn, seq)
    return o_a, o_b


def kernel(x, c, ln_mix, ln_ffn, ln_final, w_ada, b_ada, e_w_in, e_conv_w, e_a_log, e_dt_bias, e_gdn_norm, e_gla_w_a2, e_gla_b_a2, e_gla_norm, e_w_out, o_w_in, o_conv_w, o_conv_b, o_w_ra, o_b_ra, o_w_ri, o_b_ri, o_lam, o_w_out, router_w, router_b, exp_w_gu, exp_b_gu, exp_w_dn, exp_b_dn):
    bn, seq, d = x.shape
    depth = w_ada.shape[0]
    mods = _ada_mod(c, w_ada, b_ada)
    xf = x.reshape(bn * seq, d)
    w_gu = exp_w_gu.astype(BF16).reshape((-1,) + exp_w_gu.shape[2:])
    w_dn = exp_w_dn.astype(BF16).reshape((-1,) + exp_w_dn.shape[2:])
    b_gu = exp_b_gu.reshape((-1,) + exp_b_gu.shape[2:])
    b_dn = exp_b_dn.reshape((-1,) + exp_b_dn.shape[2:])
    for layer in range(depth):
        mod = mods[layer]
        j = layer // 2
        if layer % 2 == 0:
            o_a, o_b = _even_mixer(xf, mod, ln_mix[layer], e_w_in[j], e_conv_w[j], e_a_log[j], e_dt_bias[j],
                                   e_gdn_norm[j], e_gla_w_a2[j], e_gla_b_a2[j], e_gla_norm[j], bn, seq)
            split = o_a.shape[1]
            acts = [o_a, o_b]
            weights = [e_w_out[j][:split].astype(BF16), e_w_out[j][split:].astype(BF16)]
        else:
            p = _norm_proj(xf, ln_mix[layer], mod, o_w_in[j].astype(BF16), None, seq)
            acts = [_lru(p, o_conv_w[j], o_conv_b[j], o_w_ra[j], o_b_ra[j], o_w_ri[j], o_b_ri[j], o_lam[j], bn, seq)]
            weights = [o_w_out[j].astype(BF16)]
        xf, hp, ti, tw, tr, cnt = _out_proj(acts, weights, xf, mod, ln_ffn[layer], router_w[layer], router_b[layer], seq)
        xf = _ffn(xf, hp, ti, tw, tr, cnt, mod, layer, w_gu, b_gu, w_dn, b_dn,
                  ln_final, seq, final=(layer == depth - 1))
    return xf.reshape(bn, seq, d)
```

```python
import functools

import jax
import jax.numpy as jnp
from jax import lax
from jax.experimental import pallas as pl
from jax.experimental.pallas import tpu as pltpu

F32 = jnp.float32
BF16 = jnp.bfloat16
HIGHEST = lax.Precision.HIGHEST

EPS = 1e-6
CONV_W = 4
CHUNK = 64
GDN_HEADS = 8
GDN_DK = 128
GDN_DV = 128
GLA_HEADS = 8
GLA_DK = 64
GLA_DV = 128
GLA_RANK = 16
GLA_GATE_NORM = 16.0
LRU_BLOCKS = 8
LRU_C = 8.0
N_EXPERTS = 32
TOP_K = 4
SWIGLU_LIMIT = 7.0
SWIGLU_ALPHA = 1.702
MOE_BLOCK = 256

LANES = 128
SUBLANES = 8
VMEM_LIMIT = 56 * 1024 * 1024


def _dot(a, b, precision=None):
    return jnp.dot(a, b, preferred_element_type=F32, precision=precision)


def _dot_nt(a, b, precision=None):
    return lax.dot_general(a, b, (((1,), (1,)), ((), ())), preferred_element_type=F32, precision=precision)


def _dot_tn(a, b):
    return lax.dot_general(a, b, (((0,), (0,)), ((), ())), preferred_element_type=F32)


def _dot_split(a, b):
    a_hi = a.astype(BF16)
    b_hi = b.astype(BF16)
    a_lo = (a - a_hi.astype(F32)).astype(BF16)
    b_lo = (b - b_hi.astype(F32)).astype(BF16)
    return _dot(a_hi, b_hi) + (_dot(a_hi, b_lo) + _dot(a_lo, b_hi))


def _sigmoid(x):
    return 1.0 / (1.0 + jnp.exp(-x))


def _softplus(x):
    return jnp.maximum(x, 0.0) + jnp.log(1.0 + jnp.exp(-jnp.abs(x)))


def _silu(x):
    return x * _sigmoid(x)


def _params(*sem):
    return pltpu.CompilerParams(dimension_semantics=sem, vmem_limit_bytes=VMEM_LIMIT)


def _ada_kernel(c_ref, w_ref, b_ref, o_ref):
    c = c_ref[...]
    o_ref[0] = _dot(_silu(c), w_ref[0], precision=HIGHEST) + b_ref[0]


def _ada_mod(c, w_ada, b_ada):
    depth, d, n6 = w_ada.shape
    bn = c.shape[0]
    rows = -(-bn // SUBLANES) * SUBLANES
    cp = jnp.zeros((rows, d), F32).at[:bn].set(c)
    tn = 768
    out = pl.pallas_call(
        _ada_kernel,
        grid=(depth, n6 // tn),
        in_specs=[
            pl.BlockSpec((rows, d), lambda l, j: (0, 0)),
            pl.BlockSpec((1, d, tn), lambda l, j: (l, 0, j)),
            pl.BlockSpec((1, 1, tn), lambda l, j: (l, 0, j)),
        ],
        out_specs=pl.BlockSpec((1, rows, tn), lambda l, j: (l, 0, j)),
        out_shape=jax.ShapeDtypeStruct((depth, rows, n6), F32),
        compiler_params=_params("arbitrary", "arbitrary"),
        name="ada_mod",
    )(cp, w_ada, b_ada.reshape(depth, 1, n6))
    return out[:, :bn].reshape(depth, bn, 6, d)


def _norm_mod(x, ln, shift, scale):
    y = x * lax.rsqrt(jnp.mean(x * x, axis=-1, keepdims=True) + EPS) * ln
    return y * (1.0 + scale) + shift


def _norm_proj_kernel(x_ref, ln_ref, mod_ref, w_ref, *rest, has_small):
    if has_small:
        ws_ref, o_ref, os_ref, h_scr = rest
    else:
        o_ref, h_scr = rest
    j = pl.program_id(1)

    @pl.when(j == 0)
    def _():
        m = mod_ref[0]
        h = _norm_mod(x_ref[...], ln_ref[...], m[0:1], m[1:2]).astype(BF16)
        h_scr[...] = h
        if has_small:
            os_ref[...] = _dot(h, ws_ref[...])

    o_ref[...] = _dot(h_scr[...], w_ref[...]).astype(o_ref.dtype)


def _norm_proj(x, ln, mod, w, w_small, seq):
    t, d = x.shape
    n = w.shape[1]
    tm = 512
    tn = max(c for c in range(LANES, 2048 + 1, LANES) if n % c == 0)
    per_b = seq // tm
    has_small = w_small is not None
    in_specs = [
        pl.BlockSpec((tm, d), lambda i, j: (i, 0)),
        pl.BlockSpec((1, d), lambda i, j: (0, 0)),
        pl.BlockSpec((1, 6, d), lambda i, j: (i // per_b, 0, 0)),
        pl.BlockSpec((d, tn), lambda i, j: (0, j)),
    ]
    out_specs = [pl.BlockSpec((tm, tn), lambda i, j: (i, j))]
    out_shape = [jax.ShapeDtypeStruct((t, n), BF16)]
    args = [x, ln.reshape(1, d), mod, w]
    if has_small:
        in_specs.append(pl.BlockSpec((d, LANES), lambda i, j: (0, 0)))
        out_specs.append(pl.BlockSpec((tm, LANES), lambda i, j: (i, 0)))
        out_shape.append(jax.ShapeDtypeStruct((t, LANES), F32))
        args.append(w_small)
    outs = pl.pallas_call(
        functools.partial(_norm_proj_kernel, has_small=has_small),
        grid=(t // tm, n // tn),
        in_specs=in_specs,
        out_specs=out_specs,
        out_shape=out_shape,
        scratch_shapes=[pltpu.VMEM((tm, d), BF16)],
        compiler_params=_params("arbitrary", "arbitrary"),
        name="norm_proj",
    )(*args)
    return outs if has_small else outs[0]


def _gdn_gate_kernel(ga_ref, gb_ref, alog_ref, dtb_ref, gc_ref, beta_ref, eg_ref, ek_ref):
    g = -jnp.exp(alog_ref[...]) * _softplus(ga_ref[...] + dtb_ref[...])
    ci = lax.broadcasted_iota(jnp.int32, (LANES, LANES), 0)
    cj = lax.broadcasted_iota(jnp.int32, (LANES, LANES), 1)
    same = (ci // CHUNK) == (cj // CHUNK)
    upper = jnp.where(same & (ci <= cj), 1.0, 0.0).astype(F32)
    allc = jnp.where(same, 1.0, 0.0).astype(F32)
    gc = _dot(g, upper, precision=HIGHEST)
    g_last = _dot(g, allc, precision=HIGHEST)
    gc_ref[...] = gc
    beta_ref[...] = _sigmoid(gb_ref[...])
    eg_ref[...] = jnp.exp(gc)
    ek_ref[...] = jnp.exp(g_last - gc)


def _gdn_gates(small, a_log, dt_bias):
    t = small.shape[0]
    nb = t // LANES
    rows = GDN_HEADS * nb
    gb_t = small[:, 0:GDN_HEADS].T.reshape(rows, LANES)
    ga_t = small[:, GDN_HEADS:2 * GDN_HEADS].T.reshape(rows, LANES)
    alog_c = jnp.repeat(a_log.astype(F32), nb).reshape(rows, 1)
    dtb_c = jnp.repeat(dt_bias.astype(F32), nb).reshape(rows, 1)
    full = pl.BlockSpec((rows, LANES), lambda: (0, 0))
    col1 = pl.BlockSpec((rows, 1), lambda: (0, 0))
    gc, beta, eg, ek = pl.pallas_call(
        _gdn_gate_kernel,
        in_specs=[full, full, col1, col1],
        out_specs=[full] * 4,
        out_shape=[jax.ShapeDtypeStruct((rows, LANES), F32)] * 4,
        compiler_params=pltpu.CompilerParams(vmem_limit_bytes=VMEM_LIMIT),
        name="gdn_gates",
    )(ga_t, gb_t, alog_c, dtb_c)
    to_col = lambda a: a.reshape(GDN_HEADS, t).T
    col = jnp.concatenate([to_col(beta), to_col(gc), to_col(eg), to_col(ek)], axis=1)
    row = gc.reshape(GDN_HEADS, t // CHUNK, CHUNK).transpose(1, 0, 2)
    return col, row


def _gdn_kernel(q_ref, k_ref, v_ref, z_ref, col_ref, row_ref, cw_ref, nw_ref, o_ref,
                s_scr, tail_scr, qkv_scr, *, ts):
    width = GDN_HEADS * GDN_DK

    @pl.when(pl.program_id(1) == 0)
    def _():
        s_scr[...] = jnp.zeros_like(s_scr)
        tail_scr[...] = jnp.zeros_like(tail_scr)

    strip = 256
    for part, ref in enumerate((q_ref, k_ref, v_ref)):
        for s0 in range(0, width, strip):
            c0 = part * width + s0
            cur = ref[:, s0:s0 + strip].astype(F32)
            full = jnp.concatenate([tail_scr[:, c0:c0 + strip], cur], axis=0)
            w = cw_ref[:, c0:c0 + strip]
            acc = full * w[CONV_W - 1:CONV_W]
            for i in range(CONV_W - 1):
                acc = acc + pltpu.roll(full, CONV_W - 1 - i, axis=0) * w[i:i + 1]
            y = acc[SUBLANES:]
            qkv_scr[:, c0:c0 + strip] = _silu(y)
            tail_scr[:, c0:c0 + strip] = cur[ts - SUBLANES:]

    ri = lax.broadcasted_iota(jnp.int32, (CHUNK, CHUNK), 0)
    rj = lax.broadcasted_iota(jnp.int32, (CHUNK, CHUNK), 1)
    incl = ri >= rj
    strict = ri > rj
    eye = jnp.where(ri == rj, 1.0, 0.0).astype(F32)
    nw = nw_ref[...]

    def chunk_body(c, carry):
        r0 = pl.multiple_of(c * CHUNK, CHUNK)
        colc = col_ref[pl.ds(r0, CHUNK), :]
        rowc = row_ref[c]
        heads = range(GDN_HEADS)
        sl = [slice(h * GDN_DK, (h + 1) * GDN_DK) for h in heads]
        q = [qkv_scr[pl.ds(r0, CHUNK), sl[h]] for h in heads]
        k = [qkv_scr[pl.ds(r0, CHUNK), width + h * GDN_DK:width + (h + 1) * GDN_DK] for h in heads]
        v = [qkv_scr[pl.ds(r0, CHUNK), 2 * width + h * GDN_DK:2 * width + (h + 1) * GDN_DK] for h in heads]
        q = [a * lax.rsqrt(jnp.sum(a * a, axis=-1, keepdims=True) + EPS) * (GDN_DK ** -0.5) for a in q]
        k = [a * lax.rsqrt(jnp.sum(a * a, axis=-1, keepdims=True) + EPS) for a in k]
        beta = [colc[:, h:h + 1] for h in heads]
        gc = [colc[:, 8 + h:9 + h] for h in heads]
        eg = [colc[:, 16 + h:17 + h] for h in heads]
        ek = [colc[:, 24 + h:25 + h] for h in heads]
        decay = [jnp.exp(jnp.where(incl, gc[h] - rowc[h:h + 1, :], -jnp.inf)) for h in heads]
        kb = [a.astype(BF16) for a in k]
        qb = [a.astype(BF16) for a in q]
        kk = [_dot_nt(kb[h], kb[h]) for h in heads]
        qk = [_dot_nt(qb[h], kb[h]) for h in heads]
        npow = [jnp.where(strict, -(beta[h] * kk[h] * decay[h]), 0.0) for h in heads]
        ainv = [eye + npow[h] for h in heads]
        for _ in range(5):
            npow = [_dot_split(npow[h], npow[h]) for h in heads]
            ainv = [ainv[h] + _dot_split(ainv[h], npow[h]) for h in heads]
        ab = [a.astype(BF16) for a in ainv]
        u = [_dot(ab[h], (v[h] * beta[h]).astype(BF16)) for h in heads]
        w = [_dot(ab[h], (k[h] * (beta[h] * eg[h])).astype(BF16)) for h in heads]
        qk = [jnp.where(incl, qk[h] * decay[h], 0.0).astype(BF16) for h in heads]
        q_s = [(q[h] * eg[h]).astype(BF16) for h in heads]
        k_s = [(k[h] * ek[h]).astype(BF16) for h in heads]
        state = [s_scr[h] for h in heads]
        sb = [a.astype(BF16) for a in state]
        v_new = [(u[h] - _dot(w[h].astype(BF16), sb[h])).astype(BF16) for h in heads]
        o = [_dot(q_s[h], sb[h]) + _dot(qk[h], v_new[h]) for h in heads]
        for h in heads:
            s_scr[h] = state[h] * eg[h][CHUNK - 1:CHUNK, :] + _dot_tn(k_s[h], v_new[h])
        for h in heads:
            oh = o[h] * lax.rsqrt(jnp.mean(o[h] * o[h], axis=-1, keepdims=True) + EPS) * nw
            z = z_ref[pl.ds(r0, CHUNK), sl[h]].astype(F32)
            o_ref[pl.ds(r0, CHUNK), sl[h]] = (oh * _silu(z)).astype(o_ref.dtype)
        return carry

    lax.fori_loop(0, ts // CHUNK, chunk_body, 0)


def _gdn(p, col, row, conv_w, norm_w, bn, seq):
    t = p.shape[0]
    ts = 256
    per_b = seq // ts
    width = GDN_HEADS * GDN_DK
    tok = lambda b, s: b * per_b + s
    pspec = lambda cb: pl.BlockSpec((ts, width), lambda b, s: (tok(b, s), cb))
    return pl.pallas_call(
        functools.partial(_gdn_kernel, ts=ts),
        grid=(bn, per_b),
        in_specs=[
            pspec(0), pspec(1), pspec(2), pspec(3),
            pl.BlockSpec((ts, 4 * GDN_HEADS), lambda b, s: (tok(b, s), 0)),
            pl.BlockSpec((ts // CHUNK, GDN_HEADS, CHUNK), lambda b, s: (tok(b, s), 0, 0)),
            pl.BlockSpec((CONV_W, 3 * width), lambda b, s: (0, 0)),
            pl.BlockSpec((1, GDN_DV), lambda b, s: (0, 0)),
        ],
        out_specs=pl.BlockSpec((ts, width), lambda b, s: (tok(b, s), 0)),
        out_shape=jax.ShapeDtypeStruct((t, width), BF16),
        scratch_shapes=[
            pltpu.VMEM((GDN_HEADS, GDN_DK, GDN_DV), F32),
            pltpu.VMEM((SUBLANES, 3 * width), F32),
            pltpu.VMEM((ts, 3 * width), F32),
        ],
        compiler_params=_params("arbitrary", "arbitrary"),
        name="gdn",
    )(p, p, p, p, col, row, conv_w, norm_w.reshape(1, GDN_DV))


def _gla_kernel(q_ref, k_ref, v_ref, r_ref, sm_ref, wa_ref, ba_ref, nw_ref, o_ref, st_scr, g_scr, *, ts):
    pair_w = 2 * GLA_DK
    n_pairs = GLA_HEADS // 2

    @pl.when(pl.program_id(1) == 0)
    def _():
        st_scr[...] = jnp.zeros_like(st_scr)

    logits = _dot(sm_ref[...], wa_ref[...], precision=HIGHEST) + ba_ref[...]
    g_scr[...] = -_softplus(-logits) / GLA_GATE_NORM

    ri = lax.broadcasted_iota(jnp.int32, (CHUNK, CHUNK), 0)
    rj = lax.broadcasted_iota(jnp.int32, (CHUNK, CHUNK), 1)
    incl = ri >= rj
    lower = jnp.where(incl, 1.0, 0.0).astype(BF16)
    lane = lax.broadcasted_iota(jnp.int32, (CHUNK, pair_w), 1)
    sr = lax.broadcasted_iota(jnp.int32, (2 * GLA_DV, pair_w), 0)
    sc = lax.broadcasted_iota(jnp.int32, (2 * GLA_DV, pair_w), 1)
    live = (sr // GLA_DV) == (sc // GLA_DK)
    nw = nw_ref[...]

    def chunk_body(c, carry):
        r0 = pl.multiple_of(c * CHUNK, CHUNK)
        pairs = range(n_pairs)
        heads = range(GLA_HEADS)
        psl = [slice(p * pair_w, (p + 1) * pair_w) for p in pairs]
        hsl = [slice(h * GLA_DV, (h + 1) * GLA_DV) for h in heads]
        g = g_scr[pl.ds(r0, CHUNK), :]
        g1 = g.astype(BF16)
        rem = g - g1.astype(F32)
        g2 = rem.astype(BF16)
        g3 = (rem - g2.astype(F32)).astype(BF16)
        b_all = _dot(lower, g1) + (_dot(lower, g2) + _dot(lower, g3))
        b = [b_all[:, psl[p]] for p in pairs]
        b_mid = [a[CHUNK // 2:CHUNK // 2 + 1, :] for a in b]
        b_last = [a[CHUNK - 1:CHUNK, :] for a in b]
        q = [q_ref[pl.ds(r0, CHUNK), psl[p]].astype(F32) * (GLA_DK ** -0.5) for p in pairs]
        k = [k_ref[pl.ds(r0, CHUNK), psl[p]].astype(F32) for p in pairs]
        q_i = [q[p] * jnp.exp(b[p] - b_mid[p]) for p in pairs]
        k_i = [(k[p] * jnp.exp(b_mid[p] - b[p])).astype(BF16) for p in pairs]
        q_s = [(q[p] * jnp.exp(b[p])).astype(BF16) for p in pairs]
        k_s = [(k[p] * jnp.exp(b_last[p] - b[p])).astype(BF16) for p in pairs]
        vp = [v_ref[pl.ds(r0, CHUNK), 2 * p * GLA_DV:(2 * p + 2) * GLA_DV] for p in pairs]
        st = [st_scr[p] for p in pairs]
        o_inter = [_dot_nt(q_s[p], st[p].astype(BF16)) for p in pairs]
        upd = [_dot_tn(vp[p], k_s[p]) for p in pairs]
        for p in pairs:
            st_scr[p] = st[p] * jnp.exp(b_last[p]) + jnp.where(live, upd[p], 0.0)
        q_h = [jnp.where((lane // GLA_DK) == (h % 2), q_i[h // 2], 0.0).astype(BF16) for h in heads]
        att = [jnp.where(incl, _dot_nt(q_h[h], k_i[h // 2]), 0.0).astype(BF16) for h in heads]
        o = [_dot(att[h], vp[h // 2][:, (h % 2) * GLA_DV:(h % 2 + 1) * GLA_DV])
             + o_inter[h // 2][:, (h % 2) * GLA_DV:(h % 2 + 1) * GLA_DV] for h in heads]
        for h in heads:
            oh = o[h] * lax.rsqrt(jnp.mean(o[h] * o[h], axis=-1, keepdims=True) + EPS) * nw
            rg = r_ref[pl.ds(r0, CHUNK), hsl[h]].astype(F32)
            o_ref[pl.ds(r0, CHUNK), hsl[h]] = (oh * _silu(rg)).astype(o_ref.dtype)
        return carry

    lax.fori_loop(0, ts // CHUNK, chunk_body, 0)


def _gla(p, small, w_a2, b_a2, norm_w, bn, seq):
    t = p.shape[0]
    ts = 256
    per_b = seq // ts
    qk_w = GLA_HEADS * GLA_DK
    v_w = GLA_HEADS * GLA_DV
    gdn_cols = 4 * GDN_HEADS * GDN_DK
    tok = lambda b, s: b * per_b + s
    wa_pad = jnp.zeros((LANES, qk_w), F32).at[2 * GDN_HEADS:2 * GDN_HEADS + GLA_RANK].set(w_a2.astype(F32))
    return pl.pallas_call(
        functools.partial(_gla_kernel, ts=ts),
        grid=(bn, per_b),
        in_specs=[
            pl.BlockSpec((ts, qk_w), lambda b, s: (tok(b, s), gdn_cols // qk_w)),
            pl.BlockSpec((ts, qk_w), lambda b, s: (tok(b, s), gdn_cols // qk_w + 1)),
            pl.BlockSpec((ts, v_w), lambda b, s: (tok(b, s), (gdn_cols + 2 * qk_w) // v_w)),
            pl.BlockSpec((ts, v_w), lambda b, s: (tok(b, s), (gdn_cols + 2 * qk_w) // v_w + 1)),
            pl.BlockSpec((ts, LANES), lambda b, s: (tok(b, s), 0)),
            pl.BlockSpec((LANES, qk_w), lambda b, s: (0, 0)),
            pl.BlockSpec((1, qk_w), lambda b, s: (0, 0)),
            pl.BlockSpec((1, GLA_DV), lambda b, s: (0, 0)),
        ],
        out_specs=pl.BlockSpec((ts, v_w), lambda b, s: (tok(b, s), 0)),
        out_shape=jax.ShapeDtypeStruct((t, v_w), BF16),
        scratch_shapes=[
            pltpu.VMEM((GLA_HEADS // 2, 2 * GLA_DV, 2 * GLA_DK), F32),
            pltpu.VMEM((ts, qk_w), F32),
        ],
        compiler_params=_params("arbitrary", "arbitrary"),
        name="gla",
    )(p, p, p, p, small, wa_pad, b_a2.reshape(1, qk_w).astype(F32), norm_w.reshape(1, GLA_DV))


def _gelu_tanh(x):
    return 0.5 * x * (1.0 + jnp.tanh(0.7978845608028654 * (x + 0.044715 * x * x * x)))


def _lru_kernel(x_ref, g_ref, cw_ref, cb_ref, wra_ref, bra_ref, wri_ref, bri_ref, lam_ref, o_ref,
                tail_scr, h_scr, a_scr, u_scr, *, ts):
    width = x_ref.shape[1]
    blk = width // LRU_BLOCKS

    @pl.when(pl.program_id(1) == 0)
    def _():
        tail_scr[...] = jnp.zeros_like(tail_scr)
        h_scr[...] = jnp.zeros_like(h_scr)

    for gi in range(LRU_BLOCKS):
        lo, hi = gi * blk, (gi + 1) * blk
        cur = x_ref[:, lo:hi].astype(F32)
        full = jnp.concatenate([tail_scr[:, lo:hi], cur], axis=0)
        w = cw_ref[:, lo:hi]
        acc = full * w[CONV_W - 1:CONV_W]
        for i in range(CONV_W - 1):
            acc = acc + pltpu.roll(full, CONV_W - 1 - i, axis=0) * w[i:i + 1]
        xc = acc[SUBLANES:] + cb_ref[:, lo:hi]
        tail_scr[:, lo:hi] = cur[ts - SUBLANES:]
        xb = xc.astype(BF16)
        r = _sigmoid(_dot(xb, wra_ref[gi]) + bra_ref[:, lo:hi])
        ig = _sigmoid(_dot(xb, wri_ref[gi]) + bri_ref[:, lo:hi])
        log_a = -LRU_C * r * _softplus(-lam_ref[:, lo:hi])
        a = jnp.exp(log_a)
        a_scr[:, lo:hi] = a
        u_scr[:, lo:hi] = jnp.sqrt(1.0 - a * a) * (ig * xc)

    row = lax.broadcasted_iota(jnp.int32, (SUBLANES, width), 0)

    def scan_body(rb, h_prev):
        r0 = pl.multiple_of(rb * SUBLANES, SUBLANES)
        a = a_scr[pl.ds(r0, SUBLANES), :]
        u = u_scr[pl.ds(r0, SUBLANES), :]
        for d in (1, 2, 4):
            a_sh = jnp.where(row >= d, pltpu.roll(a, d, axis=0), 1.0)
            u_sh = jnp.where(row >= d, pltpu.roll(u, d, axis=0), 0.0)
            u = a * u_sh + u
            a = a * a_sh
        h = u + a * h_prev
        gate = g_ref[pl.ds(r0, SUBLANES), :].astype(F32)
        u_scr[pl.ds(r0, SUBLANES), :] = _gelu_tanh(gate) * h
        return h[SUBLANES - 1:SUBLANES, :]

    h_last = lax.fori_loop(0, ts // SUBLANES, scan_body, h_scr[...])
    h_scr[...] = h_last
    o_ref[...] = u_scr[...].astype(o_ref.dtype)


def _lru(p, conv_w, conv_b, w_ra, b_ra, w_ri, b_ri, lam, bn, seq):
    t = p.shape[0]
    width = p.shape[1] // 2
    blk = width // LRU_BLOCKS
    ts = 256
    per_b = seq // ts
    tok = lambda b, s: b * per_b + s
    vec = lambda a: a.reshape(1, width).astype(F32)
    vspec = pl.BlockSpec((1, width), lambda b, s: (0, 0))
    wspec = pl.BlockSpec((LRU_BLOCKS, blk, blk), lambda b, s: (0, 0, 0))
    return pl.pallas_call(
        functools.partial(_lru_kernel, ts=ts),
        grid=(bn, per_b),
        in_specs=[
            pl.BlockSpec((ts, width), lambda b, s: (tok(b, s), 0)),
            pl.BlockSpec((ts, width), lambda b, s: (tok(b, s), 1)),
            pl.BlockSpec((CONV_W, width), lambda b, s: (0, 0)),
            vspec, wspec, vspec, wspec, vspec, vspec,
        ],
        out_specs=pl.BlockSpec((ts, width), lambda b, s: (tok(b, s), 0)),
        out_shape=jax.ShapeDtypeStruct((t, width), BF16),
        scratch_shapes=[
            pltpu.VMEM((SUBLANES, width), F32),
            pltpu.VMEM((1, width), F32),
            pltpu.VMEM((ts, width), F32),
            pltpu.VMEM((ts, width), F32),
        ],
        compiler_params=_params("arbitrary", "arbitrary"),
        name="lru",
    )(p, p, conv_w, vec(conv_b), w_ra.astype(BF16), vec(b_ra), w_ri.astype(BF16), vec(b_ri), vec(lam))


def _pack_bf16_pairs(h):
    half = h.shape[1] // 2
    lo = lax.bitcast_convert_type(h[:, :half].astype(BF16).astype(F32), jnp.uint32)
    hi = lax.bitcast_convert_type(h[:, half:].astype(BF16).astype(F32), jnp.uint32)
    return (lo >> 16) | (hi & jnp.uint32(0xFFFF0000))


def _unpack_bf16_pairs(w):
    lo = lax.bitcast_convert_type(w << 16, F32).astype(BF16)
    hi = lax.bitcast_convert_type(w & jnp.uint32(0xFFFF0000), F32).astype(BF16)
    return lo, hi


def _out_proj_kernel(*refs, n_in):
    a_refs = refs[:n_in]
    w_refs = refs[n_in:2 * n_in]
    (x_ref, mod_ref, ln_ref, rwh_ref, rwl_ref, rb_ref,
     xo_ref, hp_ref, ti_ref, tw_ref, tr_ref, cnt_ref, cnt_scr) = refs[2 * n_in:]

    @pl.when(pl.program_id(0) == 0)
    def _():
        cnt_scr[...] = jnp.zeros_like(cnt_scr)

    y = _dot(a_refs[0][...], w_refs[0][...])
    for a_ref, w_ref in zip(a_refs[1:], w_refs[1:]):
        y = y + _dot(a_ref[...], w_ref[...])
    m = mod_ref[0]
    xn = x_ref[...] + m[2:3] * y
    xo_ref[...] = xn
    h = _norm_mod(xn, ln_ref[...], m[3:4], m[4:5])
    hp_ref[...] = _pack_bf16_pairs(h).reshape(hp_ref.shape)
    h_hi = h.astype(BF16)
    h_lo = (h - h_hi.astype(F32)).astype(BF16)
    logits = _dot(h_hi, rwh_ref[...]) + (_dot(h_hi, rwl_ref[...]) + _dot(h_lo, rwh_ref[...])) + rb_ref[...]
    lane = lax.broadcasted_iota(jnp.int32, logits.shape, 1)
    cur = jnp.where(lane < N_EXPERTS, logits, -jnp.inf)
    vals, idxs = [], []
    for _ in range(TOP_K):
        mx = jnp.max(cur, axis=-1, keepdims=True)
        ix = jnp.min(jnp.where(cur == mx, lane, LANES), axis=-1, keepdims=True)
        vals.append(mx)
        idxs.append(ix)
        cur = jnp.where(lane == ix, -jnp.inf, cur)
    es = [jnp.exp(v - vals[0]) for v in vals]
    den = es[0]
    for e in es[1:]:
        den = den + e
    tm = logits.shape[0]
    ri = lax.broadcasted_iota(jnp.int32, (tm, tm), 0)
    rj = lax.broadcasted_iota(jnp.int32, (tm, tm), 1)
    tri = jnp.where(rj < ri, 1.0, 0.0).astype(BF16)
    base = cnt_scr[...]
    ti = jnp.zeros(logits.shape, jnp.int32)
    tw = jnp.zeros(logits.shape, F32)
    tr = jnp.zeros(logits.shape, jnp.int32)
    for kk in range(TOP_K):
        sel = lane == idxs[kk]
        onehot = jnp.where(sel, 1.0, 0.0)
        before = _dot(tri, onehot.astype(BF16)) + base
        rank = jnp.sum(jnp.where(sel, before, 0.0), axis=-1, keepdims=True)
        base = base + jnp.sum(onehot, axis=0, keepdims=True)
        ti = jnp.where(lane == kk, idxs[kk], ti)
        tw = jnp.where(lane == kk, es[kk] / den, tw)
        tr = jnp.where(lane == kk, rank.astype(jnp.int32), tr)
    cnt_scr[...] = base
    ti_ref[...] = ti
    tw_ref[...] = tw
    tr_ref[...] = tr
    cnt_ref[...] = base.astype(jnp.int32)


def _out_proj(acts, weights, x, mod, ln_ffn, router_w, router_b, seq):
    t, d = x.shape
    tm = 256
    per_b = seq // tm
    n_in = len(acts)
    rw = jnp.zeros((d, LANES), F32).at[:, :N_EXPERTS].set(router_w.astype(F32))
    rw_hi = rw.astype(BF16)
    rw_lo = (rw - rw_hi.astype(F32)).astype(BF16)
    rb = jnp.zeros((1, LANES), F32).at[0, :N_EXPERTS].set(router_b.astype(F32))
    tiles = d // 2 // LANES
    in_specs = [pl.BlockSpec((tm, a.shape[1]), lambda i: (i, 0)) for a in acts]
    in_specs += [pl.BlockSpec(w.shape, lambda i: (0, 0)) for w in weights]
    in_specs += [
        pl.BlockSpec((tm, d), lambda i: (i, 0)),
        pl.BlockSpec((1, 6, d), lambda i: (i // per_b, 0, 0)),
        pl.BlockSpec((1, d), lambda i: (0, 0)),
        pl.BlockSpec((d, LANES), lambda i: (0, 0)),
        pl.BlockSpec((d, LANES), lambda i: (0, 0)),
        pl.BlockSpec((1, LANES), lambda i: (0, 0)),
    ]
    row_d = pl.BlockSpec((tm, d), lambda i: (i, 0))
    row_h = pl.BlockSpec((tm, tiles, LANES), lambda i: (i, 0, 0))
    row_l = pl.BlockSpec((tm, LANES), lambda i: (i, 0))
    return pl.pallas_call(
        functools.partial(_out_proj_kernel, n_in=n_in),
        grid=(t // tm,),
        in_specs=in_specs,
        out_specs=[row_d, row_h, row_l, row_l, row_l, pl.BlockSpec((1, LANES), lambda i: (0, 0))],
        out_shape=[
            jax.ShapeDtypeStruct((t, d), F32),
            jax.ShapeDtypeStruct((t, tiles, LANES), jnp.uint32),
            jax.ShapeDtypeStruct((t, LANES), jnp.int32),
            jax.ShapeDtypeStruct((t, LANES), F32),
            jax.ShapeDtypeStruct((t, LANES), jnp.int32),
            jax.ShapeDtypeStruct((1, LANES), jnp.int32),
        ],
        scratch_shapes=[pltpu.VMEM((1, LANES), F32)],
        compiler_params=_params("arbitrary"),
        name="out_proj",
    )(*acts, *weights, x, mod, ln_ffn.reshape(1, d), rw_hi, rw_lo, rb)


def _moe_routing(ti, tr, cnt, n_tok):
    n_blk = (n_tok * TOP_K) // MOE_BLOCK + N_EXPERTS
    counts = cnt[0, :N_EXPERTS]
    padded = ((counts + MOE_BLOCK - 1) // MOE_BLOCK) * MOE_BLOCK
    pend = jnp.cumsum(padded)
    pstart = pend - padded
    e = ti[:, :TOP_K]
    hit = e[:, :, None] == jnp.arange(N_EXPERTS, dtype=jnp.int32)
    dest = tr[:, :TOP_K] + jnp.sum(jnp.where(hit, pstart, 0), axis=-1)
    blk_start = jnp.arange(n_blk, dtype=jnp.int32) * MOE_BLOCK
    blk_e = jnp.minimum(jnp.sum(pend[None, :] <= blk_start[:, None], axis=1), N_EXPERTS - 1).astype(jnp.int32)
    i = jnp.arange(MOE_BLOCK, dtype=jnp.int32)
    tail = (pstart + counts)[:, None] + i[None, :]
    is_tail = i[None, :] < (padded - counts)[:, None]
    spare = (jnp.cumsum(jnp.logical_not(is_tail).reshape(-1).astype(jnp.int32)) - 1).reshape(is_tail.shape)
    pad_slots = jnp.where(is_tail, tail, pend[-1] + spare).reshape(-1)
    n_used = (pend[-1:] // MOE_BLOCK).astype(jnp.int32)
    return dest.astype(jnp.int32), pad_slots.astype(jnp.int32), blk_e, n_used


DISPATCH_TOKENS = 512
COMBINE_TOKENS = 128
ISSUE_TOKENS = 4
ISSUE_ROWS = ISSUE_TOKENS * TOP_K


def _dispatch_kernel(dest_ref, pad_ref, hp_ref, xs_hbm, zero_scr, sem):
    n_pad = pad_ref.shape[2]
    zero_scr[...] = jnp.zeros_like(zero_scr)

    def issue(g, c):
        t0 = g * ISSUE_TOKENS
        dsts = [dest_ref[0, 0, t0 * TOP_K + j] for j in range(ISSUE_ROWS)]
        for j, dst in enumerate(dsts):
            pltpu.make_async_copy(hp_ref.at[t0 + j // TOP_K], xs_hbm.at[dst], sem.at[0]).start(priority=j % 2)
        return c

    lax.fori_loop(0, DISPATCH_TOKENS // ISSUE_TOKENS, issue, 0)

    def issue_zero(g, c):
        dsts = [pad_ref[0, 0, g * ISSUE_ROWS + j] for j in range(ISSUE_ROWS)]
        for j, dst in enumerate(dsts):
            pltpu.make_async_copy(zero_scr, xs_hbm.at[dst], sem.at[0]).start(priority=j % 2)
        return c

    lax.fori_loop(0, n_pad // ISSUE_ROWS, issue_zero, 0)

    rows = DISPATCH_TOKENS * TOP_K + n_pad
    for _ in range(rows // DISPATCH_TOKENS):
        pltpu.make_async_copy(hp_ref, xs_hbm.at[pl.ds(0, DISPATCH_TOKENS)], sem.at[0]).wait()
    rem = rows % DISPATCH_TOKENS
    if rem:
        pltpu.make_async_copy(hp_ref.at[pl.ds(0, rem)], xs_hbm.at[pl.ds(0, rem)], sem.at[0]).wait()


def _dispatch(hp, dest, pad_slots, n_buf):
    t = hp.shape[0]
    n_steps = t // DISPATCH_TOKENS
    n_pad = pad_slots.shape[0] // n_steps
    assert n_pad * n_steps == pad_slots.shape[0] and n_pad % ISSUE_ROWS == 0
    smem = lambda n: pl.BlockSpec((1, 1, n), lambda i: (i, 0, 0), memory_space=pltpu.SMEM)
    return pl.pallas_call(
        _dispatch_kernel,
        grid=(n_steps,),
        in_specs=[
            smem(DISPATCH_TOKENS * TOP_K),
            smem(n_pad),
            pl.BlockSpec((DISPATCH_TOKENS,) + hp.shape[1:], lambda i: (i, 0, 0)),
        ],
        out_specs=pl.BlockSpec(memory_space=pl.ANY),
        out_shape=jax.ShapeDtypeStruct((n_buf,) + hp.shape[1:], jnp.uint32),
        scratch_shapes=[pltpu.VMEM(hp.shape[1:], jnp.uint32), pltpu.SemaphoreType.DMA((1,))],
        compiler_params=_params("arbitrary"),
        name="moe_dispatch",
    )(dest.reshape(n_steps, 1, DISPATCH_TOKENS * TOP_K), pad_slots.reshape(n_steps, 1, n_pad), hp)


def _moe_kernel(blk_e_ref, n_used_ref, xs_ref, wgu_ref, bgu_ref, wdn_ref, bdn_ref, y_ref):
    del blk_e_ref
    half = xs_ref.shape[1] * xs_ref.shape[2]
    d_ff = wdn_ref.shape[1]
    used = pl.program_id(0) < n_used_ref[0]

    @pl.when(used)
    def _():
        lo, hi = _unpack_bf16_pairs(xs_ref[...].reshape(MOE_BLOCK, half))
        gu = _dot(lo, wgu_ref[0, :half, :]) + _dot(hi, wgu_ref[0, half:, :]) + bgu_ref[0]
        gate = jnp.minimum(gu[:, :d_ff], SWIGLU_LIMIT)
        up = jnp.clip(gu[:, d_ff:], -SWIGLU_LIMIT, SWIGLU_LIMIT)
        act = (up + 1.0) * gate * _sigmoid(SWIGLU_ALPHA * gate)
        y = _dot(act.astype(BF16), wdn_ref[0]) + bdn_ref[0]
        y_ref[...] = y.reshape(y_ref.shape)

    @pl.when(jnp.logical_not(used))
    def _():
        y_ref[...] = jnp.zeros_like(y_ref)


def _moe_experts(xs, blk_e, n_used, w_gu, b_gu, w_dn, b_dn):
    n_buf, tiles, _ = xs.shape
    d = 2 * tiles * LANES
    d_ff = w_dn.shape[1]
    n_w = w_gu.shape[0]
    grid_spec = pltpu.PrefetchScalarGridSpec(
        num_scalar_prefetch=2,
        grid=(n_buf // MOE_BLOCK,),
        in_specs=[
            pl.BlockSpec((MOE_BLOCK, tiles, LANES), lambda b, e, u: (b, 0, 0)),
            pl.BlockSpec((1, d, 2 * d_ff), lambda b, e, u: (e[b], 0, 0)),
            pl.BlockSpec((1, 1, 2 * d_ff), lambda b, e, u: (e[b], 0, 0)),
            pl.BlockSpec((1, d_ff, d), lambda b, e, u: (e[b], 0, 0)),
            pl.BlockSpec((1, 1, d), lambda b, e, u: (e[b], 0, 0)),
        ],
        out_specs=pl.BlockSpec((MOE_BLOCK, d // LANES, LANES), lambda b, e, u: (b, 0, 0)),
    )
    return pl.pallas_call(
        _moe_kernel,
        grid_spec=grid_spec,
        out_shape=jax.ShapeDtypeStruct((n_buf, d // LANES, LANES), F32),
        compiler_params=_params("arbitrary"),
        name="moe_experts",
    )(blk_e, n_used, xs, w_gu, b_gu.reshape(n_w, 1, 2 * d_ff), w_dn, b_dn.reshape(n_w, 1, d))


def _combine_kernel(dest_ref, dest_next_ref, x_ref, tw_ref, mod_ref, lnf_ref, y_hbm, o_ref, ybuf, sem, *, final):
    i = pl.program_id(0)
    n = pl.num_programs(0)
    slot = i % 2
    tm = COMBINE_TOKENS

    def issue(ids_ref, s):
        def body(g, c):
            t0 = g * ISSUE_TOKENS
            srcs = [ids_ref[0, 0, t0 * TOP_K + j] for j in range(ISSUE_TOKENS * TOP_K)]
            for j, src in enumerate(srcs):
                pltpu.make_async_copy(y_hbm.at[src], ybuf.at[s, (j % TOP_K) * tm + t0 + j // TOP_K],
                                      sem.at[s]).start(priority=j % 2)
            return c
        lax.fori_loop(0, tm // ISSUE_TOKENS, body, 0)

    @pl.when(i == 0)
    def _():
        issue(dest_ref, 0)

    @pl.when(i + 1 < n)
    def _():
        issue(dest_next_ref, 1 - slot)

    pltpu.make_async_copy(y_hbm.at[pl.ds(0, TOP_K * tm)], ybuf.at[slot], sem.at[slot]).wait()
    tw = tw_ref[...]
    acc = None
    for kk in range(TOP_K):
        gate = jnp.broadcast_to(tw[:, kk:kk + 1], (tm, LANES)).reshape(tm, 1, LANES)
        term = gate * ybuf[slot, kk * tm:(kk + 1) * tm]
        acc = term if acc is None else acc + term
    xn = x_ref[...] + mod_ref[0][5:6] * acc.reshape(x_ref.shape)
    if final:
        xn = xn * lax.rsqrt(jnp.mean(xn * xn, axis=-1, keepdims=True) + EPS) * lnf_ref[...]
    o_ref[...] = xn


def _combine(x, y, dest, tw, mod, ln_final, seq, final):
    t, d = x.shape
    tm = COMBINE_TOKENS
    per_b = seq // tm
    nt = t // tm
    ids = dest.reshape(nt, 1, tm * TOP_K)
    id_spec = lambda f: pl.BlockSpec((1, 1, tm * TOP_K), f, memory_space=pltpu.SMEM)
    return pl.pallas_call(
        functools.partial(_combine_kernel, final=final),
        grid=(nt,),
        in_specs=[
            id_spec(lambda i: (i, 0, 0)),
            id_spec(lambda i: (jnp.minimum(i + 1, nt - 1), 0, 0)),
            pl.BlockSpec((tm, d), lambda i: (i, 0)),
            pl.BlockSpec((tm, LANES), lambda i: (i, 0)),
            pl.BlockSpec((1, 6, d), lambda i: (i // per_b, 0, 0)),
            pl.BlockSpec((1, d), lambda i: (0, 0)),
            pl.BlockSpec(memory_space=pl.ANY),
        ],
        out_specs=pl.BlockSpec((tm, d), lambda i: (i, 0)),
        out_shape=jax.ShapeDtypeStruct((t, d), F32),
        scratch_shapes=[pltpu.VMEM((2, TOP_K * tm, d // LANES, LANES), F32), pltpu.SemaphoreType.DMA((2,))],
        compiler_params=_params("arbitrary"),
        name="moe_combine",
    )(ids, ids, x, tw, mod, ln_final.reshape(1, d), y)


def _ffn(x, hp, ti, tw, tr, cnt, mod, layer, w_gu, b_gu, w_dn, b_dn, ln_final, seq, final):
    t = x.shape[0]
    dest, pad_slots, blk_e, n_used = _moe_routing(ti, tr, cnt, t)
    n_buf = blk_e.shape[0] * MOE_BLOCK
    xs = _dispatch(hp, dest, pad_slots, n_buf)
    y = _moe_experts(xs, blk_e + layer * N_EXPERTS, n_used, w_gu, b_gu, w_dn, b_dn)
    return _combine(x, y, dest, tw, mod, ln_final, seq, final)


def _even_inproj_weights(w_in):
    gdn_cols = 4 * GDN_HEADS * GDN_DK
    small0 = gdn_cols
    gla0 = small0 + 2 * GDN_HEADS
    gla_cols = 2 * GLA_HEADS * GLA_DK + 2 * GLA_HEADS * GLA_DV
    lg0 = gla0 + gla_cols
    w_big = jnp.concatenate([w_in[:, :gdn_cols], w_in[:, gla0:lg0]], axis=1).astype(BF16)
    w_small = jnp.zeros((w_in.shape[0], LANES), F32)
    w_small = w_small.at[:, :2 * GDN_HEADS].set(w_in[:, small0:gla0])
    w_small = w_small.at[:, 2 * GDN_HEADS:2 * GDN_HEADS + GLA_RANK].set(w_in[:, lg0:lg0 + GLA_RANK])
    return w_big, w_small.astype(BF16)


def _even_mixer(x, mod, ln_mix, w_in, conv_w, a_log, dt_bias, gdn_norm, w_a2, b_a2, gla_norm, bn, seq):
    w_big, w_small = _even_inproj_weights(w_in)
    p, small = _norm_proj(x, ln_mix, mod, w_big, w_small, seq)
    col, row = _gdn_gates(small, a_log, dt_bias)
    o_a = _gdn(p, col, row, conv_w, gdn_norm, bn, seq)
    o_b = _gla(p, small, w_a2, b_a2, gla_norm, bn, seq)
    return o_a, o_b


def kernel(x, c, ln_mix, ln_ffn, ln_final, w_ada, b_ada, e_w_in, e_conv_w, e_a_log, e_dt_bias, e_gdn_norm, e_gla_w_a2, e_gla_b_a2, e_gla_norm, e_w_out, o_w_in, o_conv_w, o_conv_b, o_w_ra, o_b_ra, o_w_ri, o_b_ri, o_lam, o_w_out, router_w, router_b, exp_w_gu, exp_b_gu, exp_w_dn, exp_b_dn):
    bn, seq, d = x.shape
    depth = w_ada.shape[0]
    mods = _ada_mod(c, w_ada, b_ada)
    xf = x.reshape(bn * seq, d)
    w_gu = exp_w_gu.astype(BF16).reshape((-1,) + exp_w_gu.shape[2:])
    w_dn = exp_w_dn.astype(BF16).reshape((-1,) + exp_w_dn.shape[2:])
    b_gu = exp_b_gu.reshape((-1,) + exp_b_gu.shape[2:])
    b_dn = exp_b_dn.reshape((-1,) + exp_b_dn.shape[2:])
    for layer in range(depth):
        mod = mods[layer]
        j = layer // 2
        if layer % 2 == 0:
            o_a, o_b = _even_mixer(xf, mod, ln_mix[layer], e_w_in[j], e_conv_w[j], e_a_log[j], e_dt_bias[j],
                                   e_gdn_norm[j], e_gla_w_a2[j], e_gla_b_a2[j], e_gla_norm[j], bn, seq)
            split = o_a.shape[1]
            acts = [o_a, o_b]
            weights = [e_w_out[j][:split].astype(BF16), e_w_out[j][split:].astype(BF16)]
        else:
            p = _norm_proj(xf, ln_mix[layer], mod, o_w_in[j].astype(BF16), None, seq)
            acts = [_lru(p, o_conv_w[j], o_conv_b[j], o_w_ra[j], o_b_ra[j], o_w_ri[j], o_b_ri[j], o_lam[j], bn, seq)]
            weights = [o_w_out[j].astype(BF16)]
        xf, hp, ti, tw, tr, cnt = _out_proj(acts, weights, xf, mod, ln_ffn[layer], router_w[layer], router_b[layer], seq)
        xf = _ffn(xf, hp, ti, tw, tr, cnt, mod, layer, w_gu, b_gu, w_dn, b_dn,
                  ln_final, seq, final=(layer == depth - 1))
    return xf.reshape(bn, seq, d)
```

```python
import functools

import jax
import jax.numpy as jnp
from jax import lax
from jax.experimental import pallas as pl
from jax.experimental.pallas import tpu as pltpu

F32 = jnp.float32
BF16 = jnp.bfloat16
HIGHEST = lax.Precision.HIGHEST

EPS = 1e-6
CONV_W = 4
CHUNK = 64
GDN_HEADS = 8
GDN_DK = 128
GDN_DV = 128
GLA_HEADS = 8
GLA_DK = 64
GLA_DV = 128
GLA_RANK = 16
GLA_GATE_NORM = 16.0
LRU_BLOCKS = 8
LRU_C = 8.0
N_EXPERTS = 32
TOP_K = 4
SWIGLU_LIMIT = 7.0
SWIGLU_ALPHA = 1.702
MOE_BLOCK = 256

LANES = 128
SUBLANES = 8
VMEM_LIMIT = 56 * 1024 * 1024


def _dot(a, b, precision=None):
    return jnp.dot(a, b, preferred_element_type=F32, precision=precision)


def _dot_nt(a, b, precision=None):
    return lax.dot_general(a, b, (((1,), (1,)), ((), ())), preferred_element_type=F32, precision=precision)


def _dot_tn(a, b):
    return lax.dot_general(a, b, (((0,), (0,)), ((), ())), preferred_element_type=F32)


def _dot_split(a, b):
    a_hi = a.astype(BF16)
    b_hi = b.astype(BF16)
    a_lo = (a - a_hi.astype(F32)).astype(BF16)
    b_lo = (b - b_hi.astype(F32)).astype(BF16)
    return _dot(a_hi, b_hi) + (_dot(a_hi, b_lo) + _dot(a_lo, b_hi))


def _sigmoid(x):
    return 1.0 / (1.0 + jnp.exp(-x))


def _softplus(x):
    return jnp.maximum(x, 0.0) + jnp.log(1.0 + jnp.exp(-jnp.abs(x)))


def _silu(x):
    return x * _sigmoid(x)


def _params(*sem):
    return pltpu.CompilerParams(dimension_semantics=sem, vmem_limit_bytes=VMEM_LIMIT)


def _ada_kernel(c_ref, w_ref, b_ref, o_ref):
    c = c_ref[...]
    o_ref[0] = _dot(_silu(c), w_ref[0], precision=HIGHEST) + b_ref[0]


def _ada_mod(c, w_ada, b_ada):
    depth, d, n6 = w_ada.shape
    bn = c.shape[0]
    rows = -(-bn // SUBLANES) * SUBLANES
    cp = jnp.zeros((rows, d), F32).at[:bn].set(c)
    tn = 768
    out = pl.pallas_call(
        _ada_kernel,
        grid=(depth, n6 // tn),
        in_specs=[
            pl.BlockSpec((rows, d), lambda l, j: (0, 0)),
            pl.BlockSpec((1, d, tn), lambda l, j: (l, 0, j)),
            pl.BlockSpec((1, 1, tn), lambda l, j: (l, 0, j)),
        ],
        out_specs=pl.BlockSpec((1, rows, tn), lambda l, j: (l, 0, j)),
        out_shape=jax.ShapeDtypeStruct((depth, rows, n6), F32),
        compiler_params=_params("arbitrary", "arbitrary"),
        name="ada_mod",
    )(cp, w_ada, b_ada.reshape(depth, 1, n6))
    return out[:, :bn].reshape(depth, bn, 6, d)


def _norm_mod(x, ln, shift, scale):
    y = x * lax.rsqrt(jnp.mean(x * x, axis=-1, keepdims=True) + EPS) * ln
    return y * (1.0 + scale) + shift


def _norm_proj_kernel(x_ref, ln_ref, mod_ref, w_ref, *rest, has_small):
    if has_small:
        ws_ref, o_ref, os_ref, h_scr = rest
    else:
        o_ref, h_scr = rest
    j = pl.program_id(1)

    @pl.when(j == 0)
    def _():
        m = mod_ref[0]
        h = _norm_mod(x_ref[...], ln_ref[...], m[0:1], m[1:2]).astype(BF16)
        h_scr[...] = h
        if has_small:
            os_ref[...] = _dot(h, ws_ref[...])

    o_ref[...] = _dot(h_scr[...], w_ref[...]).astype(o_ref.dtype)


def _norm_proj(x, ln, mod, w, w_small, seq):
    t, d = x.shape
    n = w.shape[1]
    tm = 512
    tn = max(c for c in range(LANES, 2048 + 1, LANES) if n % c == 0)
    per_b = seq // tm
    has_small = w_small is not None
    in_specs = [
        pl.BlockSpec((tm, d), lambda i, j: (i, 0)),
        pl.BlockSpec((1, d), lambda i, j: (0, 0)),
        pl.BlockSpec((1, 6, d), lambda i, j: (i // per_b, 0, 0)),
        pl.BlockSpec((d, tn), lambda i, j: (0, j)),
    ]
    out_specs = [pl.BlockSpec((tm, tn), lambda i, j: (i, j))]
    out_shape = [jax.ShapeDtypeStruct((t, n), BF16)]
    args = [x, ln.reshape(1, d), mod, w]
    if has_small:
        in_specs.append(pl.BlockSpec((d, LANES), lambda i, j: (0, 0)))
        out_specs.append(pl.BlockSpec((tm, LANES), lambda i, j: (i, 0)))
        out_shape.append(jax.ShapeDtypeStruct((t, LANES), F32))
        args.append(w_small)
    outs = pl.pallas_call(
        functools.partial(_norm_proj_kernel, has_small=has_small),
        grid=(t // tm, n // tn),
        in_specs=in_specs,
        out_specs=out_specs,
        out_shape=out_shape,
        scratch_shapes=[pltpu.VMEM((tm, d), BF16)],
        compiler_params=_params("arbitrary", "arbitrary"),
        name="norm_proj",
    )(*args)
    return outs if has_small else outs[0]


def _gdn_gate_kernel(ga_ref, gb_ref, alog_ref, dtb_ref, gc_ref, beta_ref, eg_ref, ek_ref):
    g = -jnp.exp(alog_ref[...]) * _softplus(ga_ref[...] + dtb_ref[...])
    ci = lax.broadcasted_iota(jnp.int32, (LANES, LANES), 0)
    cj = lax.broadcasted_iota(jnp.int32, (LANES, LANES), 1)
    same = (ci // CHUNK) == (cj // CHUNK)
    upper = jnp.where(same & (ci <= cj), 1.0, 0.0).astype(F32)
    allc = jnp.where(same, 1.0, 0.0).astype(F32)
    gc = _dot(g, upper, precision=HIGHEST)
    g_last = _dot(g, allc, precision=HIGHEST)
    gc_ref[...] = gc
    beta_ref[...] = _sigmoid(gb_ref[...])
    eg_ref[...] = jnp.exp(gc)
    ek_ref[...] = jnp.exp(g_last - gc)


def _gdn_gates(small, a_log, dt_bias):
    t = small.shape[0]
    nb = t // LANES
    rows = GDN_HEADS * nb
    gb_t = small[:, 0:GDN_HEADS].T.reshape(rows, LANES)
    ga_t = small[:, GDN_HEADS:2 * GDN_HEADS].T.reshape(rows, LANES)
    alog_c = jnp.repeat(a_log.astype(F32), nb).reshape(rows, 1)
    dtb_c = jnp.repeat(dt_bias.astype(F32), nb).reshape(rows, 1)
    full = pl.BlockSpec((rows, LANES), lambda: (0, 0))
    col1 = pl.BlockSpec((rows, 1), lambda: (0, 0))
    gc, beta, eg, ek = pl.pallas_call(
        _gdn_gate_kernel,
        in_specs=[full, full, col1, col1],
        out_specs=[full] * 4,
        out_shape=[jax.ShapeDtypeStruct((rows, LANES), F32)] * 4,
        compiler_params=pltpu.CompilerParams(vmem_limit_bytes=VMEM_LIMIT),
        name="gdn_gates",
    )(ga_t, gb_t, alog_c, dtb_c)
    to_col = lambda a: a.reshape(GDN_HEADS, t).T
    col = jnp.concatenate([to_col(beta), to_col(gc), to_col(eg), to_col(ek)], axis=1)
    row = gc.reshape(GDN_HEADS, t // CHUNK, CHUNK).transpose(1, 0, 2)
    return col, row


def _gdn_kernel(q_ref, k_ref, v_ref, z_ref, col_ref, row_ref, cw_ref, nw_ref, o_ref,
                s_scr, tail_scr, qkv_scr, *, ts):
    width = GDN_HEADS * GDN_DK

    @pl.when(pl.program_id(1) == 0)
    def _():
        s_scr[...] = jnp.zeros_like(s_scr)
        tail_scr[...] = jnp.zeros_like(tail_scr)

    strip = 256
    for part, ref in enumerate((q_ref, k_ref, v_ref)):
        for s0 in range(0, width, strip):
            c0 = part * width + s0
            cur = ref[:, s0:s0 + strip].astype(F32)
            full = jnp.concatenate([tail_scr[:, c0:c0 + strip], cur], axis=0)
            w = cw_ref[:, c0:c0 + strip]
            acc = full * w[CONV_W - 1:CONV_W]
            for i in range(CONV_W - 1):
                acc = acc + pltpu.roll(full, CONV_W - 1 - i, axis=0) * w[i:i + 1]
            y = acc[SUBLANES:]
            qkv_scr[:, c0:c0 + strip] = _silu(y)
            tail_scr[:, c0:c0 + strip] = cur[ts - SUBLANES:]

    ri = lax.broadcasted_iota(jnp.int32, (CHUNK, CHUNK), 0)
    rj = lax.broadcasted_iota(jnp.int32, (CHUNK, CHUNK), 1)
    incl = ri >= rj
    strict = ri > rj
    eye = jnp.where(ri == rj, 1.0, 0.0).astype(F32)
    nw = nw_ref[...]

    def chunk_body(c, carry):
        r0 = pl.multiple_of(c * CHUNK, CHUNK)
        colc = col_ref[pl.ds(r0, CHUNK), :]
        rowc = row_ref[c]
        heads = range(GDN_HEADS)
        sl = [slice(h * GDN_DK, (h + 1) * GDN_DK) for h in heads]
        q = [qkv_scr[pl.ds(r0, CHUNK), sl[h]] for h in heads]
        k = [qkv_scr[pl.ds(r0, CHUNK), width + h * GDN_DK:width + (h + 1) * GDN_DK] for h in heads]
        v = [qkv_scr[pl.ds(r0, CHUNK), 2 * width + h * GDN_DK:2 * width + (h + 1) * GDN_DK] for h in heads]
        q = [a * lax.rsqrt(jnp.sum(a * a, axis=-1, keepdims=True) + EPS) * (GDN_DK ** -0.5) for a in q]
        k = [a * lax.rsqrt(jnp.sum(a * a, axis=-1, keepdims=True) + EPS) for a in k]
        beta = [colc[:, h:h + 1] for h in heads]
        gc = [colc[:, 8 + h:9 + h] for h in heads]
        eg = [colc[:, 16 + h:17 + h] for h in heads]
        ek = [colc[:, 24 + h:25 + h] for h in heads]
        decay = [jnp.exp(jnp.where(incl, gc[h] - rowc[h:h + 1, :], -jnp.inf)) for h in heads]
        kb = [a.astype(BF16) for a in k]
        qb = [a.astype(BF16) for a in q]
        kk = [_dot_nt(kb[h], kb[h]) for h in heads]
        qk = [_dot_nt(qb[h], kb[h]) for h in heads]
        npow = [jnp.where(strict, -(beta[h] * kk[h] * decay[h]), 0.0) for h in heads]
        ainv = [eye + npow[h] for h in heads]
        for _ in range(5):
            nb = [a.astype(BF16) for a in npow]
            npow = [_dot(nb[h], nb[h]) for h in heads]
            nb = [a.astype(BF16) for a in npow]
            ainv = [ainv[h] + _dot(ainv[h].astype(BF16), nb[h]) for h in heads]
        ab = [a.astype(BF16) for a in ainv]
        u = [_dot(ab[h], (v[h] * beta[h]).astype(BF16)) for h in heads]
        w = [_dot(ab[h], (k[h] * (beta[h] * eg[h])).astype(BF16)) for h in heads]
        qk = [jnp.where(incl, qk[h] * decay[h], 0.0).astype(BF16) for h in heads]
        q_s = [(q[h] * eg[h]).astype(BF16) for h in heads]
        k_s = [(k[h] * ek[h]).astype(BF16) for h in heads]
        state = [s_scr[h] for h in heads]
        sb = [a.astype(BF16) for a in state]
        v_new = [(u[h] - _dot(w[h].astype(BF16), sb[h])).astype(BF16) for h in heads]
        o = [_dot(q_s[h], sb[h]) + _dot(qk[h], v_new[h]) for h in heads]
        for h in heads:
            s_scr[h] = state[h] * eg[h][CHUNK - 1:CHUNK, :] + _dot_tn(k_s[h], v_new[h])
        for h in heads:
            oh = o[h] * lax.rsqrt(jnp.mean(o[h] * o[h], axis=-1, keepdims=True) + EPS) * nw
            z = z_ref[pl.ds(r0, CHUNK), sl[h]].astype(F32)
            o_ref[pl.ds(r0, CHUNK), sl[h]] = (oh * _silu(z)).astype(o_ref.dtype)
        return carry

    lax.fori_loop(0, ts // CHUNK, chunk_body, 0)


def _gdn(p, col, row, conv_w, norm_w, bn, seq):
    t = p.shape[0]
    ts = 256
    per_b = seq // ts
    width = GDN_HEADS * GDN_DK
    tok = lambda b, s: b * per_b + s
    pspec = lambda cb: pl.BlockSpec((ts, width), lambda b, s: (tok(b, s), cb))
    return pl.pallas_call(
        functools.partial(_gdn_kernel, ts=ts),
        grid=(bn, per_b),
        in_specs=[
            pspec(0), pspec(1), pspec(2), pspec(3),
            pl.BlockSpec((ts, 4 * GDN_HEADS), lambda b, s: (tok(b, s), 0)),
            pl.BlockSpec((ts // CHUNK, GDN_HEADS, CHUNK), lambda b, s: (tok(b, s), 0, 0)),
            pl.BlockSpec((CONV_W, 3 * width), lambda b, s: (0, 0)),
            pl.BlockSpec((1, GDN_DV), lambda b, s: (0, 0)),
        ],
        out_specs=pl.BlockSpec((ts, width), lambda b, s: (tok(b, s), 0)),
        out_shape=jax.ShapeDtypeStruct((t, width), BF16),
        scratch_shapes=[
            pltpu.VMEM((GDN_HEADS, GDN_DK, GDN_DV), F32),
            pltpu.VMEM((SUBLANES, 3 * width), F32),
            pltpu.VMEM((ts, 3 * width), F32),
        ],
        compiler_params=_params("arbitrary", "arbitrary"),
        name="gdn",
    )(p, p, p, p, col, row, conv_w, norm_w.reshape(1, GDN_DV))


def _gla_kernel(q_ref, k_ref, v_ref, r_ref, sm_ref, wa_ref, ba_ref, nw_ref, o_ref, st_scr, g_scr, *, ts):
    pair_w = 2 * GLA_DK
    n_pairs = GLA_HEADS // 2

    @pl.when(pl.program_id(1) == 0)
    def _():
        st_scr[...] = jnp.zeros_like(st_scr)

    logits = _dot(sm_ref[...], wa_ref[...], precision=HIGHEST) + ba_ref[...]
    g_scr[...] = -_softplus(-logits) / GLA_GATE_NORM

    ri = lax.broadcasted_iota(jnp.int32, (CHUNK, CHUNK), 0)
    rj = lax.broadcasted_iota(jnp.int32, (CHUNK, CHUNK), 1)
    incl = ri >= rj
    lower = jnp.where(incl, 1.0, 0.0).astype(BF16)
    lane = lax.broadcasted_iota(jnp.int32, (CHUNK, pair_w), 1)
    sr = lax.broadcasted_iota(jnp.int32, (2 * GLA_DV, pair_w), 0)
    sc = lax.broadcasted_iota(jnp.int32, (2 * GLA_DV, pair_w), 1)
    live = (sr // GLA_DV) == (sc // GLA_DK)
    nw = nw_ref[...]

    def chunk_body(c, carry):
        r0 = pl.multiple_of(c * CHUNK, CHUNK)
        pairs = range(n_pairs)
        heads = range(GLA_HEADS)
        psl = [slice(p * pair_w, (p + 1) * pair_w) for p in pairs]
        hsl = [slice(h * GLA_DV, (h + 1) * GLA_DV) for h in heads]
        g = g_scr[pl.ds(r0, CHUNK), :]
        g1 = g.astype(BF16)
        rem = g - g1.astype(F32)
        g2 = rem.astype(BF16)
        g3 = (rem - g2.astype(F32)).astype(BF16)
        b_all = _dot(lower, g1) + (_dot(lower, g2) + _dot(lower, g3))
        b = [b_all[:, psl[p]] for p in pairs]
        b_mid = [a[CHUNK // 2:CHUNK // 2 + 1, :] for a in b]
        b_last = [a[CHUNK - 1:CHUNK, :] for a in b]
        q = [q_ref[pl.ds(r0, CHUNK), psl[p]].astype(F32) * (GLA_DK ** -0.5) for p in pairs]
        k = [k_ref[pl.ds(r0, CHUNK), psl[p]].astype(F32) for p in pairs]
        q_i = [q[p] * jnp.exp(b[p] - b_mid[p]) for p in pairs]
        k_i = [(k[p] * jnp.exp(b_mid[p] - b[p])).astype(BF16) for p in pairs]
        q_s = [(q[p] * jnp.exp(b[p])).astype(BF16) for p in pairs]
        k_s = [(k[p] * jnp.exp(b_last[p] - b[p])).astype(BF16) for p in pairs]
        vp = [v_ref[pl.ds(r0, CHUNK), 2 * p * GLA_DV:(2 * p + 2) * GLA_DV] for p in pairs]
        st = [st_scr[p] for p in pairs]
        o_inter = [_dot_nt(q_s[p], st[p].astype(BF16)) for p in pairs]
        upd = [_dot_tn(vp[p], k_s[p]) for p in pairs]
        for p in pairs:
            st_scr[p] = st[p] * jnp.exp(b_last[p]) + jnp.where(live, upd[p], 0.0)
        q_h = [jnp.where((lane // GLA_DK) == (h % 2), q_i[h // 2], 0.0).astype(BF16) for h in heads]
        att = [jnp.where(incl, _dot_nt(q_h[h], k_i[h // 2]), 0.0).astype(BF16) for h in heads]
        o = [_dot(att[h], vp[h // 2][:, (h % 2) * GLA_DV:(h % 2 + 1) * GLA_DV])
             + o_inter[h // 2][:, (h % 2) * GLA_DV:(h % 2 + 1) * GLA_DV] for h in heads]
        for h in heads:
            oh = o[h] * lax.rsqrt(jnp.mean(o[h] * o[h], axis=-1, keepdims=True) + EPS) * nw
            rg = r_ref[pl.ds(r0, CHUNK), hsl[h]].astype(F32)
            o_ref[pl.ds(r0, CHUNK), hsl[h]] = (oh * _silu(rg)).astype(o_ref.dtype)
        return carry

    lax.fori_loop(0, ts // CHUNK, chunk_body, 0)


def _gla(p, small, w_a2, b_a2, norm_w, bn, seq):
    t = p.shape[0]
    ts = 256
    per_b = seq // ts
    qk_w = GLA_HEADS * GLA_DK
    v_w = GLA_HEADS * GLA_DV
    gdn_cols = 4 * GDN_HEADS * GDN_DK
    tok = lambda b, s: b * per_b + s
    wa_pad = jnp.zeros((LANES, qk_w), F32).at[2 * GDN_HEADS:2 * GDN_HEADS + GLA_RANK].set(w_a2.astype(F32))
    return pl.pallas_call(
        functools.partial(_gla_kernel, ts=ts),
        grid=(bn, per_b),
        in_specs=[
            pl.BlockSpec((ts, qk_w), lambda b, s: (tok(b, s), gdn_cols // qk_w)),
            pl.BlockSpec((ts, qk_w), lambda b, s: (tok(b, s), gdn_cols // qk_w + 1)),
            pl.BlockSpec((ts, v_w), lambda b, s: (tok(b, s), (gdn_cols + 2 * qk_w) // v_w)),
            pl.BlockSpec((ts, v_w), lambda b, s: (tok(b, s), (gdn_cols + 2 * qk_w) // v_w + 1)),
            pl.BlockSpec((ts, LANES), lambda b, s: (tok(b, s), 0)),
            pl.BlockSpec((LANES, qk_w), lambda b, s: (0, 0)),
            pl.BlockSpec((1, qk_w), lambda b, s: (0, 0)),
            pl.BlockSpec((1, GLA_DV), lambda b, s: (0, 0)),
        ],
        out_specs=pl.BlockSpec((ts, v_w), lambda b, s: (tok(b, s), 0)),
        out_shape=jax.ShapeDtypeStruct((t, v_w), BF16),
        scratch_shapes=[
            pltpu.VMEM((GLA_HEADS // 2, 2 * GLA_DV, 2 * GLA_DK), F32),
            pltpu.VMEM((ts, qk_w), F32),
        ],
        compiler_params=_params("arbitrary", "arbitrary"),
        name="gla",
    )(p, p, p, p, small, wa_pad, b_a2.reshape(1, qk_w).astype(F32), norm_w.reshape(1, GLA_DV))


def _gelu_tanh(x):
    return 0.5 * x * (1.0 + jnp.tanh(0.7978845608028654 * (x + 0.044715 * x * x * x)))


def _lru_kernel(x_ref, g_ref, cw_ref, cb_ref, wra_ref, bra_ref, wri_ref, bri_ref, lam_ref, o_ref,
                tail_scr, h_scr, a_scr, u_scr, *, ts):
    width = x_ref.shape[1]
    blk = width // LRU_BLOCKS

    @pl.when(pl.program_id(1) == 0)
    def _():
        tail_scr[...] = jnp.zeros_like(tail_scr)
        h_scr[...] = jnp.zeros_like(h_scr)

    for gi in range(LRU_BLOCKS):
        lo, hi = gi * blk, (gi + 1) * blk
        cur = x_ref[:, lo:hi].astype(F32)
        full = jnp.concatenate([tail_scr[:, lo:hi], cur], axis=0)
        w = cw_ref[:, lo:hi]
        acc = full * w[CONV_W - 1:CONV_W]
        for i in range(CONV_W - 1):
            acc = acc + pltpu.roll(full, CONV_W - 1 - i, axis=0) * w[i:i + 1]
        xc = acc[SUBLANES:] + cb_ref[:, lo:hi]
        tail_scr[:, lo:hi] = cur[ts - SUBLANES:]
        xb = xc.astype(BF16)
        r = _sigmoid(_dot(xb, wra_ref[gi]) + bra_ref[:, lo:hi])
        ig = _sigmoid(_dot(xb, wri_ref[gi]) + bri_ref[:, lo:hi])
        log_a = -LRU_C * r * _softplus(-lam_ref[:, lo:hi])
        a = jnp.exp(log_a)
        a_scr[:, lo:hi] = a
        u_scr[:, lo:hi] = jnp.sqrt(1.0 - a * a) * (ig * xc)

    row = lax.broadcasted_iota(jnp.int32, (SUBLANES, width), 0)

    def scan_body(rb, h_prev):
        r0 = pl.multiple_of(rb * SUBLANES, SUBLANES)
        a = a_scr[pl.ds(r0, SUBLANES), :]
        u = u_scr[pl.ds(r0, SUBLANES), :]
        for d in (1, 2, 4):
            a_sh = jnp.where(row >= d, pltpu.roll(a, d, axis=0), 1.0)
            u_sh = jnp.where(row >= d, pltpu.roll(u, d, axis=0), 0.0)
            u = a * u_sh + u
            a = a * a_sh
        h = u + a * h_prev
        gate = g_ref[pl.ds(r0, SUBLANES), :].astype(F32)
        u_scr[pl.ds(r0, SUBLANES), :] = _gelu_tanh(gate) * h
        return h[SUBLANES - 1:SUBLANES, :]

    h_last = lax.fori_loop(0, ts // SUBLANES, scan_body, h_scr[...])
    h_scr[...] = h_last
    o_ref[...] = u_scr[...].astype(o_ref.dtype)


def _lru(p, conv_w, conv_b, w_ra, b_ra, w_ri, b_ri, lam, bn, seq):
    t = p.shape[0]
    width = p.shape[1] // 2
    blk = width // LRU_BLOCKS
    ts = 256
    per_b = seq // ts
    tok = lambda b, s: b * per_b + s
    vec = lambda a: a.reshape(1, width).astype(F32)
    vspec = pl.BlockSpec((1, width), lambda b, s: (0, 0))
    wspec = pl.BlockSpec((LRU_BLOCKS, blk, blk), lambda b, s: (0, 0, 0))
    return pl.pallas_call(
        functools.partial(_lru_kernel, ts=ts),
        grid=(bn, per_b),
        in_specs=[
            pl.BlockSpec((ts, width), lambda b, s: (tok(b, s), 0)),
            pl.BlockSpec((ts, width), lambda b, s: (tok(b, s), 1)),
            pl.BlockSpec((CONV_W, width), lambda b, s: (0, 0)),
            vspec, wspec, vspec, wspec, vspec, vspec,
        ],
        out_specs=pl.BlockSpec((ts, width), lambda b, s: (tok(b, s), 0)),
        out_shape=jax.ShapeDtypeStruct((t, width), BF16),
        scratch_shapes=[
            pltpu.VMEM((SUBLANES, width), F32),
            pltpu.VMEM((1, width), F32),
            pltpu.VMEM((ts, width), F32),
            pltpu.VMEM((ts, width), F32),
        ],
        compiler_params=_params("arbitrary", "arbitrary"),
        name="lru",
    )(p, p, conv_w, vec(conv_b), w_ra.astype(BF16), vec(b_ra), w_ri.astype(BF16), vec(b_ri), vec(lam))


def _pack_bf16_pairs(h):
    half = h.shape[1] // 2
    lo = lax.bitcast_convert_type(h[:, :half].astype(BF16).astype(F32), jnp.uint32)
    hi = lax.bitcast_convert_type(h[:, half:].astype(BF16).astype(F32), jnp.uint32)
    return (lo >> 16) | (hi & jnp.uint32(0xFFFF0000))


def _unpack_bf16_pairs(w):
    lo = lax.bitcast_convert_type(w << 16, F32).astype(BF16)
    hi = lax.bitcast_convert_type(w & jnp.uint32(0xFFFF0000), F32).astype(BF16)
    return lo, hi


def _out_proj_kernel(*refs, n_in):
    a_refs = refs[:n_in]
    w_refs = refs[n_in:2 * n_in]
    (x_ref, mod_ref, ln_ref, rwh_ref, rwl_ref, rb_ref,
     xo_ref, hp_ref, ti_ref, tw_ref, tr_ref, cnt_ref, cnt_scr) = refs[2 * n_in:]

    @pl.when(pl.program_id(0) == 0)
    def _():
        cnt_scr[...] = jnp.zeros_like(cnt_scr)

    y = _dot(a_refs[0][...], w_refs[0][...])
    for a_ref, w_ref in zip(a_refs[1:], w_refs[1:]):
        y = y + _dot(a_ref[...], w_ref[...])
    m = mod_ref[0]
    xn = x_ref[...] + m[2:3] * y
    xo_ref[...] = xn
    h = _norm_mod(xn, ln_ref[...], m[3:4], m[4:5])
    hp_ref[...] = _pack_bf16_pairs(h).reshape(hp_ref.shape)
    h_hi = h.astype(BF16)
    h_lo = (h - h_hi.astype(F32)).astype(BF16)
    logits = _dot(h_hi, rwh_ref[...]) + (_dot(h_hi, rwl_ref[...]) + _dot(h_lo, rwh_ref[...])) + rb_ref[...]
    lane = lax.broadcasted_iota(jnp.int32, logits.shape, 1)
    cur = jnp.where(lane < N_EXPERTS, logits, -jnp.inf)
    vals, idxs = [], []
    for _ in range(TOP_K):
        mx = jnp.max(cur, axis=-1, keepdims=True)
        ix = jnp.min(jnp.where(cur == mx, lane, LANES), axis=-1, keepdims=True)
        vals.append(mx)
        idxs.append(ix)
        cur = jnp.where(lane == ix, -jnp.inf, cur)
    es = [jnp.exp(v - vals[0]) for v in vals]
    den = es[0]
    for e in es[1:]:
        den = den + e
    tm = logits.shape[0]
    ri = lax.broadcasted_iota(jnp.int32, (tm, tm), 0)
    rj = lax.broadcasted_iota(jnp.int32, (tm, tm), 1)
    tri = jnp.where(rj < ri, 1.0, 0.0).astype(BF16)
    base = cnt_scr[...]
    ti = jnp.zeros(logits.shape, jnp.int32)
    tw = jnp.zeros(logits.shape, F32)
    tr = jnp.zeros(logits.shape, jnp.int32)
    for kk in range(TOP_K):
        sel = lane == idxs[kk]
        onehot = jnp.where(sel, 1.0, 0.0)
        before = _dot(tri, onehot.astype(BF16)) + base
        rank = jnp.sum(jnp.where(sel, before, 0.0), axis=-1, keepdims=True)
        base = base + jnp.sum(onehot, axis=0, keepdims=True)
        ti = jnp.where(lane == kk, idxs[kk], ti)
        tw = jnp.where(lane == kk, es[kk] / den, tw)
        tr = jnp.where(lane == kk, rank.astype(jnp.int32), tr)
    cnt_scr[...] = base
    ti_ref[...] = ti
    tw_ref[...] = tw
    tr_ref[...] = tr
    cnt_ref[...] = base.astype(jnp.int32)


def _out_proj(acts, weights, x, mod, ln_ffn, router_w, router_b, seq):
    t, d = x.shape
    tm = 256
    per_b = seq // tm
    n_in = len(acts)
    rw = jnp.zeros((d, LANES), F32).at[:, :N_EXPERTS].set(router_w.astype(F32))
    rw_hi = rw.astype(BF16)
    rw_lo = (rw - rw_hi.astype(F32)).astype(BF16)
    rb = jnp.zeros((1, LANES), F32).at[0, :N_EXPERTS].set(router_b.astype(F32))
    tiles = d // 2 // LANES
    in_specs = [pl.BlockSpec((tm, a.shape[1]), lambda i: (i, 0)) for a in acts]
    in_specs += [pl.BlockSpec(w.shape, lambda i: (0, 0)) for w in weights]
    in_specs += [
        pl.BlockSpec((tm, d), lambda i: (i, 0)),
        pl.BlockSpec((1, 6, d), lambda i: (i // per_b, 0, 0)),
        pl.BlockSpec((1, d), lambda i: (0, 0)),
        pl.BlockSpec((d, LANES), lambda i: (0, 0)),
        pl.BlockSpec((d, LANES), lambda i: (0, 0)),
        pl.BlockSpec((1, LANES), lambda i: (0, 0)),
    ]
    row_d = pl.BlockSpec((tm, d), lambda i: (i, 0))
    row_h = pl.BlockSpec((tm, tiles, LANES), lambda i: (i, 0, 0))
    row_l = pl.BlockSpec((tm, LANES), lambda i: (i, 0))
    return pl.pallas_call(
        functools.partial(_out_proj_kernel, n_in=n_in),
        grid=(t // tm,),
        in_specs=in_specs,
        out_specs=[row_d, row_h, row_l, row_l, row_l, pl.BlockSpec((1, LANES), lambda i: (0, 0))],
        out_shape=[
            jax.ShapeDtypeStruct((t, d), F32),
            jax.ShapeDtypeStruct((t, tiles, LANES), jnp.uint32),
            jax.ShapeDtypeStruct((t, LANES), jnp.int32),
            jax.ShapeDtypeStruct((t, LANES), F32),
            jax.ShapeDtypeStruct((t, LANES), jnp.int32),
            jax.ShapeDtypeStruct((1, LANES), jnp.int32),
        ],
        scratch_shapes=[pltpu.VMEM((1, LANES), F32)],
        compiler_params=_params("arbitrary"),
        name="out_proj",
    )(*acts, *weights, x, mod, ln_ffn.reshape(1, d), rw_hi, rw_lo, rb)


def _moe_routing(ti, tr, cnt, n_tok):
    n_blk = (n_tok * TOP_K) // MOE_BLOCK + N_EXPERTS
    counts = cnt[0, :N_EXPERTS]
    padded = ((counts + MOE_BLOCK - 1) // MOE_BLOCK) * MOE_BLOCK
    pend = jnp.cumsum(padded)
    pstart = pend - padded
    e = ti[:, :TOP_K]
    hit = e[:, :, None] == jnp.arange(N_EXPERTS, dtype=jnp.int32)
    dest = tr[:, :TOP_K] + jnp.sum(jnp.where(hit, pstart, 0), axis=-1)
    blk_start = jnp.arange(n_blk, dtype=jnp.int32) * MOE_BLOCK
    blk_e = jnp.minimum(jnp.sum(pend[None, :] <= blk_start[:, None], axis=1), N_EXPERTS - 1).astype(jnp.int32)
    i = jnp.arange(MOE_BLOCK, dtype=jnp.int32)
    tail = (pstart + counts)[:, None] + i[None, :]
    is_tail = i[None, :] < (padded - counts)[:, None]
    spare = (jnp.cumsum(jnp.logical_not(is_tail).reshape(-1).astype(jnp.int32)) - 1).reshape(is_tail.shape)
    pad_slots = jnp.where(is_tail, tail, pend[-1] + spare).reshape(-1)
    n_used = (pend[-1:] // MOE_BLOCK).astype(jnp.int32)
    return dest.astype(jnp.int32), pad_slots.astype(jnp.int32), blk_e, n_used


DISPATCH_TOKENS = 512
COMBINE_TOKENS = 128
ISSUE_TOKENS = 4
ISSUE_ROWS = ISSUE_TOKENS * TOP_K


def _dispatch_kernel(dest_ref, pad_ref, hp_ref, xs_hbm, zero_scr, sem):
    n_pad = pad_ref.shape[2]
    zero_scr[...] = jnp.zeros_like(zero_scr)

    def issue(g, c):
        t0 = g * ISSUE_TOKENS
        dsts = [dest_ref[0, 0, t0 * TOP_K + j] for j in range(ISSUE_ROWS)]
        for j, dst in enumerate(dsts):
            pltpu.make_async_copy(hp_ref.at[t0 + j // TOP_K], xs_hbm.at[dst], sem.at[0]).start(priority=j % 2)
        return c

    lax.fori_loop(0, DISPATCH_TOKENS // ISSUE_TOKENS, issue, 0)

    def issue_zero(g, c):
        dsts = [pad_ref[0, 0, g * ISSUE_ROWS + j] for j in range(ISSUE_ROWS)]
        for j, dst in enumerate(dsts):
            pltpu.make_async_copy(zero_scr, xs_hbm.at[dst], sem.at[0]).start(priority=j % 2)
        return c

    lax.fori_loop(0, n_pad // ISSUE_ROWS, issue_zero, 0)

    rows = DISPATCH_TOKENS * TOP_K + n_pad
    for _ in range(rows // DISPATCH_TOKENS):
        pltpu.make_async_copy(hp_ref, xs_hbm.at[pl.ds(0, DISPATCH_TOKENS)], sem.at[0]).wait()
    rem = rows % DISPATCH_TOKENS
    if rem:
        pltpu.make_async_copy(hp_ref.at[pl.ds(0, rem)], xs_hbm.at[pl.ds(0, rem)], sem.at[0]).wait()


def _dispatch(hp, dest, pad_slots, n_buf):
    t = hp.shape[0]
    n_steps = t // DISPATCH_TOKENS
    n_pad = pad_slots.shape[0] // n_steps
    assert n_pad * n_steps == pad_slots.shape[0] and n_pad % ISSUE_ROWS == 0
    smem = lambda n: pl.BlockSpec((1, 1, n), lambda i: (i, 0, 0), memory_space=pltpu.SMEM)
    return pl.pallas_call(
        _dispatch_kernel,
        grid=(n_steps,),
        in_specs=[
            smem(DISPATCH_TOKENS * TOP_K),
            smem(n_pad),
            pl.BlockSpec((DISPATCH_TOKENS,) + hp.shape[1:], lambda i: (i, 0, 0)),
        ],
        out_specs=pl.BlockSpec(memory_space=pl.ANY),
        out_shape=jax.ShapeDtypeStruct((n_buf,) + hp.shape[1:], jnp.uint32),
        scratch_shapes=[pltpu.VMEM(hp.shape[1:], jnp.uint32), pltpu.SemaphoreType.DMA((1,))],
        compiler_params=_params("arbitrary"),
        name="moe_dispatch",
    )(dest.reshape(n_steps, 1, DISPATCH_TOKENS * TOP_K), pad_slots.reshape(n_steps, 1, n_pad), hp)


def _moe_kernel(blk_e_ref, n_used_ref, xs_ref, wgu_ref, bgu_ref, wdn_ref, bdn_ref, y_ref):
    del blk_e_ref
    half = xs_ref.shape[1] * xs_ref.shape[2]
    d_ff = wdn_ref.shape[1]
    used = pl.program_id(0) < n_used_ref[0]

    @pl.when(used)
    def _():
        lo, hi = _unpack_bf16_pairs(xs_ref[...].reshape(MOE_BLOCK, half))
        gu = _dot(lo, wgu_ref[0, :half, :]) + _dot(hi, wgu_ref[0, half:, :]) + bgu_ref[0]
        gate = jnp.minimum(gu[:, :d_ff], SWIGLU_LIMIT)
        up = jnp.clip(gu[:, d_ff:], -SWIGLU_LIMIT, SWIGLU_LIMIT)
        act = (up + 1.0) * gate * _sigmoid(SWIGLU_ALPHA * gate)
        y = _dot(act.astype(BF16), wdn_ref[0]) + bdn_ref[0]
        y_ref[...] = y.reshape(y_ref.shape)

    @pl.when(jnp.logical_not(used))
    def _():
        y_ref[...] = jnp.zeros_like(y_ref)


def _moe_experts(xs, blk_e, n_used, w_gu, b_gu, w_dn, b_dn):
    n_buf, tiles, _ = xs.shape
    d = 2 * tiles * LANES
    d_ff = w_dn.shape[1]
    n_w = w_gu.shape[0]
    grid_spec = pltpu.PrefetchScalarGridSpec(
        num_scalar_prefetch=2,
        grid=(n_buf // MOE_BLOCK,),
        in_specs=[
            pl.BlockSpec((MOE_BLOCK, tiles, LANES), lambda b, e, u: (b, 0, 0)),
            pl.BlockSpec((1, d, 2 * d_ff), lambda b, e, u: (e[b], 0, 0)),
            pl.BlockSpec((1, 1, 2 * d_ff), lambda b, e, u: (e[b], 0, 0)),
            pl.BlockSpec((1, d_ff, d), lambda b, e, u: (e[b], 0, 0)),
            pl.BlockSpec((1, 1, d), lambda b, e, u: (e[b], 0, 0)),
        ],
        out_specs=pl.BlockSpec((MOE_BLOCK, d // LANES, LANES), lambda b, e, u: (b, 0, 0)),
    )
    return pl.pallas_call(
        _moe_kernel,
        grid_spec=grid_spec,
        out_shape=jax.ShapeDtypeStruct((n_buf, d // LANES, LANES), F32),
        compiler_params=_params("arbitrary"),
        name="moe_experts",
    )(blk_e, n_used, xs, w_gu, b_gu.reshape(n_w, 1, 2 * d_ff), w_dn, b_dn.reshape(n_w, 1, d))


def _combine_kernel(dest_ref, dest_next_ref, x_ref, tw_ref, mod_ref, lnf_ref, y_hbm, o_ref, ybuf, sem, *, final):
    i = pl.program_id(0)
    n = pl.num_programs(0)
    slot = i % 2
    tm = COMBINE_TOKENS

    def issue_group(ids_ref, s, t0):
        srcs = [ids_ref[0, 0, t0 * TOP_K + j] for j in range(ISSUE_ROWS)]
        for j, src in enumerate(srcs):
            pltpu.make_async_copy(y_hbm.at[src], ybuf.at[s, (j % TOP_K) * tm + t0 + j // TOP_K],
                                  sem.at[s]).start(priority=j % 2)

    @pl.when(i == 0)
    def _():
        def body(g, c):
            issue_group(dest_ref, 0, g * ISSUE_TOKENS)
            return c
        lax.fori_loop(0, tm // ISSUE_TOKENS, body, 0)

    pltpu.make_async_copy(y_hbm.at[pl.ds(0, TOP_K * tm)], ybuf.at[slot], sem.at[slot]).wait()
    for g in range(tm // ISSUE_TOKENS):
        issue_group(dest_next_ref, 1 - slot, g * ISSUE_TOKENS)
    tw = tw_ref[...]
    acc = None
    for kk in range(TOP_K):
        gate = jnp.broadcast_to(tw[:, kk:kk + 1], (tm, LANES)).reshape(tm, 1, LANES)
        term = gate * ybuf[slot, kk * tm:(kk + 1) * tm]
        acc = term if acc is None else acc + term
    xn = x_ref[...] + mod_ref[0][5:6] * acc.reshape(x_ref.shape)
    if final:
        xn = xn * lax.rsqrt(jnp.mean(xn * xn, axis=-1, keepdims=True) + EPS) * lnf_ref[...]
    o_ref[...] = xn

    @pl.when(i + 1 == n)
    def _():
        pltpu.make_async_copy(y_hbm.at[pl.ds(0, TOP_K * tm)], ybuf.at[1 - slot], sem.at[1 - slot]).wait()


def _combine(x, y, dest, tw, mod, ln_final, seq, final):
    t, d = x.shape
    tm = COMBINE_TOKENS
    per_b = seq // tm
    nt = t // tm
    ids = dest.reshape(nt, 1, tm * TOP_K)
    id_spec = lambda f: pl.BlockSpec((1, 1, tm * TOP_K), f, memory_space=pltpu.SMEM)
    return pl.pallas_call(
        functools.partial(_combine_kernel, final=final),
        grid=(nt,),
        in_specs=[
            id_spec(lambda i: (i, 0, 0)),
            id_spec(lambda i: (jnp.minimum(i + 1, nt - 1), 0, 0)),
            pl.BlockSpec((tm, d), lambda i: (i, 0)),
            pl.BlockSpec((tm, LANES), lambda i: (i, 0)),
            pl.BlockSpec((1, 6, d), lambda i: (i // per_b, 0, 0)),
            pl.BlockSpec((1, d), lambda i: (0, 0)),
            pl.BlockSpec(memory_space=pl.ANY),
        ],
        out_specs=pl.BlockSpec((tm, d), lambda i: (i, 0)),
        out_shape=jax.ShapeDtypeStruct((t, d), F32),
        scratch_shapes=[pltpu.VMEM((2, TOP_K * tm, d // LANES, LANES), F32), pltpu.SemaphoreType.DMA((2,))],
        compiler_params=_params("arbitrary"),
        name="moe_combine",
    )(ids, ids, x, tw, mod, ln_final.reshape(1, d), y)


def _ffn(x, hp, ti, tw, tr, cnt, mod, layer, w_gu, b_gu, w_dn, b_dn, ln_final, seq, final):
    t = x.shape[0]
    dest, pad_slots, blk_e, n_used = _moe_routing(ti, tr, cnt, t)
    n_buf = blk_e.shape[0] * MOE_BLOCK
    xs = _dispatch(hp, dest, pad_slots, n_buf)
    y = _moe_experts(xs, blk_e + layer * N_EXPERTS, n_used, w_gu, b_gu, w_dn, b_dn)
    return _combine(x, y, dest, tw, mod, ln_final, seq, final)


def _even_inproj_weights(w_in):
    gdn_cols = 4 * GDN_HEADS * GDN_DK
    small0 = gdn_cols
    gla0 = small0 + 2 * GDN_HEADS
    gla_cols = 2 * GLA_HEADS * GLA_DK + 2 * GLA_HEADS * GLA_DV
    lg0 = gla0 + gla_cols
    w_big = jnp.concatenate([w_in[:, :gdn_cols], w_in[:, gla0:lg0]], axis=1).astype(BF16)
    w_small = jnp.zeros((w_in.shape[0], LANES), F32)
    w_small = w_small.at[:, :2 * GDN_HEADS].set(w_in[:, small0:gla0])
    w_small = w_small.at[:, 2 * GDN_HEADS:2 * GDN_HEADS + GLA_RANK].set(w_in[:, lg0:lg0 + GLA_RANK])
    return w_big, w_small.astype(BF16)


def _even_mixer(x, mod, ln_mix, w_in, conv_w, a_log, dt_bias, gdn_norm, w_a2, b_a2, gla_norm, bn, seq):
    w_big, w_small = _even_inproj_weights(w_in)
    p, small = _norm_proj(x, ln_mix, mod, w_big, w_small, seq)
    col, row = _gdn_gates(small, a_log, dt_bias)
    o_a = _gdn(p, col, row, conv_w, gdn_norm, bn, seq)
    o_b = _gla(p, small, w_a2, b_a2, gla_norm, bn, seq)
    return o_a, o_b


def kernel(x, c, ln_mix, ln_ffn, ln_final, w_ada, b_ada, e_w_in, e_conv_w, e_a_log, e_dt_bias, e_gdn_norm, e_gla_w_a2, e_gla_b_a2, e_gla_norm, e_w_out, o_w_in, o_conv_w, o_conv_b, o_w_ra, o_b_ra, o_w_ri, o_b_ri, o_lam, o_w_out, router_w, router_b, exp_w_gu, exp_b_gu, exp_w_dn, exp_b_dn):
    bn, seq, d = x.shape
    depth = w_ada.shape[0]
    mods = _ada_mod(c, w_ada, b_ada)
    xf = x.reshape(bn * seq, d)
    w_gu = exp_w_gu.astype(BF16).reshape((-1,) + exp_w_gu.shape[2:])
    w_dn = exp_w_dn.astype(BF16).reshape((-1,) + exp_w_dn.shape[2:])
    b_gu = exp_b_gu.reshape((-1,) + exp_b_gu.shape[2:])
    b_dn = exp_b_dn.reshape((-1,) + exp_b_dn.shape[2:])
    for layer in range(depth):
        mod = mods[layer]
        j = layer // 2
        if layer % 2 == 0:
            o_a, o_b = _even_mixer(xf, mod, ln_mix[layer], e_w_in[j], e_conv_w[j], e_a_log[j], e_dt_bias[j],
                                   e_gdn_norm[j], e_gla_w_a2[j], e_gla_b_a2[j], e_gla_norm[j], bn, seq)
            split = o_a.shape[1]
            acts = [o_a, o_b]
            weights = [e_w_out[j][:split].astype(BF16), e_w_out[j][split:].astype(BF16)]
        else:
            p = _norm_proj(xf, ln_mix[layer], mod, o_w_in[j].astype(BF16), None, seq)
            acts = [_lru(p, o_conv_w[j], o_conv_b[j], o_w_ra[j], o_b_ra[j], o_w_ri[j], o_b_ri[j], o_lam[j], bn, seq)]
            weights = [o_w_out[j].astype(BF16)]
        xf, hp, ti, tw, tr, cnt = _out_proj(acts, weights, xf, mod, ln_ffn[layer], router_w[layer], router_b[layer], seq)
        xf = _ffn(xf, hp, ti, tw, tr, cnt, mod, layer, w_gu, b_gu, w_dn, b_dn,
                  ln_final, seq, final=(layer == depth - 1))
    return xf.reshape(bn, seq, d)
```

```python
import functools

import jax
import jax.numpy as jnp
from jax import lax
from jax.experimental import pallas as pl
from jax.experimental.pallas import tpu as pltpu

F32 = jnp.float32
BF16 = jnp.bfloat16
HIGHEST = lax.Precision.HIGHEST

EPS = 1e-6
CONV_W = 4
CHUNK = 64
GDN_HEADS = 8
GDN_DK = 128
GDN_DV = 128
GLA_HEADS = 8
GLA_DK = 64
GLA_DV = 128
GLA_RANK = 16
GLA_GATE_NORM = 16.0
LRU_BLOCKS = 8
LRU_C = 8.0
N_EXPERTS = 32
TOP_K = 4
SWIGLU_LIMIT = 7.0
SWIGLU_ALPHA = 1.702
MOE_BLOCK = 256

LANES = 128
SUBLANES = 8
VMEM_LIMIT = 56 * 1024 * 1024


def _dot(a, b, precision=None):
    return jnp.dot(a, b, preferred_element_type=F32, precision=precision)


def _dot_nt(a, b, precision=None):
    return lax.dot_general(a, b, (((1,), (1,)), ((), ())), preferred_element_type=F32, precision=precision)


def _dot_tn(a, b):
    return lax.dot_general(a, b, (((0,), (0,)), ((), ())), preferred_element_type=F32)


def _dot_split(a, b):
    a_hi = a.astype(BF16)
    b_hi = b.astype(BF16)
    a_lo = (a - a_hi.astype(F32)).astype(BF16)
    b_lo = (b - b_hi.astype(F32)).astype(BF16)
    return _dot(a_hi, b_hi) + (_dot(a_hi, b_lo) + _dot(a_lo, b_hi))


def _sigmoid(x):
    return 1.0 / (1.0 + jnp.exp(-x))


def _softplus(x):
    return jnp.maximum(x, 0.0) + jnp.log(1.0 + jnp.exp(-jnp.abs(x)))


def _silu(x):
    return x * _sigmoid(x)


def _params(*sem):
    return pltpu.CompilerParams(dimension_semantics=sem, vmem_limit_bytes=VMEM_LIMIT)


def _ada_kernel(c_ref, w_ref, b_ref, o_ref):
    c = c_ref[...]
    o_ref[0] = _dot(_silu(c), w_ref[0], precision=HIGHEST) + b_ref[0]


def _ada_mod(c, w_ada, b_ada):
    depth, d, n6 = w_ada.shape
    bn = c.shape[0]
    rows = -(-bn // SUBLANES) * SUBLANES
    cp = jnp.zeros((rows, d), F32).at[:bn].set(c)
    tn = 768
    out = pl.pallas_call(
        _ada_kernel,
        grid=(depth, n6 // tn),
        in_specs=[
            pl.BlockSpec((rows, d), lambda l, j: (0, 0)),
            pl.BlockSpec((1, d, tn), lambda l, j: (l, 0, j)),
            pl.BlockSpec((1, 1, tn), lambda l, j: (l, 0, j)),
        ],
        out_specs=pl.BlockSpec((1, rows, tn), lambda l, j: (l, 0, j)),
        out_shape=jax.ShapeDtypeStruct((depth, rows, n6), F32),
        compiler_params=_params("arbitrary", "arbitrary"),
        name="ada_mod",
    )(cp, w_ada, b_ada.reshape(depth, 1, n6))
    return out[:, :bn].reshape(depth, bn, 6, d)


def _norm_mod(x, ln, shift, scale):
    y = x * lax.rsqrt(jnp.mean(x * x, axis=-1, keepdims=True) + EPS) * ln
    return y * (1.0 + scale) + shift


def _norm_proj_kernel(x_ref, ln_ref, mod_ref, w_ref, *rest, has_small):
    if has_small:
        ws_ref, o_ref, os_ref, h_scr = rest
    else:
        o_ref, h_scr = rest
    j = pl.program_id(1)

    @pl.when(j == 0)
    def _():
        m = mod_ref[0]
        h = _norm_mod(x_ref[...], ln_ref[...], m[0:1], m[1:2]).astype(BF16)
        h_scr[...] = h
        if has_small:
            os_ref[...] = _dot(h, ws_ref[...])

    o_ref[...] = _dot(h_scr[...], w_ref[...]).astype(o_ref.dtype)


def _norm_proj(x, ln, mod, w, w_small, seq):
    t, d = x.shape
    n = w.shape[1]
    tm = 512
    tn = max(c for c in range(LANES, 2048 + 1, LANES) if n % c == 0)
    per_b = seq // tm
    has_small = w_small is not None
    in_specs = [
        pl.BlockSpec((tm, d), lambda i, j: (i, 0)),
        pl.BlockSpec((1, d), lambda i, j: (0, 0)),
        pl.BlockSpec((1, 6, d), lambda i, j: (i // per_b, 0, 0)),
        pl.BlockSpec((d, tn), lambda i, j: (0, j)),
    ]
    out_specs = [pl.BlockSpec((tm, tn), lambda i, j: (i, j))]
    out_shape = [jax.ShapeDtypeStruct((t, n), BF16)]
    args = [x, ln.reshape(1, d), mod, w]
    if has_small:
        in_specs.append(pl.BlockSpec((d, LANES), lambda i, j: (0, 0)))
        out_specs.append(pl.BlockSpec((tm, LANES), lambda i, j: (i, 0)))
        out_shape.append(jax.ShapeDtypeStruct((t, LANES), F32))
        args.append(w_small)
    outs = pl.pallas_call(
        functools.partial(_norm_proj_kernel, has_small=has_small),
        grid=(t // tm, n // tn),
        in_specs=in_specs,
        out_specs=out_specs,
        out_shape=out_shape,
        scratch_shapes=[pltpu.VMEM((tm, d), BF16)],
        compiler_params=_params("arbitrary", "arbitrary"),
        name="norm_proj",
    )(*args)
    return outs if has_small else outs[0]


def _gdn_gate_kernel(ga_ref, gb_ref, alog_ref, dtb_ref, gc_ref, beta_ref, eg_ref, ek_ref):
    g = -jnp.exp(alog_ref[...]) * _softplus(ga_ref[...] + dtb_ref[...])
    ci = lax.broadcasted_iota(jnp.int32, (LANES, LANES), 0)
    cj = lax.broadcasted_iota(jnp.int32, (LANES, LANES), 1)
    same = (ci // CHUNK) == (cj // CHUNK)
    upper = jnp.where(same & (ci <= cj), 1.0, 0.0).astype(F32)
    allc = jnp.where(same, 1.0, 0.0).astype(F32)
    gc = _dot(g, upper, precision=HIGHEST)
    g_last = _dot(g, allc, precision=HIGHEST)
    gc_ref[...] = gc
    beta_ref[...] = _sigmoid(gb_ref[...])
    eg_ref[...] = jnp.exp(gc)
    ek_ref[...] = jnp.exp(g_last - gc)


def _gdn_gates(small, a_log, dt_bias):
    t = small.shape[0]
    nb = t // LANES
    rows = GDN_HEADS * nb
    gb_t = small[:, 0:GDN_HEADS].T.reshape(rows, LANES)
    ga_t = small[:, GDN_HEADS:2 * GDN_HEADS].T.reshape(rows, LANES)
    alog_c = jnp.repeat(a_log.astype(F32), nb).reshape(rows, 1)
    dtb_c = jnp.repeat(dt_bias.astype(F32), nb).reshape(rows, 1)
    full = pl.BlockSpec((rows, LANES), lambda: (0, 0))
    col1 = pl.BlockSpec((rows, 1), lambda: (0, 0))
    gc, beta, eg, ek = pl.pallas_call(
        _gdn_gate_kernel,
        in_specs=[full, full, col1, col1],
        out_specs=[full] * 4,
        out_shape=[jax.ShapeDtypeStruct((rows, LANES), F32)] * 4,
        compiler_params=pltpu.CompilerParams(vmem_limit_bytes=VMEM_LIMIT),
        name="gdn_gates",
    )(ga_t, gb_t, alog_c, dtb_c)
    to_col = lambda a: a.reshape(GDN_HEADS, t).T
    col = jnp.concatenate([to_col(beta), to_col(gc), to_col(eg), to_col(ek)], axis=1)
    row = gc.reshape(GDN_HEADS, t // CHUNK, CHUNK).transpose(1, 0, 2)
    return col, row


def _gdn_kernel(q_ref, k_ref, v_ref, z_ref, col_ref, row_ref, cw_ref, nw_ref, o_ref,
                s_scr, tail_scr, qkv_scr, *, ts):
    width = GDN_HEADS * GDN_DK

    @pl.when(pl.program_id(1) == 0)
    def _():
        s_scr[...] = jnp.zeros_like(s_scr)
        tail_scr[...] = jnp.zeros_like(tail_scr)

    strip = 256
    for part, ref in enumerate((q_ref, k_ref, v_ref)):
        for s0 in range(0, width, strip):
            c0 = part * width + s0
            cur = ref[:, s0:s0 + strip].astype(F32)
            full = jnp.concatenate([tail_scr[:, c0:c0 + strip], cur], axis=0)
            w = cw_ref[:, c0:c0 + strip]
            acc = full * w[CONV_W - 1:CONV_W]
            for i in range(CONV_W - 1):
                acc = acc + pltpu.roll(full, CONV_W - 1 - i, axis=0) * w[i:i + 1]
            y = acc[SUBLANES:]
            qkv_scr[:, c0:c0 + strip] = _silu(y)
            tail_scr[:, c0:c0 + strip] = cur[ts - SUBLANES:]

    ri = lax.broadcasted_iota(jnp.int32, (CHUNK, CHUNK), 0)
    rj = lax.broadcasted_iota(jnp.int32, (CHUNK, CHUNK), 1)
    incl = ri >= rj
    strict = ri > rj
    eye = jnp.where(ri == rj, 1.0, 0.0).astype(F32)
    nw = nw_ref[...]

    def chunk_body(c, carry):
        r0 = pl.multiple_of(c * CHUNK, CHUNK)
        colc = col_ref[pl.ds(r0, CHUNK), :]
        rowc = row_ref[c]
        heads = range(GDN_HEADS)
        sl = [slice(h * GDN_DK, (h + 1) * GDN_DK) for h in heads]
        q = [qkv_scr[pl.ds(r0, CHUNK), sl[h]] for h in heads]
        k = [qkv_scr[pl.ds(r0, CHUNK), width + h * GDN_DK:width + (h + 1) * GDN_DK] for h in heads]
        v = [qkv_scr[pl.ds(r0, CHUNK), 2 * width + h * GDN_DK:2 * width + (h + 1) * GDN_DK] for h in heads]
        q = [a * lax.rsqrt(jnp.sum(a * a, axis=-1, keepdims=True) + EPS) * (GDN_DK ** -0.5) for a in q]
        k = [a * lax.rsqrt(jnp.sum(a * a, axis=-1, keepdims=True) + EPS) for a in k]
        beta = [colc[:, h:h + 1] for h in heads]
        gc = [colc[:, 8 + h:9 + h] for h in heads]
        eg = [colc[:, 16 + h:17 + h] for h in heads]
        ek = [colc[:, 24 + h:25 + h] for h in heads]
        decay = [jnp.exp(jnp.where(incl, gc[h] - rowc[h:h + 1, :], -jnp.inf)) for h in heads]
        kb = [a.astype(BF16) for a in k]
        qb = [a.astype(BF16) for a in q]
        kk = [_dot_nt(kb[h], kb[h]) for h in heads]
        qk = [_dot_nt(qb[h], kb[h]) for h in heads]
        npow = [jnp.where(strict, -(beta[h] * kk[h] * decay[h]), 0.0) for h in heads]
        ainv = [eye + npow[h] for h in heads]
        for _ in range(5):
            npow = [_dot_split(npow[h], npow[h]) for h in heads]
            ainv = [ainv[h] + _dot_split(ainv[h], npow[h]) for h in heads]
        ab = [a.astype(BF16) for a in ainv]
        u = [_dot(ab[h], (v[h] * beta[h]).astype(BF16)) for h in heads]
        w = [_dot(ab[h], (k[h] * (beta[h] * eg[h])).astype(BF16)) for h in heads]
        qk = [jnp.where(incl, qk[h] * decay[h], 0.0).astype(BF16) for h in heads]
        q_s = [(q[h] * eg[h]).astype(BF16) for h in heads]
        k_s = [(k[h] * ek[h]).astype(BF16) for h in heads]
        state = [s_scr[h] for h in heads]
        sb = [a.astype(BF16) for a in state]
        v_new = [(u[h] - _dot(w[h].astype(BF16), sb[h])).astype(BF16) for h in heads]
        o = [_dot(q_s[h], sb[h]) + _dot(qk[h], v_new[h]) for h in heads]
        for h in heads:
            s_scr[h] = state[h] * eg[h][CHUNK - 1:CHUNK, :] + _dot_tn(k_s[h], v_new[h])
        for h in heads:
            oh = o[h] * lax.rsqrt(jnp.mean(o[h] * o[h], axis=-1, keepdims=True) + EPS) * nw
            z = z_ref[pl.ds(r0, CHUNK), sl[h]].astype(F32)
            o_ref[pl.ds(r0, CHUNK), sl[h]] = (oh * _silu(z)).astype(o_ref.dtype)
        return carry

    lax.fori_loop(0, ts // CHUNK, chunk_body, 0)


def _gdn(p, col, row, conv_w, norm_w, bn, seq):
    t = p.shape[0]
    ts = 256
    per_b = seq // ts
    width = GDN_HEADS * GDN_DK
    tok = lambda b, s: b * per_b + s
    pspec = lambda cb: pl.BlockSpec((ts, width), lambda b, s: (tok(b, s), cb))
    return pl.pallas_call(
        functools.partial(_gdn_kernel, ts=ts),
        grid=(bn, per_b),
        in_specs=[
            pspec(0), pspec(1), pspec(2), pspec(3),
            pl.BlockSpec((ts, 4 * GDN_HEADS), lambda b, s: (tok(b, s), 0)),
            pl.BlockSpec((ts // CHUNK, GDN_HEADS, CHUNK), lambda b, s: (tok(b, s), 0, 0)),
            pl.BlockSpec((CONV_W, 3 * width), lambda b, s: (0, 0)),
            pl.BlockSpec((1, GDN_DV), lambda b, s: (0, 0)),
        ],
        out_specs=pl.BlockSpec((ts, width), lambda b, s: (tok(b, s), 0)),
        out_shape=jax.ShapeDtypeStruct((t, width), BF16),
        scratch_shapes=[
            pltpu.VMEM((GDN_HEADS, GDN_DK, GDN_DV), F32),
            pltpu.VMEM((SUBLANES, 3 * width), F32),
            pltpu.VMEM((ts, 3 * width), F32),
        ],
        compiler_params=_params("arbitrary", "arbitrary"),
        name="gdn",
    )(p, p, p, p, col, row, conv_w, norm_w.reshape(1, GDN_DV))


def _gla_kernel(q_ref, k_ref, v_ref, r_ref, sm_ref, wa_ref, ba_ref, nw_ref, o_ref, st_scr, g_scr, *, ts):
    pair_w = 2 * GLA_DK
    n_pairs = GLA_HEADS // 2

    @pl.when(pl.program_id(1) == 0)
    def _():
        st_scr[...] = jnp.zeros_like(st_scr)

    logits = _dot(sm_ref[...], wa_ref[...], precision=HIGHEST) + ba_ref[...]
    g_scr[...] = -_softplus(-logits) / GLA_GATE_NORM

    ri = lax.broadcasted_iota(jnp.int32, (CHUNK, CHUNK), 0)
    rj = lax.broadcasted_iota(jnp.int32, (CHUNK, CHUNK), 1)
    incl = ri >= rj
    lower = jnp.where(incl, 1.0, 0.0).astype(BF16)
    lane = lax.broadcasted_iota(jnp.int32, (CHUNK, pair_w), 1)
    sr = lax.broadcasted_iota(jnp.int32, (2 * GLA_DV, pair_w), 0)
    sc = lax.broadcasted_iota(jnp.int32, (2 * GLA_DV, pair_w), 1)
    live = (sr // GLA_DV) == (sc // GLA_DK)
    nw = nw_ref[...]

    def chunk_body(c, carry):
        r0 = pl.multiple_of(c * CHUNK, CHUNK)
        pairs = range(n_pairs)
        heads = range(GLA_HEADS)
        psl = [slice(p * pair_w, (p + 1) * pair_w) for p in pairs]
        hsl = [slice(h * GLA_DV, (h + 1) * GLA_DV) for h in heads]
        g = g_scr[pl.ds(r0, CHUNK), :]
        g1 = g.astype(BF16)
        rem = g - g1.astype(F32)
        g2 = rem.astype(BF16)
        g3 = (rem - g2.astype(F32)).astype(BF16)
        b_all = _dot(lower, g1) + (_dot(lower, g2) + _dot(lower, g3))
        b = [b_all[:, psl[p]] for p in pairs]
        b_mid = [a[CHUNK // 2:CHUNK // 2 + 1, :] for a in b]
        b_last = [a[CHUNK - 1:CHUNK, :] for a in b]
        q = [q_ref[pl.ds(r0, CHUNK), psl[p]].astype(F32) * (GLA_DK ** -0.5) for p in pairs]
        k = [k_ref[pl.ds(r0, CHUNK), psl[p]].astype(F32) for p in pairs]
        q_i = [q[p] * jnp.exp(b[p] - b_mid[p]) for p in pairs]
        k_i = [(k[p] * jnp.exp(b_mid[p] - b[p])).astype(BF16) for p in pairs]
        q_s = [(q[p] * jnp.exp(b[p])).astype(BF16) for p in pairs]
        k_s = [(k[p] * jnp.exp(b_last[p] - b[p])).astype(BF16) for p in pairs]
        vp = [v_ref[pl.ds(r0, CHUNK), 2 * p * GLA_DV:(2 * p + 2) * GLA_DV] for p in pairs]
        st = [st_scr[p] for p in pairs]
        o_inter = [_dot_nt(q_s[p], st[p].astype(BF16)) for p in pairs]
        upd = [_dot_tn(vp[p], k_s[p]) for p in pairs]
        for p in pairs:
            st_scr[p] = st[p] * jnp.exp(b_last[p]) + jnp.where(live, upd[p], 0.0)
        q_h = [jnp.where((lane // GLA_DK) == (h % 2), q_i[h // 2], 0.0).astype(BF16) for h in heads]
        att = [jnp.where(incl, _dot_nt(q_h[h], k_i[h // 2]), 0.0).astype(BF16) for h in heads]
        o = [_dot(att[h], vp[h // 2][:, (h % 2) * GLA_DV:(h % 2 + 1) * GLA_DV])
             + o_inter[h // 2][:, (h % 2) * GLA_DV:(h % 2 + 1) * GLA_DV] for h in heads]
        for h in heads:
            oh = o[h] * lax.rsqrt(jnp.mean(o[h] * o[h], axis=-1, keepdims=True) + EPS) * nw
            rg = r_ref[pl.ds(r0, CHUNK), hsl[h]].astype(F32)
            o_ref[pl.ds(r0, CHUNK), hsl[h]] = (oh * _silu(rg)).astype(o_ref.dtype)
        return carry

    lax.fori_loop(0, ts // CHUNK, chunk_body, 0)


def _gla(p, small, w_a2, b_a2, norm_w, bn, seq):
    t = p.shape[0]
    ts = 256
    per_b = seq // ts
    qk_w = GLA_HEADS * GLA_DK
    v_w = GLA_HEADS * GLA_DV
    gdn_cols = 4 * GDN_HEADS * GDN_DK
    tok = lambda b, s: b * per_b + s
    wa_pad = jnp.zeros((LANES, qk_w), F32).at[2 * GDN_HEADS:2 * GDN_HEADS + GLA_RANK].set(w_a2.astype(F32))
    return pl.pallas_call(
        functools.partial(_gla_kernel, ts=ts),
        grid=(bn, per_b),
        in_specs=[
            pl.BlockSpec((ts, qk_w), lambda b, s: (tok(b, s), gdn_cols // qk_w)),
            pl.BlockSpec((ts, qk_w), lambda b, s: (tok(b, s), gdn_cols // qk_w + 1)),
            pl.BlockSpec((ts, v_w), lambda b, s: (tok(b, s), (gdn_cols + 2 * qk_w) // v_w)),
            pl.BlockSpec((ts, v_w), lambda b, s: (tok(b, s), (gdn_cols + 2 * qk_w) // v_w + 1)),
            pl.BlockSpec((ts, LANES), lambda b, s: (tok(b, s), 0)),
            pl.BlockSpec((LANES, qk_w), lambda b, s: (0, 0)),
            pl.BlockSpec((1, qk_w), lambda b, s: (0, 0)),
            pl.BlockSpec((1, GLA_DV), lambda b, s: (0, 0)),
        ],
        out_specs=pl.BlockSpec((ts, v_w), lambda b, s: (tok(b, s), 0)),
        out_shape=jax.ShapeDtypeStruct((t, v_w), BF16),
        scratch_shapes=[
            pltpu.VMEM((GLA_HEADS // 2, 2 * GLA_DV, 2 * GLA_DK), F32),
            pltpu.VMEM((ts, qk_w), F32),
        ],
        compiler_params=_params("arbitrary", "arbitrary"),
        name="gla",
    )(p, p, p, p, small, wa_pad, b_a2.reshape(1, qk_w).astype(F32), norm_w.reshape(1, GLA_DV))


def _gelu_tanh(x):
    return 0.5 * x * (1.0 + jnp.tanh(0.7978845608028654 * (x + 0.044715 * x * x * x)))


def _lru_kernel(x_ref, g_ref, cw_ref, cb_ref, wra_ref, bra_ref, wri_ref, bri_ref, lam_ref, o_ref,
                tail_scr, h_scr, a_scr, u_scr, *, ts):
    width = x_ref.shape[1]
    blk = width // LRU_BLOCKS

    @pl.when(pl.program_id(1) == 0)
    def _():
        tail_scr[...] = jnp.zeros_like(tail_scr)
        h_scr[...] = jnp.zeros_like(h_scr)

    for gi in range(LRU_BLOCKS):
        lo, hi = gi * blk, (gi + 1) * blk
        cur = x_ref[:, lo:hi].astype(F32)
        full = jnp.concatenate([tail_scr[:, lo:hi], cur], axis=0)
        w = cw_ref[:, lo:hi]
        acc = full * w[CONV_W - 1:CONV_W]
        for i in range(CONV_W - 1):
            acc = acc + pltpu.roll(full, CONV_W - 1 - i, axis=0) * w[i:i + 1]
        xc = acc[SUBLANES:] + cb_ref[:, lo:hi]
        tail_scr[:, lo:hi] = cur[ts - SUBLANES:]
        xb = xc.astype(BF16)
        r = _sigmoid(_dot(xb, wra_ref[gi]) + bra_ref[:, lo:hi])
        ig = _sigmoid(_dot(xb, wri_ref[gi]) + bri_ref[:, lo:hi])
        log_a = -LRU_C * r * _softplus(-lam_ref[:, lo:hi])
        a = jnp.exp(log_a)
        a_scr[:, lo:hi] = a
        u_scr[:, lo:hi] = jnp.sqrt(1.0 - a * a) * (ig * xc)

    row = lax.broadcasted_iota(jnp.int32, (SUBLANES, width), 0)

    def scan_body(rb, h_prev):
        r0 = pl.multiple_of(rb * SUBLANES, SUBLANES)
        a = a_scr[pl.ds(r0, SUBLANES), :]
        u = u_scr[pl.ds(r0, SUBLANES), :]
        for d in (1, 2, 4):
            a_sh = jnp.where(row >= d, pltpu.roll(a, d, axis=0), 1.0)
            u_sh = jnp.where(row >= d, pltpu.roll(u, d, axis=0), 0.0)
            u = a * u_sh + u
            a = a * a_sh
        h = u + a * h_prev
        gate = g_ref[pl.ds(r0, SUBLANES), :].astype(F32)
        u_scr[pl.ds(r0, SUBLANES), :] = _gelu_tanh(gate) * h
        return h[SUBLANES - 1:SUBLANES, :]

    h_last = lax.fori_loop(0, ts // SUBLANES, scan_body, h_scr[...])
    h_scr[...] = h_last
    o_ref[...] = u_scr[...].astype(o_ref.dtype)


def _lru(p, conv_w, conv_b, w_ra, b_ra, w_ri, b_ri, lam, bn, seq):
    t = p.shape[0]
    width = p.shape[1] // 2
    blk = width // LRU_BLOCKS
    ts = 256
    per_b = seq // ts
    tok = lambda b, s: b * per_b + s
    vec = lambda a: a.reshape(1, width).astype(F32)
    vspec = pl.BlockSpec((1, width), lambda b, s: (0, 0))
    wspec = pl.BlockSpec((LRU_BLOCKS, blk, blk), lambda b, s: (0, 0, 0))
    return pl.pallas_call(
        functools.partial(_lru_kernel, ts=ts),
        grid=(bn, per_b),
        in_specs=[
            pl.BlockSpec((ts, width), lambda b, s: (tok(b, s), 0)),
            pl.BlockSpec((ts, width), lambda b, s: (tok(b, s), 1)),
            pl.BlockSpec((CONV_W, width), lambda b, s: (0, 0)),
            vspec, wspec, vspec, wspec, vspec, vspec,
        ],
        out_specs=pl.BlockSpec((ts, width), lambda b, s: (tok(b, s), 0)),
        out_shape=jax.ShapeDtypeStruct((t, width), BF16),
        scratch_shapes=[
            pltpu.VMEM((SUBLANES, width), F32),
            pltpu.VMEM((1, width), F32),
            pltpu.VMEM((ts, width), F32),
            pltpu.VMEM((ts, width), F32),
        ],
        compiler_params=_params("arbitrary", "arbitrary"),
        name="lru",
    )(p, p, conv_w, vec(conv_b), w_ra.astype(BF16), vec(b_ra), w_ri.astype(BF16), vec(b_ri), vec(lam))


def _pack_bf16_pairs(h):
    half = h.shape[1] // 2
    lo = lax.bitcast_convert_type(h[:, :half].astype(BF16).astype(F32), jnp.uint32)
    hi = lax.bitcast_convert_type(h[:, half:].astype(BF16).astype(F32), jnp.uint32)
    return (lo >> 16) | (hi & jnp.uint32(0xFFFF0000))


def _unpack_bf16_pairs(w):
    lo = lax.bitcast_convert_type(w << 16, F32).astype(BF16)
    hi = lax.bitcast_convert_type(w & jnp.uint32(0xFFFF0000), F32).astype(BF16)
    return lo, hi


def _out_proj_kernel(*refs, n_in):
    a_refs = refs[:n_in]
    w_refs = refs[n_in:2 * n_in]
    (x_ref, mod_ref, ln_ref, rwh_ref, rwl_ref, rb_ref,
     xo_ref, hp_ref, ti_ref, tw_ref, tr_ref, cnt_ref, cnt_scr) = refs[2 * n_in:]

    @pl.when(pl.program_id(0) == 0)
    def _():
        cnt_scr[...] = jnp.zeros_like(cnt_scr)

    y = _dot(a_refs[0][...], w_refs[0][...])
    for a_ref, w_ref in zip(a_refs[1:], w_refs[1:]):
        y = y + _dot(a_ref[...], w_ref[...])
    m = mod_ref[0]
    xn = x_ref[...] + m[2:3] * y
    xo_ref[...] = xn
    h = _norm_mod(xn, ln_ref[...], m[3:4], m[4:5])
    hp_ref[...] = _pack_bf16_pairs(h).reshape(hp_ref.shape)
    h_hi = h.astype(BF16)
    h_lo = (h - h_hi.astype(F32)).astype(BF16)
    logits = _dot(h_hi, rwh_ref[...]) + (_dot(h_hi, rwl_ref[...]) + _dot(h_lo, rwh_ref[...])) + rb_ref[...]
    lane = lax.broadcasted_iota(jnp.int32, logits.shape, 1)
    cur = jnp.where(lane < N_EXPERTS, logits, -jnp.inf)
    vals, idxs = [], []
    for _ in range(TOP_K):
        mx = jnp.max(cur, axis=-1, keepdims=True)
        ix = jnp.min(jnp.where(cur == mx, lane, LANES), axis=-1, keepdims=True)
        vals.append(mx)
        idxs.append(ix)
        cur = jnp.where(lane == ix, -jnp.inf, cur)
    es = [jnp.exp(v - vals[0]) for v in vals]
    den = es[0]
    for e in es[1:]:
        den = den + e
    tm = logits.shape[0]
    ri = lax.broadcasted_iota(jnp.int32, (tm, tm), 0)
    rj = lax.broadcasted_iota(jnp.int32, (tm, tm), 1)
    tri = jnp.where(rj < ri, 1.0, 0.0).astype(BF16)
    base = cnt_scr[...]
    ti = jnp.zeros(logits.shape, jnp.int32)
    tw = jnp.zeros(logits.shape, F32)
    tr = jnp.zeros(logits.shape, jnp.int32)
    for kk in range(TOP_K):
        sel = lane == idxs[kk]
        onehot = jnp.where(sel, 1.0, 0.0)
        before = _dot(tri, onehot.astype(BF16)) + base
        rank = jnp.sum(jnp.where(sel, before, 0.0), axis=-1, keepdims=True)
        base = base + jnp.sum(onehot, axis=0, keepdims=True)
        ti = jnp.where(lane == kk, idxs[kk], ti)
        tw = jnp.where(lane == kk, es[kk] / den, tw)
        tr = jnp.where(lane == kk, rank.astype(jnp.int32), tr)
    cnt_scr[...] = base
    ti_ref[...] = ti
    tw_ref[...] = tw
    tr_ref[...] = tr
    cnt_ref[...] = base.astype(jnp.int32)


def _out_proj(acts, weights, x, mod, ln_ffn, router_w, router_b, seq):
    t, d = x.shape
    tm = 256
    per_b = seq // tm
    n_in = len(acts)
    rw = jnp.zeros((d, LANES), F32).at[:, :N_EXPERTS].set(router_w.astype(F32))
    rw_hi = rw.astype(BF16)
    rw_lo = (rw - rw_hi.astype(F32)).astype(BF16)
    rb = jnp.zeros((1, LANES), F32).at[0, :N_EXPERTS].set(router_b.astype(F32))
    tiles = d // 2 // LANES
    in_specs = [pl.BlockSpec((tm, a.shape[1]), lambda i: (i, 0)) for a in acts]
    in_specs += [pl.BlockSpec(w.shape, lambda i: (0, 0)) for w in weights]
    in_specs += [
        pl.BlockSpec((tm, d), lambda i: (i, 0)),
        pl.BlockSpec((1, 6, d), lambda i: (i // per_b, 0, 0)),
        pl.BlockSpec((1, d), lambda i: (0, 0)),
        pl.BlockSpec((d, LANES), lambda i: (0, 0)),
        pl.BlockSpec((d, LANES), lambda i: (0, 0)),
        pl.BlockSpec((1, LANES), lambda i: (0, 0)),
    ]
    row_d = pl.BlockSpec((tm, d), lambda i: (i, 0))
    row_h = pl.BlockSpec((tm, tiles, LANES), lambda i: (i, 0, 0))
    row_l = pl.BlockSpec((tm, LANES), lambda i: (i, 0))
    return pl.pallas_call(
        functools.partial(_out_proj_kernel, n_in=n_in),
        grid=(t // tm,),
        in_specs=in_specs,
        out_specs=[row_d, row_h, row_l, row_l, row_l, pl.BlockSpec((1, LANES), lambda i: (0, 0))],
        out_shape=[
            jax.ShapeDtypeStruct((t, d), F32),
            jax.ShapeDtypeStruct((t, tiles, LANES), jnp.uint32),
            jax.ShapeDtypeStruct((t, LANES), jnp.int32),
            jax.ShapeDtypeStruct((t, LANES), F32),
            jax.ShapeDtypeStruct((t, LANES), jnp.int32),
            jax.ShapeDtypeStruct((1, LANES), jnp.int32),
        ],
        scratch_shapes=[pltpu.VMEM((1, LANES), F32)],
        compiler_params=_params("arbitrary"),
        name="out_proj",
    )(*acts, *weights, x, mod, ln_ffn.reshape(1, d), rw_hi, rw_lo, rb)


def _moe_routing(ti, tr, cnt, n_tok):
    n_blk = (n_tok * TOP_K) // MOE_BLOCK + N_EXPERTS
    counts = cnt[0, :N_EXPERTS]
    padded = ((counts + MOE_BLOCK - 1) // MOE_BLOCK) * MOE_BLOCK
    pend = jnp.cumsum(padded)
    pstart = pend - padded
    e = ti[:, :TOP_K]
    hit = e[:, :, None] == jnp.arange(N_EXPERTS, dtype=jnp.int32)
    dest = tr[:, :TOP_K] + jnp.sum(jnp.where(hit, pstart, 0), axis=-1)
    blk_start = jnp.arange(n_blk, dtype=jnp.int32) * MOE_BLOCK
    blk_e = jnp.minimum(jnp.sum(pend[None, :] <= blk_start[:, None], axis=1), N_EXPERTS - 1).astype(jnp.int32)
    i = jnp.arange(MOE_BLOCK, dtype=jnp.int32)
    tail = (pstart + counts)[:, None] + i[None, :]
    is_tail = i[None, :] < (padded - counts)[:, None]
    spare = (jnp.cumsum(jnp.logical_not(is_tail).reshape(-1).astype(jnp.int32)) - 1).reshape(is_tail.shape)
    pad_slots = jnp.where(is_tail, tail, pend[-1] + spare).reshape(-1)
    n_used = (pend[-1:] // MOE_BLOCK).astype(jnp.int32)
    first = jnp.concatenate([jnp.ones((1,), jnp.int32), (blk_e[1:] != blk_e[:-1]).astype(jnp.int32)])
    ex = jnp.arange(N_EXPERTS, dtype=jnp.int32)
    later = (ex[None, :] > ex[:, None]) & (padded > 0)[None, :]
    nxt_of = jnp.min(jnp.where(later, ex[None, :], N_EXPERTS), axis=1)
    nxt_of = jnp.where(nxt_of == N_EXPERTS, -1, nxt_of)
    nxt = jnp.sum(jnp.where(blk_e[:, None] == ex[None, :], nxt_of[None, :], 0), axis=1).astype(jnp.int32)
    return dest.astype(jnp.int32), pad_slots.astype(jnp.int32), blk_e, first, nxt, n_used


DISPATCH_TOKENS = 512
COMBINE_TOKENS = 128
ISSUE_TOKENS = 4
ISSUE_ROWS = ISSUE_TOKENS * TOP_K


def _dispatch_kernel(dest_ref, pad_ref, hp_ref, xs_hbm, zero_scr, sem):
    n_pad = pad_ref.shape[2]
    zero_scr[...] = jnp.zeros_like(zero_scr)

    def issue(g, c):
        t0 = g * ISSUE_TOKENS
        dsts = [dest_ref[0, 0, t0 * TOP_K + j] for j in range(ISSUE_ROWS)]
        for j, dst in enumerate(dsts):
            pltpu.make_async_copy(hp_ref.at[t0 + j // TOP_K], xs_hbm.at[dst], sem.at[0]).start(priority=j % 2)
        return c

    lax.fori_loop(0, DISPATCH_TOKENS // ISSUE_TOKENS, issue, 0)

    def issue_zero(g, c):
        dsts = [pad_ref[0, 0, g * ISSUE_ROWS + j] for j in range(ISSUE_ROWS)]
        for j, dst in enumerate(dsts):
            pltpu.make_async_copy(zero_scr, xs_hbm.at[dst], sem.at[0]).start(priority=j % 2)
        return c

    lax.fori_loop(0, n_pad // ISSUE_ROWS, issue_zero, 0)

    rows = DISPATCH_TOKENS * TOP_K + n_pad
    for _ in range(rows // DISPATCH_TOKENS):
        pltpu.make_async_copy(hp_ref, xs_hbm.at[pl.ds(0, DISPATCH_TOKENS)], sem.at[0]).wait()
    rem = rows % DISPATCH_TOKENS
    if rem:
        pltpu.make_async_copy(hp_ref.at[pl.ds(0, rem)], xs_hbm.at[pl.ds(0, rem)], sem.at[0]).wait()


def _dispatch(hp, dest, pad_slots, n_buf):
    t = hp.shape[0]
    n_steps = t // DISPATCH_TOKENS
    n_pad = pad_slots.shape[0] // n_steps
    assert n_pad * n_steps == pad_slots.shape[0] and n_pad % ISSUE_ROWS == 0
    smem = lambda n: pl.BlockSpec((1, 1, n), lambda i: (i, 0, 0), memory_space=pltpu.SMEM)
    return pl.pallas_call(
        _dispatch_kernel,
        grid=(n_steps,),
        in_specs=[
            smem(DISPATCH_TOKENS * TOP_K),
            smem(n_pad),
            pl.BlockSpec((DISPATCH_TOKENS,) + hp.shape[1:], lambda i: (i, 0, 0)),
        ],
        out_specs=pl.BlockSpec(memory_space=pl.ANY),
        out_shape=jax.ShapeDtypeStruct((n_buf,) + hp.shape[1:], jnp.uint32),
        scratch_shapes=[pltpu.VMEM(hp.shape[1:], jnp.uint32), pltpu.SemaphoreType.DMA((1,))],
        compiler_params=_params("arbitrary"),
        name="moe_dispatch",
    )(dest.reshape(n_steps, 1, DISPATCH_TOKENS * TOP_K), pad_slots.reshape(n_steps, 1, n_pad), hp)


W_ROWS = 256


def _moe_kernel(blk_e_ref, first_ref, nxt_ref, n_used_ref, xs_ref, wgu_hbm, bgu_ref, wdn_hbm, bdn_ref, y_ref,
                wbuf, stage, sem, st):
    b = pl.program_id(0)
    half = xs_ref.shape[1] * xs_ref.shape[2]
    d = 2 * half
    d_ff = wdn_hbm.shape[1]
    n_gu = d // W_ROWS
    n_ch = (d + d_ff) // W_ROWS
    used = b < n_used_ref[0]

    def start_chunk(e, c):
        @pl.when(c < n_gu)
        def _():
            r0 = pl.multiple_of(c * W_ROWS, W_ROWS)
            pltpu.make_async_copy(wgu_hbm.at[e, pl.ds(r0, W_ROWS)], stage.at[c % 2], sem.at[c % 2]).start()

        @pl.when(c >= n_gu)
        def _():
            r0 = pl.multiple_of((c - n_gu) * W_ROWS, W_ROWS)
            pltpu.make_async_copy(wdn_hbm.at[e, pl.ds(r0, W_ROWS)], stage.at[c % 2], sem.at[c % 2]).start()

    def wait_chunk(c):
        pltpu.make_async_copy(wgu_hbm.at[0, pl.ds(0, W_ROWS)], stage.at[c % 2], sem.at[c % 2]).wait()

    def cast_chunk(c, slot):
        wbuf[slot, pl.ds(pl.multiple_of(c * W_ROWS, W_ROWS), W_ROWS), :] = stage[c % 2].astype(BF16)

    @pl.when(used)
    def _():
        e = blk_e_ref[b]

        @pl.when(b == 0)
        def _():
            st[0] = 1
            st[1] = 0
            start_chunk(e, 0)

        @pl.when(first_ref[b] == 1)
        def _():
            tgt = 1 - st[0]

            def body(c, carry):
                wait_chunk(c)

                @pl.when(c + 1 < n_ch)
                def _():
                    start_chunk(e, c + 1)

                cast_chunk(c, tgt)
                return carry

            lax.fori_loop(st[1], n_ch, body, 0)
            st[0] = tgt
            nxt = nxt_ref[b]

            @pl.when(nxt >= 0)
            def _():
                st[1] = 0
                start_chunk(nxt, 0)

            @pl.when(nxt < 0)
            def _():
                st[1] = n_ch

        cur = st[0]
        done = st[1]
        c = jnp.minimum(done, n_ch - 1)

        @pl.when(done < n_ch)
        def _():
            wait_chunk(c)

            @pl.when(c + 1 < n_ch)
            def _():
                start_chunk(nxt_ref[b], c + 1)

            st[1] = c + 1

        lo, hi = _unpack_bf16_pairs(xs_ref[...].reshape(MOE_BLOCK, half))
        gu = _dot(lo, wbuf[cur, 0:half, :]) + _dot(hi, wbuf[cur, half:d, :]) + bgu_ref[0]
        gate = jnp.minimum(gu[:, :d_ff], SWIGLU_LIMIT)
        up = jnp.clip(gu[:, d_ff:], -SWIGLU_LIMIT, SWIGLU_LIMIT)
        act = (up + 1.0) * gate * _sigmoid(SWIGLU_ALPHA * gate)
        y = _dot(act.astype(BF16), wbuf[cur, d:d + d_ff, :]) + bdn_ref[0]
        y_ref[...] = y.reshape(y_ref.shape)
        cast_chunk(c, 1 - cur)

    @pl.when(jnp.logical_not(used))
    def _():
        y_ref[...] = jnp.zeros_like(y_ref)


def _moe_experts(xs, blk_e, first, nxt, n_used, w_gu, b_gu, w_dn, b_dn):
    n_buf, tiles, _ = xs.shape
    d = 2 * tiles * LANES
    d_ff = w_dn.shape[1]
    n_w = w_gu.shape[0]
    assert w_gu.shape[2] == w_dn.shape[2] and d % W_ROWS == 0 and d_ff % W_ROWS == 0
    width = w_gu.shape[2]
    grid_spec = pltpu.PrefetchScalarGridSpec(
        num_scalar_prefetch=4,
        grid=(n_buf // MOE_BLOCK,),
        in_specs=[
            pl.BlockSpec((MOE_BLOCK, tiles, LANES), lambda b, e, f, x, u: (b, 0, 0)),
            pl.BlockSpec(memory_space=pl.ANY),
            pl.BlockSpec((1, 1, 2 * d_ff), lambda b, e, f, x, u: (e[b], 0, 0)),
            pl.BlockSpec(memory_space=pl.ANY),
            pl.BlockSpec((1, 1, d), lambda b, e, f, x, u: (e[b], 0, 0)),
        ],
        out_specs=pl.BlockSpec((MOE_BLOCK, d // LANES, LANES), lambda b, e, f, x, u: (b, 0, 0)),
        scratch_shapes=[
            pltpu.VMEM((2, d + d_ff, width), BF16),
            pltpu.VMEM((2, W_ROWS, width), F32),
            pltpu.SemaphoreType.DMA((2,)),
            pltpu.SMEM((2,), jnp.int32),
        ],
    )
    return pl.pallas_call(
        _moe_kernel,
        grid_spec=grid_spec,
        out_shape=jax.ShapeDtypeStruct((n_buf, d // LANES, LANES), F32),
        compiler_params=_params("arbitrary"),
        name="moe_experts",
    )(blk_e, first, nxt, n_used, xs, w_gu, b_gu.reshape(n_w, 1, 2 * d_ff), w_dn, b_dn.reshape(n_w, 1, d))


def _combine_kernel(dest_ref, dest_next_ref, x_ref, tw_ref, mod_ref, lnf_ref, y_hbm, o_ref, ybuf, sem, *, final):
    i = pl.program_id(0)
    n = pl.num_programs(0)
    slot = i % 2
    tm = COMBINE_TOKENS

    def issue(ids_ref, s):
        def body(g, c):
            t0 = g * ISSUE_TOKENS
            srcs = [ids_ref[0, 0, t0 * TOP_K + j] for j in range(ISSUE_TOKENS * TOP_K)]
            for j, src in enumerate(srcs):
                pltpu.make_async_copy(y_hbm.at[src], ybuf.at[s, (j % TOP_K) * tm + t0 + j // TOP_K],
                                      sem.at[s]).start(priority=j % 2)
            return c
        lax.fori_loop(0, tm // ISSUE_TOKENS, body, 0)

    @pl.when(i == 0)
    def _():
        issue(dest_ref, 0)

    @pl.when(i + 1 < n)
    def _():
        issue(dest_next_ref, 1 - slot)

    pltpu.make_async_copy(y_hbm.at[pl.ds(0, TOP_K * tm)], ybuf.at[slot], sem.at[slot]).wait()
    tw = tw_ref[...]
    acc = None
    for kk in range(TOP_K):
        gate = jnp.broadcast_to(tw[:, kk:kk + 1], (tm, LANES)).reshape(tm, 1, LANES)
        term = gate * ybuf[slot, kk * tm:(kk + 1) * tm]
        acc = term if acc is None else acc + term
    xn = x_ref[...] + mod_ref[0][5:6] * acc.reshape(x_ref.shape)
    if final:
        xn = xn * lax.rsqrt(jnp.mean(xn * xn, axis=-1, keepdims=True) + EPS) * lnf_ref[...]
    o_ref[...] = xn


def _combine(x, y, dest, tw, mod, ln_final, seq, final):
    t, d = x.shape
    tm = COMBINE_TOKENS
    per_b = seq // tm
    nt = t // tm
    ids = dest.reshape(nt, 1, tm * TOP_K)
    id_spec = lambda f: pl.BlockSpec((1, 1, tm * TOP_K), f, memory_space=pltpu.SMEM)
    return pl.pallas_call(
        functools.partial(_combine_kernel, final=final),
        grid=(nt,),
        in_specs=[
            id_spec(lambda i: (i, 0, 0)),
            id_spec(lambda i: (jnp.minimum(i + 1, nt - 1), 0, 0)),
            pl.BlockSpec((tm, d), lambda i: (i, 0)),
            pl.BlockSpec((tm, LANES), lambda i: (i, 0)),
            pl.BlockSpec((1, 6, d), lambda i: (i // per_b, 0, 0)),
            pl.BlockSpec((1, d), lambda i: (0, 0)),
            pl.BlockSpec(memory_space=pl.ANY),
        ],
        out_specs=pl.BlockSpec((tm, d), lambda i: (i, 0)),
        out_shape=jax.ShapeDtypeStruct((t, d), F32),
        scratch_shapes=[pltpu.VMEM((2, TOP_K * tm, d // LANES, LANES), F32), pltpu.SemaphoreType.DMA((2,))],
        compiler_params=_params("arbitrary"),
        name="moe_combine",
    )(ids, ids, x, tw, mod, ln_final.reshape(1, d), y)


def _ffn(x, hp, ti, tw, tr, cnt, mod, layer, w_gu, b_gu, w_dn, b_dn, ln_final, seq, final):
    t = x.shape[0]
    dest, pad_slots, blk_e, first, nxt, n_used = _moe_routing(ti, tr, cnt, t)
    n_buf = blk_e.shape[0] * MOE_BLOCK
    xs = _dispatch(hp, dest, pad_slots, n_buf)
    off = layer * N_EXPERTS
    y = _moe_experts(xs, blk_e + off, first, jnp.where(nxt >= 0, nxt + off, -1), n_used, w_gu, b_gu, w_dn, b_dn)
    return _combine(x, y, dest, tw, mod, ln_final, seq, final)


def _even_inproj_weights(w_in):
    gdn_cols = 4 * GDN_HEADS * GDN_DK
    small0 = gdn_cols
    gla0 = small0 + 2 * GDN_HEADS
    gla_cols = 2 * GLA_HEADS * GLA_DK + 2 * GLA_HEADS * GLA_DV
    lg0 = gla0 + gla_cols
    w_big = jnp.concatenate([w_in[:, :gdn_cols], w_in[:, gla0:lg0]], axis=1).astype(BF16)
    w_small = jnp.zeros((w_in.shape[0], LANES), F32)
    w_small = w_small.at[:, :2 * GDN_HEADS].set(w_in[:, small0:gla0])
    w_small = w_small.at[:, 2 * GDN_HEADS:2 * GDN_HEADS + GLA_RANK].set(w_in[:, lg0:lg0 + GLA_RANK])
    return w_big, w_small.astype(BF16)


def _even_mixer(x, mod, ln_mix, w_in, conv_w, a_log, dt_bias, gdn_norm, w_a2, b_a2, gla_norm, bn, seq):
    w_big, w_small = _even_inproj_weights(w_in)
    p, small = _norm_proj(x, ln_mix, mod, w_big, w_small, seq)
    col, row = _gdn_gates(small, a_log, dt_bias)
    o_a = _gdn(p, col, row, conv_w, gdn_norm, bn, seq)
    o_b = _gla(p, small, w_a2, b_a2, gla_norm, bn, seq)
    return o_a, o_b


def kernel(x, c, ln_mix, ln_ffn, ln_final, w_ada, b_ada, e_w_in, e_conv_w, e_a_log, e_dt_bias, e_gdn_norm, e_gla_w_a2, e_gla_b_a2, e_gla_norm, e_w_out, o_w_in, o_conv_w, o_conv_b, o_w_ra, o_b_ra, o_w_ri, o_b_ri, o_lam, o_w_out, router_w, router_b, exp_w_gu, exp_b_gu, exp_w_dn, exp_b_dn):
    bn, seq, d = x.shape
    depth = w_ada.shape[0]
    mods = _ada_mod(c, w_ada, b_ada)
    xf = x.reshape(bn * seq, d)
    w_gu = exp_w_gu.reshape((-1,) + exp_w_gu.shape[2:])
    w_dn = exp_w_dn.reshape((-1,) + exp_w_dn.shape[2:])
    b_gu = exp_b_gu.reshape((-1,) + exp_b_gu.shape[2:])
    b_dn = exp_b_dn.reshape((-1,) + exp_b_dn.shape[2:])
    for layer in range(depth):
        mod = mods[layer]
        j = layer // 2
        if layer % 2 == 0:
            o_a, o_b = _even_mixer(xf, mod, ln_mix[layer], e_w_in[j], e_conv_w[j], e_a_log[j], e_dt_bias[j],
                                   e_gdn_norm[j], e_gla_w_a2[j], e_gla_b_a2[j], e_gla_norm[j], bn, seq)
            split = o_a.shape[1]
            acts = [o_a, o_b]
            weights = [e_w_out[j][:split].astype(BF16), e_w_out[j][split:].astype(BF16)]
        else:
            p = _norm_proj(xf, ln_mix[layer], mod, o_w_in[j].astype(BF16), None, seq)
            acts = [_lru(p, o_conv_w[j], o_conv_b[j], o_w_ra[j], o_b_ra[j], o_w_ri[j], o_b_ri[j], o_lam[j], bn, seq)]
            weights = [o_w_out[j].astype(BF16)]
        xf, hp, ti, tw, tr, cnt = _out_proj(acts, weights, xf, mod, ln_ffn[layer], router_w[layer], router_b[layer], seq)
        xf = _ffn(xf, hp, ti, tw, tr, cnt, mod, layer, w_gu, b_gu, w_dn, b_dn,
                  ln_final, seq, final=(layer == depth - 1))
    return xf.reshape(bn, seq, d)
```

```python
import functools

import jax
import jax.numpy as jnp
from jax import lax
from jax.experimental import pallas as pl
from jax.experimental.pallas import tpu as pltpu

F32 = jnp.float32
BF16 = jnp.bfloat16
HIGHEST = lax.Precision.HIGHEST

EPS = 1e-6
CONV_W = 4
CHUNK = 64
GDN_HEADS = 8
GDN_DK = 128
GDN_DV = 128
GLA_HEADS = 8
GLA_DK = 64
GLA_DV = 128
GLA_RANK = 16
GLA_GATE_NORM = 16.0
LRU_BLOCKS = 8
LRU_C = 8.0
N_EXPERTS = 32
TOP_K = 4
SWIGLU_LIMIT = 7.0
SWIGLU_ALPHA = 1.702
MOE_BLOCK = 512

LANES = 128
SUBLANES = 8
VMEM_LIMIT = 56 * 1024 * 1024


def _dot(a, b, precision=None):
    return jnp.dot(a, b, preferred_element_type=F32, precision=precision)


def _dot_nt(a, b, precision=None):
    return lax.dot_general(a, b, (((1,), (1,)), ((), ())), preferred_element_type=F32, precision=precision)


def _dot_tn(a, b):
    return lax.dot_general(a, b, (((0,), (0,)), ((), ())), preferred_element_type=F32)


def _dot_split(a, b):
    a_hi = a.astype(BF16)
    b_hi = b.astype(BF16)
    a_lo = (a - a_hi.astype(F32)).astype(BF16)
    b_lo = (b - b_hi.astype(F32)).astype(BF16)
    return _dot(a_hi, b_hi) + (_dot(a_hi, b_lo) + _dot(a_lo, b_hi))


def _sigmoid(x):
    return 1.0 / (1.0 + jnp.exp(-x))


def _softplus(x):
    return jnp.maximum(x, 0.0) + jnp.log(1.0 + jnp.exp(-jnp.abs(x)))


def _silu(x):
    return x * _sigmoid(x)


def _params(*sem):
    return pltpu.CompilerParams(dimension_semantics=sem, vmem_limit_bytes=VMEM_LIMIT)


def _ada_kernel(c_ref, w_ref, b_ref, o_ref):
    c = c_ref[...]
    o_ref[0] = _dot(_silu(c), w_ref[0], precision=HIGHEST) + b_ref[0]


def _ada_mod(c, w_ada, b_ada):
    depth, d, n6 = w_ada.shape
    bn = c.shape[0]
    rows = -(-bn // SUBLANES) * SUBLANES
    cp = jnp.zeros((rows, d), F32).at[:bn].set(c)
    tn = 768
    out = pl.pallas_call(
        _ada_kernel,
        grid=(depth, n6 // tn),
        in_specs=[
            pl.BlockSpec((rows, d), lambda l, j: (0, 0)),
            pl.BlockSpec((1, d, tn), lambda l, j: (l, 0, j)),
            pl.BlockSpec((1, 1, tn), lambda l, j: (l, 0, j)),
        ],
        out_specs=pl.BlockSpec((1, rows, tn), lambda l, j: (l, 0, j)),
        out_shape=jax.ShapeDtypeStruct((depth, rows, n6), F32),
        compiler_params=_params("arbitrary", "arbitrary"),
        name="ada_mod",
    )(cp, w_ada, b_ada.reshape(depth, 1, n6))
    return out[:, :bn].reshape(depth, bn, 6, d)


def _norm_mod(x, ln, shift, scale):
    y = x * lax.rsqrt(jnp.mean(x * x, axis=-1, keepdims=True) + EPS) * ln
    return y * (1.0 + scale) + shift


def _norm_proj_kernel(x_ref, ln_ref, mod_ref, w_ref, *rest, has_small):
    if has_small:
        ws_ref, o_ref, os_ref, h_scr = rest
    else:
        o_ref, h_scr = rest
    j = pl.program_id(1)

    @pl.when(j == 0)
    def _():
        m = mod_ref[0]
        h = _norm_mod(x_ref[...], ln_ref[...], m[0:1], m[1:2]).astype(BF16)
        h_scr[...] = h
        if has_small:
            os_ref[...] = _dot(h, ws_ref[...])

    o_ref[...] = _dot(h_scr[...], w_ref[...]).astype(o_ref.dtype)


def _norm_proj(x, ln, mod, w, w_small, seq):
    t, d = x.shape
    n = w.shape[1]
    tm = 512
    tn = max(c for c in range(LANES, 2048 + 1, LANES) if n % c == 0)
    per_b = seq // tm
    has_small = w_small is not None
    in_specs = [
        pl.BlockSpec((tm, d), lambda i, j: (i, 0)),
        pl.BlockSpec((1, d), lambda i, j: (0, 0)),
        pl.BlockSpec((1, 6, d), lambda i, j: (i // per_b, 0, 0)),
        pl.BlockSpec((d, tn), lambda i, j: (0, j)),
    ]
    out_specs = [pl.BlockSpec((tm, tn), lambda i, j: (i, j))]
    out_shape = [jax.ShapeDtypeStruct((t, n), BF16)]
    args = [x, ln.reshape(1, d), mod, w]
    if has_small:
        in_specs.append(pl.BlockSpec((d, LANES), lambda i, j: (0, 0)))
        out_specs.append(pl.BlockSpec((tm, LANES), lambda i, j: (i, 0)))
        out_shape.append(jax.ShapeDtypeStruct((t, LANES), F32))
        args.append(w_small)
    outs = pl.pallas_call(
        functools.partial(_norm_proj_kernel, has_small=has_small),
        grid=(t // tm, n // tn),
        in_specs=in_specs,
        out_specs=out_specs,
        out_shape=out_shape,
        scratch_shapes=[pltpu.VMEM((tm, d), BF16)],
        compiler_params=_params("arbitrary", "arbitrary"),
        name="norm_proj",
    )(*args)
    return outs if has_small else outs[0]


def _gdn_gate_kernel(ga_ref, gb_ref, alog_ref, dtb_ref, gc_ref, beta_ref, eg_ref, ek_ref):
    g = -jnp.exp(alog_ref[...]) * _softplus(ga_ref[...] + dtb_ref[...])
    ci = lax.broadcasted_iota(jnp.int32, (LANES, LANES), 0)
    cj = lax.broadcasted_iota(jnp.int32, (LANES, LANES), 1)
    same = (ci // CHUNK) == (cj // CHUNK)
    upper = jnp.where(same & (ci <= cj), 1.0, 0.0).astype(F32)
    allc = jnp.where(same, 1.0, 0.0).astype(F32)
    gc = _dot(g, upper, precision=HIGHEST)
    g_last = _dot(g, allc, precision=HIGHEST)
    gc_ref[...] = gc
    beta_ref[...] = _sigmoid(gb_ref[...])
    eg_ref[...] = jnp.exp(gc)
    ek_ref[...] = jnp.exp(g_last - gc)


def _gdn_gates(small, a_log, dt_bias):
    t = small.shape[0]
    nb = t // LANES
    rows = GDN_HEADS * nb
    gb_t = small[:, 0:GDN_HEADS].T.reshape(rows, LANES)
    ga_t = small[:, GDN_HEADS:2 * GDN_HEADS].T.reshape(rows, LANES)
    alog_c = jnp.repeat(a_log.astype(F32), nb).reshape(rows, 1)
    dtb_c = jnp.repeat(dt_bias.astype(F32), nb).reshape(rows, 1)
    full = pl.BlockSpec((rows, LANES), lambda: (0, 0))
    col1 = pl.BlockSpec((rows, 1), lambda: (0, 0))
    gc, beta, eg, ek = pl.pallas_call(
        _gdn_gate_kernel,
        in_specs=[full, full, col1, col1],
        out_specs=[full] * 4,
        out_shape=[jax.ShapeDtypeStruct((rows, LANES), F32)] * 4,
        compiler_params=pltpu.CompilerParams(vmem_limit_bytes=VMEM_LIMIT),
        name="gdn_gates",
    )(ga_t, gb_t, alog_c, dtb_c)
    to_col = lambda a: a.reshape(GDN_HEADS, t).T
    col = jnp.concatenate([to_col(beta), to_col(gc), to_col(eg), to_col(ek)], axis=1)
    row = gc.reshape(GDN_HEADS, t // CHUNK, CHUNK).transpose(1, 0, 2)
    return col, row


def _gdn_kernel(q_ref, k_ref, v_ref, z_ref, col_ref, row_ref, cw_ref, nw_ref, o_ref,
                s_scr, tail_scr, qkv_scr, *, ts):
    width = GDN_HEADS * GDN_DK

    @pl.when(pl.program_id(1) == 0)
    def _():
        s_scr[...] = jnp.zeros_like(s_scr)
        tail_scr[...] = jnp.zeros_like(tail_scr)

    strip = 256
    for part, ref in enumerate((q_ref, k_ref, v_ref)):
        for s0 in range(0, width, strip):
            c0 = part * width + s0
            cur = ref[:, s0:s0 + strip].astype(F32)
            full = jnp.concatenate([tail_scr[:, c0:c0 + strip], cur], axis=0)
            w = cw_ref[:, c0:c0 + strip]
            acc = full * w[CONV_W - 1:CONV_W]
            for i in range(CONV_W - 1):
                acc = acc + pltpu.roll(full, CONV_W - 1 - i, axis=0) * w[i:i + 1]
            y = acc[SUBLANES:]
            qkv_scr[:, c0:c0 + strip] = _silu(y)
            tail_scr[:, c0:c0 + strip] = cur[ts - SUBLANES:]

    ri = lax.broadcasted_iota(jnp.int32, (CHUNK, CHUNK), 0)
    rj = lax.broadcasted_iota(jnp.int32, (CHUNK, CHUNK), 1)
    incl = ri >= rj
    strict = ri > rj
    eye = jnp.where(ri == rj, 1.0, 0.0).astype(F32)
    nw = nw_ref[...]

    def chunk_body(c, carry):
        r0 = pl.multiple_of(c * CHUNK, CHUNK)
        colc = col_ref[pl.ds(r0, CHUNK), :]
        rowc = row_ref[c]
        heads = range(GDN_HEADS)
        sl = [slice(h * GDN_DK, (h + 1) * GDN_DK) for h in heads]
        q = [qkv_scr[pl.ds(r0, CHUNK), sl[h]] for h in heads]
        k = [qkv_scr[pl.ds(r0, CHUNK), width + h * GDN_DK:width + (h + 1) * GDN_DK] for h in heads]
        v = [qkv_scr[pl.ds(r0, CHUNK), 2 * width + h * GDN_DK:2 * width + (h + 1) * GDN_DK] for h in heads]
        q = [a * lax.rsqrt(jnp.sum(a * a, axis=-1, keepdims=True) + EPS) * (GDN_DK ** -0.5) for a in q]
        k = [a * lax.rsqrt(jnp.sum(a * a, axis=-1, keepdims=True) + EPS) for a in k]
        beta = [colc[:, h:h + 1] for h in heads]
        gc = [colc[:, 8 + h:9 + h] for h in heads]
        eg = [colc[:, 16 + h:17 + h] for h in heads]
        ek = [colc[:, 24 + h:25 + h] for h in heads]
        decay = [jnp.exp(jnp.where(incl, gc[h] - rowc[h:h + 1, :], -jnp.inf)) for h in heads]
        kb = [a.astype(BF16) for a in k]
        qb = [a.astype(BF16) for a in q]
        kk = [_dot_nt(kb[h], kb[h]) for h in heads]
        qk = [_dot_nt(qb[h], kb[h]) for h in heads]
        npow = [jnp.where(strict, -(beta[h] * kk[h] * decay[h]), 0.0) for h in heads]
        ainv = [eye + npow[h] for h in heads]
        for _ in range(5):
            npow = [_dot_split(npow[h], npow[h]) for h in heads]
            ainv = [ainv[h] + _dot_split(ainv[h], npow[h]) for h in heads]
        ab = [a.astype(BF16) for a in ainv]
        u = [_dot(ab[h], (v[h] * beta[h]).astype(BF16)) for h in heads]
        w = [_dot(ab[h], (k[h] * (beta[h] * eg[h])).astype(BF16)) for h in heads]
        qk = [jnp.where(incl, qk[h] * decay[h], 0.0).astype(BF16) for h in heads]
        q_s = [(q[h] * eg[h]).astype(BF16) for h in heads]
        k_s = [(k[h] * ek[h]).astype(BF16) for h in heads]
        state = [s_scr[h] for h in heads]
        sb = [a.astype(BF16) for a in state]
        v_new = [(u[h] - _dot(w[h].astype(BF16), sb[h])).astype(BF16) for h in heads]
        o = [_dot(q_s[h], sb[h]) + _dot(qk[h], v_new[h]) for h in heads]
        for h in heads:
            s_scr[h] = state[h] * eg[h][CHUNK - 1:CHUNK, :] + _dot_tn(k_s[h], v_new[h])
        for h in heads:
            oh = o[h] * lax.rsqrt(jnp.mean(o[h] * o[h], axis=-1, keepdims=True) + EPS) * nw
            z = z_ref[pl.ds(r0, CHUNK), sl[h]].astype(F32)
            o_ref[pl.ds(r0, CHUNK), sl[h]] = (oh * _silu(z)).astype(o_ref.dtype)
        return carry

    lax.fori_loop(0, ts // CHUNK, chunk_body, 0)


def _gdn(p, col, row, conv_w, norm_w, bn, seq):
    t = p.shape[0]
    ts = 256
    per_b = seq // ts
    width = GDN_HEADS * GDN_DK
    tok = lambda b, s: b * per_b + s
    pspec = lambda cb: pl.BlockSpec((ts, width), lambda b, s: (tok(b, s), cb))
    return pl.pallas_call(
        functools.partial(_gdn_kernel, ts=ts),
        grid=(bn, per_b),
        in_specs=[
            pspec(0), pspec(1), pspec(2), pspec(3),
            pl.BlockSpec((ts, 4 * GDN_HEADS), lambda b, s: (tok(b, s), 0)),
            pl.BlockSpec((ts // CHUNK, GDN_HEADS, CHUNK), lambda b, s: (tok(b, s), 0, 0)),
            pl.BlockSpec((CONV_W, 3 * width), lambda b, s: (0, 0)),
            pl.BlockSpec((1, GDN_DV), lambda b, s: (0, 0)),
        ],
        out_specs=pl.BlockSpec((ts, width), lambda b, s: (tok(b, s), 0)),
        out_shape=jax.ShapeDtypeStruct((t, width), BF16),
        scratch_shapes=[
            pltpu.VMEM((GDN_HEADS, GDN_DK, GDN_DV), F32),
            pltpu.VMEM((SUBLANES, 3 * width), F32),
            pltpu.VMEM((ts, 3 * width), F32),
        ],
        compiler_params=_params("arbitrary", "arbitrary"),
        name="gdn",
    )(p, p, p, p, col, row, conv_w, norm_w.reshape(1, GDN_DV))


def _gla_kernel(q_ref, k_ref, v_ref, r_ref, sm_ref, wa_ref, ba_ref, nw_ref, o_ref, st_scr, g_scr, *, ts):
    pair_w = 2 * GLA_DK
    n_pairs = GLA_HEADS // 2

    @pl.when(pl.program_id(1) == 0)
    def _():
        st_scr[...] = jnp.zeros_like(st_scr)

    logits = _dot(sm_ref[...], wa_ref[...], precision=HIGHEST) + ba_ref[...]
    g_scr[...] = -_softplus(-logits) / GLA_GATE_NORM

    ri = lax.broadcasted_iota(jnp.int32, (CHUNK, CHUNK), 0)
    rj = lax.broadcasted_iota(jnp.int32, (CHUNK, CHUNK), 1)
    incl = ri >= rj
    lower = jnp.where(incl, 1.0, 0.0).astype(BF16)
    lane = lax.broadcasted_iota(jnp.int32, (CHUNK, pair_w), 1)
    sr = lax.broadcasted_iota(jnp.int32, (2 * GLA_DV, pair_w), 0)
    sc = lax.broadcasted_iota(jnp.int32, (2 * GLA_DV, pair_w), 1)
    live = (sr // GLA_DV) == (sc // GLA_DK)
    nw = nw_ref[...]

    def chunk_body(c, carry):
        r0 = pl.multiple_of(c * CHUNK, CHUNK)
        pairs = range(n_pairs)
        heads = range(GLA_HEADS)
        psl = [slice(p * pair_w, (p + 1) * pair_w) for p in pairs]
        hsl = [slice(h * GLA_DV, (h + 1) * GLA_DV) for h in heads]
        g = g_scr[pl.ds(r0, CHUNK), :]
        g1 = g.astype(BF16)
        rem = g - g1.astype(F32)
        g2 = rem.astype(BF16)
        g3 = (rem - g2.astype(F32)).astype(BF16)
        b_all = _dot(lower, g1) + (_dot(lower, g2) + _dot(lower, g3))
        b = [b_all[:, psl[p]] for p in pairs]
        b_mid = [a[CHUNK // 2:CHUNK // 2 + 1, :] for a in b]
        b_last = [a[CHUNK - 1:CHUNK, :] for a in b]
        q = [q_ref[pl.ds(r0, CHUNK), psl[p]].astype(F32) * (GLA_DK ** -0.5) for p in pairs]
        k = [k_ref[pl.ds(r0, CHUNK), psl[p]].astype(F32) for p in pairs]
        q_i = [q[p] * jnp.exp(b[p] - b_mid[p]) for p in pairs]
        k_i = [(k[p] * jnp.exp(b_mid[p] - b[p])).astype(BF16) for p in pairs]
        q_s = [(q[p] * jnp.exp(b[p])).astype(BF16) for p in pairs]
        k_s = [(k[p] * jnp.exp(b_last[p] - b[p])).astype(BF16) for p in pairs]
        vp = [v_ref[pl.ds(r0, CHUNK), 2 * p * GLA_DV:(2 * p + 2) * GLA_DV] for p in pairs]
        st = [st_scr[p] for p in pairs]
        o_inter = [_dot_nt(q_s[p], st[p].astype(BF16)) for p in pairs]
        upd = [_dot_tn(vp[p], k_s[p]) for p in pairs]
        for p in pairs:
            st_scr[p] = st[p] * jnp.exp(b_last[p]) + jnp.where(live, upd[p], 0.0)
        q_h = [jnp.where((lane // GLA_DK) == (h % 2), q_i[h // 2], 0.0).astype(BF16) for h in heads]
        att = [jnp.where(incl, _dot_nt(q_h[h], k_i[h // 2]), 0.0).astype(BF16) for h in heads]
        o = [_dot(att[h], vp[h // 2][:, (h % 2) * GLA_DV:(h % 2 + 1) * GLA_DV])
             + o_inter[h // 2][:, (h % 2) * GLA_DV:(h % 2 + 1) * GLA_DV] for h in heads]
        for h in heads:
            oh = o[h] * lax.rsqrt(jnp.mean(o[h] * o[h], axis=-1, keepdims=True) + EPS) * nw
            rg = r_ref[pl.ds(r0, CHUNK), hsl[h]].astype(F32)
            o_ref[pl.ds(r0, CHUNK), hsl[h]] = (oh * _silu(rg)).astype(o_ref.dtype)
        return carry

    lax.fori_loop(0, ts // CHUNK, chunk_body, 0)


def _gla(p, small, w_a2, b_a2, norm_w, bn, seq):
    t = p.shape[0]
    ts = 256
    per_b = seq // ts
    qk_w = GLA_HEADS * GLA_DK
    v_w = GLA_HEADS * GLA_DV
    gdn_cols = 4 * GDN_HEADS * GDN_DK
    tok = lambda b, s: b * per_b + s
    wa_pad = jnp.zeros((LANES, qk_w), F32).at[2 * GDN_HEADS:2 * GDN_HEADS + GLA_RANK].set(w_a2.astype(F32))
    return pl.pallas_call(
        functools.partial(_gla_kernel, ts=ts),
        grid=(bn, per_b),
        in_specs=[
            pl.BlockSpec((ts, qk_w), lambda b, s: (tok(b, s), gdn_cols // qk_w)),
            pl.BlockSpec((ts, qk_w), lambda b, s: (tok(b, s), gdn_cols // qk_w + 1)),
            pl.BlockSpec((ts, v_w), lambda b, s: (tok(b, s), (gdn_cols + 2 * qk_w) // v_w)),
            pl.BlockSpec((ts, v_w), lambda b, s: (tok(b, s), (gdn_cols + 2 * qk_w) // v_w + 1)),
            pl.BlockSpec((ts, LANES), lambda b, s: (tok(b, s), 0)),
            pl.BlockSpec((LANES, qk_w), lambda b, s: (0, 0)),
            pl.BlockSpec((1, qk_w), lambda b, s: (0, 0)),
            pl.BlockSpec((1, GLA_DV), lambda b, s: (0, 0)),
        ],
        out_specs=pl.BlockSpec((ts, v_w), lambda b, s: (tok(b, s), 0)),
        out_shape=jax.ShapeDtypeStruct((t, v_w), BF16),
        scratch_shapes=[
            pltpu.VMEM((GLA_HEADS // 2, 2 * GLA_DV, 2 * GLA_DK), F32),
            pltpu.VMEM((ts, qk_w), F32),
        ],
        compiler_params=_params("arbitrary", "arbitrary"),
        name="gla",
    )(p, p, p, p, small, wa_pad, b_a2.reshape(1, qk_w).astype(F32), norm_w.reshape(1, GLA_DV))


def _gelu_tanh(x):
    return 0.5 * x * (1.0 + jnp.tanh(0.7978845608028654 * (x + 0.044715 * x * x * x)))


def _lru_kernel(x_ref, g_ref, cw_ref, cb_ref, wra_ref, bra_ref, wri_ref, bri_ref, lam_ref, o_ref,
                tail_scr, h_scr, a_scr, u_scr, *, ts):
    width = x_ref.shape[1]
    blk = width // LRU_BLOCKS

    @pl.when(pl.program_id(1) == 0)
    def _():
        tail_scr[...] = jnp.zeros_like(tail_scr)
        h_scr[...] = jnp.zeros_like(h_scr)

    for gi in range(LRU_BLOCKS):
        lo, hi = gi * blk, (gi + 1) * blk
        cur = x_ref[:, lo:hi].astype(F32)
        full = jnp.concatenate([tail_scr[:, lo:hi], cur], axis=0)
        w = cw_ref[:, lo:hi]
        acc = full * w[CONV_W - 1:CONV_W]
        for i in range(CONV_W - 1):
            acc = acc + pltpu.roll(full, CONV_W - 1 - i, axis=0) * w[i:i + 1]
        xc = acc[SUBLANES:] + cb_ref[:, lo:hi]
        tail_scr[:, lo:hi] = cur[ts - SUBLANES:]
        xb = xc.astype(BF16)
        r = _sigmoid(_dot(xb, wra_ref[gi]) + bra_ref[:, lo:hi])
        ig = _sigmoid(_dot(xb, wri_ref[gi]) + bri_ref[:, lo:hi])
        log_a = -LRU_C * r * _softplus(-lam_ref[:, lo:hi])
        a = jnp.exp(log_a)
        a_scr[:, lo:hi] = a
        u_scr[:, lo:hi] = jnp.sqrt(1.0 - a * a) * (ig * xc)

    row = lax.broadcasted_iota(jnp.int32, (SUBLANES, width), 0)

    def scan_body(rb, h_prev):
        r0 = pl.multiple_of(rb * SUBLANES, SUBLANES)
        a = a_scr[pl.ds(r0, SUBLANES), :]
        u = u_scr[pl.ds(r0, SUBLANES), :]
        for d in (1, 2, 4):
            a_sh = jnp.where(row >= d, pltpu.roll(a, d, axis=0), 1.0)
            u_sh = jnp.where(row >= d, pltpu.roll(u, d, axis=0), 0.0)
            u = a * u_sh + u
            a = a * a_sh
        h = u + a * h_prev
        gate = g_ref[pl.ds(r0, SUBLANES), :].astype(F32)
        u_scr[pl.ds(r0, SUBLANES), :] = _gelu_tanh(gate) * h
        return h[SUBLANES - 1:SUBLANES, :]

    h_last = lax.fori_loop(0, ts // SUBLANES, scan_body, h_scr[...])
    h_scr[...] = h_last
    o_ref[...] = u_scr[...].astype(o_ref.dtype)


def _lru(p, conv_w, conv_b, w_ra, b_ra, w_ri, b_ri, lam, bn, seq):
    t = p.shape[0]
    width = p.shape[1] // 2
    blk = width // LRU_BLOCKS
    ts = 256
    per_b = seq // ts
    tok = lambda b, s: b * per_b + s
    vec = lambda a: a.reshape(1, width).astype(F32)
    vspec = pl.BlockSpec((1, width), lambda b, s: (0, 0))
    wspec = pl.BlockSpec((LRU_BLOCKS, blk, blk), lambda b, s: (0, 0, 0))
    return pl.pallas_call(
        functools.partial(_lru_kernel, ts=ts),
        grid=(bn, per_b),
        in_specs=[
            pl.BlockSpec((ts, width), lambda b, s: (tok(b, s), 0)),
            pl.BlockSpec((ts, width), lambda b, s: (tok(b, s), 1)),
            pl.BlockSpec((CONV_W, width), lambda b, s: (0, 0)),
            vspec, wspec, vspec, wspec, vspec, vspec,
        ],
        out_specs=pl.BlockSpec((ts, width), lambda b, s: (tok(b, s), 0)),
        out_shape=jax.ShapeDtypeStruct((t, width), BF16),
        scratch_shapes=[
            pltpu.VMEM((SUBLANES, width), F32),
            pltpu.VMEM((1, width), F32),
            pltpu.VMEM((ts, width), F32),
            pltpu.VMEM((ts, width), F32),
        ],
        compiler_params=_params("arbitrary", "arbitrary"),
        name="lru",
    )(p, p, conv_w, vec(conv_b), w_ra.astype(BF16), vec(b_ra), w_ri.astype(BF16), vec(b_ri), vec(lam))


def _pack_bf16_pairs(h):
    half = h.shape[1] // 2
    lo = lax.bitcast_convert_type(h[:, :half].astype(BF16).astype(F32), jnp.uint32)
    hi = lax.bitcast_convert_type(h[:, half:].astype(BF16).astype(F32), jnp.uint32)
    return (lo >> 16) | (hi & jnp.uint32(0xFFFF0000))


def _unpack_bf16_pairs(w):
    lo = lax.bitcast_convert_type(w << 16, F32).astype(BF16)
    hi = lax.bitcast_convert_type(w & jnp.uint32(0xFFFF0000), F32).astype(BF16)
    return lo, hi


def _out_proj_kernel(*refs, n_in):
    a_refs = refs[:n_in]
    w_refs = refs[n_in:2 * n_in]
    (x_ref, mod_ref, ln_ref, rwh_ref, rwl_ref, rb_ref,
     xo_ref, hp_ref, ti_ref, tw_ref, tr_ref, cnt_ref, cnt_scr) = refs[2 * n_in:]

    @pl.when(pl.program_id(0) == 0)
    def _():
        cnt_scr[...] = jnp.zeros_like(cnt_scr)

    y = _dot(a_refs[0][...], w_refs[0][...])
    for a_ref, w_ref in zip(a_refs[1:], w_refs[1:]):
        y = y + _dot(a_ref[...], w_ref[...])
    m = mod_ref[0]
    xn = x_ref[...] + m[2:3] * y
    xo_ref[...] = xn
    h = _norm_mod(xn, ln_ref[...], m[3:4], m[4:5])
    hp_ref[...] = _pack_bf16_pairs(h).reshape(hp_ref.shape)
    h_hi = h.astype(BF16)
    h_lo = (h - h_hi.astype(F32)).astype(BF16)
    logits = _dot(h_hi, rwh_ref[...]) + (_dot(h_hi, rwl_ref[...]) + _dot(h_lo, rwh_ref[...])) + rb_ref[...]
    lane = lax.broadcasted_iota(jnp.int32, logits.shape, 1)
    cur = jnp.where(lane < N_EXPERTS, logits, -jnp.inf)
    vals, idxs = [], []
    for _ in range(TOP_K):
        mx = jnp.max(cur, axis=-1, keepdims=True)
        ix = jnp.min(jnp.where(cur == mx, lane, LANES), axis=-1, keepdims=True)
        vals.append(mx)
        idxs.append(ix)
        cur = jnp.where(lane == ix, -jnp.inf, cur)
    es = [jnp.exp(v - vals[0]) for v in vals]
    den = es[0]
    for e in es[1:]:
        den = den + e
    tm = logits.shape[0]
    ri = lax.broadcasted_iota(jnp.int32, (tm, tm), 0)
    rj = lax.broadcasted_iota(jnp.int32, (tm, tm), 1)
    tri = jnp.where(rj < ri, 1.0, 0.0).astype(BF16)
    base = cnt_scr[...]
    ti = jnp.zeros(logits.shape, jnp.int32)
    tw = jnp.zeros(logits.shape, F32)
    tr = jnp.zeros(logits.shape, jnp.int32)
    for kk in range(TOP_K):
        sel = lane == idxs[kk]
        onehot = jnp.where(sel, 1.0, 0.0)
        before = _dot(tri, onehot.astype(BF16)) + base
        rank = jnp.sum(jnp.where(sel, before, 0.0), axis=-1, keepdims=True)
        base = base + jnp.sum(onehot, axis=0, keepdims=True)
        ti = jnp.where(lane == kk, idxs[kk], ti)
        tw = jnp.where(lane == kk, es[kk] / den, tw)
        tr = jnp.where(lane == kk, rank.astype(jnp.int32), tr)
    cnt_scr[...] = base
    ti_ref[...] = ti
    tw_ref[...] = tw
    tr_ref[...] = tr
    cnt_ref[...] = base.astype(jnp.int32)


def _out_proj(acts, weights, x, mod, ln_ffn, router_w, router_b, seq):
    t, d = x.shape
    tm = 256
    per_b = seq // tm
    n_in = len(acts)
    rw = jnp.zeros((d, LANES), F32).at[:, :N_EXPERTS].set(router_w.astype(F32))
    rw_hi = rw.astype(BF16)
    rw_lo = (rw - rw_hi.astype(F32)).astype(BF16)
    rb = jnp.zeros((1, LANES), F32).at[0, :N_EXPERTS].set(router_b.astype(F32))
    tiles = d // 2 // LANES
    in_specs = [pl.BlockSpec((tm, a.shape[1]), lambda i: (i, 0)) for a in acts]
    in_specs += [pl.BlockSpec(w.shape, lambda i: (0, 0)) for w in weights]
    in_specs += [
        pl.BlockSpec((tm, d), lambda i: (i, 0)),
        pl.BlockSpec((1, 6, d), lambda i: (i // per_b, 0, 0)),
        pl.BlockSpec((1, d), lambda i: (0, 0)),
        pl.BlockSpec((d, LANES), lambda i: (0, 0)),
        pl.BlockSpec((d, LANES), lambda i: (0, 0)),
        pl.BlockSpec((1, LANES), lambda i: (0, 0)),
    ]
    row_d = pl.BlockSpec((tm, d), lambda i: (i, 0))
    row_h = pl.BlockSpec((tm, tiles, LANES), lambda i: (i, 0, 0))
    row_l = pl.BlockSpec((tm, LANES), lambda i: (i, 0))
    return pl.pallas_call(
        functools.partial(_out_proj_kernel, n_in=n_in),
        grid=(t // tm,),
        in_specs=in_specs,
        out_specs=[row_d, row_h, row_l, row_l, row_l, pl.BlockSpec((1, LANES), lambda i: (0, 0))],
        out_shape=[
            jax.ShapeDtypeStruct((t, d), F32),
            jax.ShapeDtypeStruct((t, tiles, LANES), jnp.uint32),
            jax.ShapeDtypeStruct((t, LANES), jnp.int32),
            jax.ShapeDtypeStruct((t, LANES), F32),
            jax.ShapeDtypeStruct((t, LANES), jnp.int32),
            jax.ShapeDtypeStruct((1, LANES), jnp.int32),
        ],
        scratch_shapes=[pltpu.VMEM((1, LANES), F32)],
        compiler_params=_params("arbitrary"),
        name="out_proj",
    )(*acts, *weights, x, mod, ln_ffn.reshape(1, d), rw_hi, rw_lo, rb)


def _moe_routing(ti, tr, cnt, n_tok):
    n_blk = (n_tok * TOP_K) // MOE_BLOCK + N_EXPERTS
    counts = cnt[0, :N_EXPERTS]
    padded = ((counts + MOE_BLOCK - 1) // MOE_BLOCK) * MOE_BLOCK
    pend = jnp.cumsum(padded)
    pstart = pend - padded
    e = ti[:, :TOP_K]
    hit = e[:, :, None] == jnp.arange(N_EXPERTS, dtype=jnp.int32)
    dest = tr[:, :TOP_K] + jnp.sum(jnp.where(hit, pstart, 0), axis=-1)
    blk_start = jnp.arange(n_blk, dtype=jnp.int32) * MOE_BLOCK
    blk_e = jnp.minimum(jnp.sum(pend[None, :] <= blk_start[:, None], axis=1), N_EXPERTS - 1).astype(jnp.int32)
    i = jnp.arange(MOE_BLOCK, dtype=jnp.int32)
    tail = (pstart + counts)[:, None] + i[None, :]
    is_tail = i[None, :] < (padded - counts)[:, None]
    spare = (jnp.cumsum(jnp.logical_not(is_tail).reshape(-1).astype(jnp.int32)) - 1).reshape(is_tail.shape)
    pad_slots = jnp.where(is_tail, tail, pend[-1] + spare).reshape(-1)
    n_used = (pend[-1:] // MOE_BLOCK).astype(jnp.int32)
    return dest.astype(jnp.int32), pad_slots.astype(jnp.int32), blk_e, n_used


DISPATCH_TOKENS = 512
COMBINE_TOKENS = 256
ISSUE_TOKENS = 4
ISSUE_ROWS = ISSUE_TOKENS * TOP_K


def _dispatch_kernel(dest_ref, pad_ref, hp_ref, xs_hbm, zero_scr, sem):
    n_pad = pad_ref.shape[2]
    zero_scr[...] = jnp.zeros_like(zero_scr)

    def issue(g, c):
        t0 = g * ISSUE_TOKENS
        dsts = [dest_ref[0, 0, t0 * TOP_K + j] for j in range(ISSUE_ROWS)]
        for j, dst in enumerate(dsts):
            pltpu.make_async_copy(hp_ref.at[t0 + j // TOP_K], xs_hbm.at[dst], sem.at[0]).start(priority=j % 2)
        return c

    lax.fori_loop(0, DISPATCH_TOKENS // ISSUE_TOKENS, issue, 0)

    def issue_zero(g, c):
        dsts = [pad_ref[0, 0, g * ISSUE_ROWS + j] for j in range(ISSUE_ROWS)]
        for j, dst in enumerate(dsts):
            pltpu.make_async_copy(zero_scr, xs_hbm.at[dst], sem.at[0]).start(priority=j % 2)
        return c

    lax.fori_loop(0, n_pad // ISSUE_ROWS, issue_zero, 0)

    rows = DISPATCH_TOKENS * TOP_K + n_pad
    for _ in range(rows // DISPATCH_TOKENS):
        pltpu.make_async_copy(hp_ref, xs_hbm.at[pl.ds(0, DISPATCH_TOKENS)], sem.at[0]).wait()
    rem = rows % DISPATCH_TOKENS
    if rem:
        pltpu.make_async_copy(hp_ref.at[pl.ds(0, rem)], xs_hbm.at[pl.ds(0, rem)], sem.at[0]).wait()


def _dispatch(hp, dest, pad_slots, n_buf):
    t = hp.shape[0]
    n_steps = t // DISPATCH_TOKENS
    n_pad = pad_slots.shape[0] // n_steps
    assert n_pad * n_steps == pad_slots.shape[0] and n_pad % ISSUE_ROWS == 0
    smem = lambda n: pl.BlockSpec((1, 1, n), lambda i: (i, 0, 0), memory_space=pltpu.SMEM)
    return pl.pallas_call(
        _dispatch_kernel,
        grid=(n_steps,),
        in_specs=[
            smem(DISPATCH_TOKENS * TOP_K),
            smem(n_pad),
            pl.BlockSpec((DISPATCH_TOKENS,) + hp.shape[1:], lambda i: (i, 0, 0)),
        ],
        out_specs=pl.BlockSpec(memory_space=pl.ANY),
        out_shape=jax.ShapeDtypeStruct((n_buf,) + hp.shape[1:], jnp.uint32),
        scratch_shapes=[pltpu.VMEM(hp.shape[1:], jnp.uint32), pltpu.SemaphoreType.DMA((1,))],
        compiler_params=_params("arbitrary"),
        name="moe_dispatch",
    )(dest.reshape(n_steps, 1, DISPATCH_TOKENS * TOP_K), pad_slots.reshape(n_steps, 1, n_pad), hp)


def _moe_kernel(blk_e_ref, n_used_ref, xs_ref, wgu_ref, bgu_ref, wdn_ref, bdn_ref, y_ref):
    del blk_e_ref
    half = xs_ref.shape[1] * xs_ref.shape[2]
    d_ff = wdn_ref.shape[1]
    used = pl.program_id(0) < n_used_ref[0]

    @pl.when(used)
    def _():
        lo, hi = _unpack_bf16_pairs(xs_ref[...].reshape(MOE_BLOCK, half))
        gu = _dot(lo, wgu_ref[0, :half, :]) + _dot(hi, wgu_ref[0, half:, :]) + bgu_ref[0]
        gate = jnp.minimum(gu[:, :d_ff], SWIGLU_LIMIT)
        up = jnp.clip(gu[:, d_ff:], -SWIGLU_LIMIT, SWIGLU_LIMIT)
        act = (up + 1.0) * gate * _sigmoid(SWIGLU_ALPHA * gate)
        y = _dot(act.astype(BF16), wdn_ref[0]) + bdn_ref[0]
        y_ref[...] = y.reshape(y_ref.shape)

    @pl.when(jnp.logical_not(used))
    def _():
        y_ref[...] = jnp.zeros_like(y_ref)


def _moe_experts(xs, blk_e, n_used, w_gu, b_gu, w_dn, b_dn):
    n_buf, tiles, _ = xs.shape
    d = 2 * tiles * LANES
    d_ff = w_dn.shape[1]
    n_w = w_gu.shape[0]
    grid_spec = pltpu.PrefetchScalarGridSpec(
        num_scalar_prefetch=2,
        grid=(n_buf // MOE_BLOCK,),
        in_specs=[
            pl.BlockSpec((MOE_BLOCK, tiles, LANES), lambda b, e, u: (b, 0, 0)),
            pl.BlockSpec((1, d, 2 * d_ff), lambda b, e, u: (e[b], 0, 0)),
            pl.BlockSpec((1, 1, 2 * d_ff), lambda b, e, u: (e[b], 0, 0)),
            pl.BlockSpec((1, d_ff, d), lambda b, e, u: (e[b], 0, 0)),
            pl.BlockSpec((1, 1, d), lambda b, e, u: (e[b], 0, 0)),
        ],
        out_specs=pl.BlockSpec((MOE_BLOCK, d // LANES, LANES), lambda b, e, u: (b, 0, 0)),
    )
    return pl.pallas_call(
        _moe_kernel,
        grid_spec=grid_spec,
        out_shape=jax.ShapeDtypeStruct((n_buf, d // LANES, LANES), F32),
        compiler_params=_params("arbitrary"),
        name="moe_experts",
    )(blk_e, n_used, xs, w_gu, b_gu.reshape(n_w, 1, 2 * d_ff), w_dn, b_dn.reshape(n_w, 1, d))


def _combine_kernel(dest_ref, dest_next_ref, x_ref, tw_ref, mod_ref, lnf_ref, y_hbm, o_ref, ybuf, sem, *, final):
    i = pl.program_id(0)
    n = pl.num_programs(0)
    slot = i % 2
    tm = COMBINE_TOKENS

    def issue(ids_ref, s):
        def body(g, c):
            t0 = g * ISSUE_TOKENS
            srcs = [ids_ref[0, 0, t0 * TOP_K + j] for j in range(ISSUE_TOKENS * TOP_K)]
            for j, src in enumerate(srcs):
                pltpu.make_async_copy(y_hbm.at[src], ybuf.at[s, (j % TOP_K) * tm + t0 + j // TOP_K],
                                      sem.at[s]).start(priority=j % 2)
            return c
        lax.fori_loop(0, tm // ISSUE_TOKENS, body, 0)

    @pl.when(i == 0)
    def _():
        issue(dest_ref, 0)

    @pl.when(i + 1 < n)
    def _():
        issue(dest_next_ref, 1 - slot)

    pltpu.make_async_copy(y_hbm.at[pl.ds(0, TOP_K * tm)], ybuf.at[slot], sem.at[slot]).wait()
    tw = tw_ref[...]
    acc = None
    for kk in range(TOP_K):
        gate = jnp.broadcast_to(tw[:, kk:kk + 1], (tm, LANES)).reshape(tm, 1, LANES)
        term = gate * ybuf[slot, kk * tm:(kk + 1) * tm]
        acc = term if acc is None else acc + term
    xn = x_ref[...] + mod_ref[0][5:6] * acc.reshape(x_ref.shape)
    if final:
        xn = xn * lax.rsqrt(jnp.mean(xn * xn, axis=-1, keepdims=True) + EPS) * lnf_ref[...]
    o_ref[...] = xn


def _combine(x, y, dest, tw, mod, ln_final, seq, final):
    t, d = x.shape
    tm = COMBINE_TOKENS
    per_b = seq // tm
    nt = t // tm
    ids = dest.reshape(nt, 1, tm * TOP_K)
    id_spec = lambda f: pl.BlockSpec((1, 1, tm * TOP_K), f, memory_space=pltpu.SMEM)
    return pl.pallas_call(
        functools.partial(_combine_kernel, final=final),
        grid=(nt,),
        in_specs=[
            id_spec(lambda i: (i, 0, 0)),
            id_spec(lambda i: (jnp.minimum(i + 1, nt - 1), 0, 0)),
            pl.BlockSpec((tm, d), lambda i: (i, 0)),
            pl.BlockSpec((tm, LANES), lambda i: (i, 0)),
            pl.BlockSpec((1, 6, d), lambda i: (i // per_b, 0, 0)),
            pl.BlockSpec((1, d), lambda i: (0, 0)),
            pl.BlockSpec(memory_space=pl.ANY),
        ],
        out_specs=pl.BlockSpec((tm, d), lambda i: (i, 0)),
        out_shape=jax.ShapeDtypeStruct((t, d), F32),
        scratch_shapes=[pltpu.VMEM((2, TOP_K * tm, d // LANES, LANES), F32), pltpu.SemaphoreType.DMA((2,))],
        compiler_params=_params("arbitrary"),
        name="moe_combine",
    )(ids, ids, x, tw, mod, ln_final.reshape(1, d), y)


def _ffn(x, hp, ti, tw, tr, cnt, mod, layer, w_gu, b_gu, w_dn, b_dn, ln_final, seq, final):
    t = x.shape[0]
    dest, pad_slots, blk_e, n_used = _moe_routing(ti, tr, cnt, t)
    n_buf = blk_e.shape[0] * MOE_BLOCK
    xs = _dispatch(hp, dest, pad_slots, n_buf)
    y = _moe_experts(xs, blk_e + layer * N_EXPERTS, n_used, w_gu, b_gu, w_dn, b_dn)
    return _combine(x, y, dest, tw, mod, ln_final, seq, final)


def _even_inproj_weights(w_in):
    gdn_cols = 4 * GDN_HEADS * GDN_DK
    small0 = gdn_cols
    gla0 = small0 + 2 * GDN_HEADS
    gla_cols = 2 * GLA_HEADS * GLA_DK + 2 * GLA_HEADS * GLA_DV
    lg0 = gla0 + gla_cols
    w_big = jnp.concatenate([w_in[:, :gdn_cols], w_in[:, gla0:lg0]], axis=1).astype(BF16)
    w_small = jnp.zeros((w_in.shape[0], LANES), F32)
    w_small = w_small.at[:, :2 * GDN_HEADS].set(w_in[:, small0:gla0])
    w_small = w_small.at[:, 2 * GDN_HEADS:2 * GDN_HEADS + GLA_RANK].set(w_in[:, lg0:lg0 + GLA_RANK])
    return w_big, w_small.astype(BF16)


def _even_mixer(x, mod, ln_mix, w_in, conv_w, a_log, dt_bias, gdn_norm, w_a2, b_a2, gla_norm, bn, seq):
    w_big, w_small = _even_inproj_weights(w_in)
    p, small = _norm_proj(x, ln_mix, mod, w_big, w_small, seq)
    col, row = _gdn_gates(small, a_log, dt_bias)
    o_a = _gdn(p, col, row, conv_w, gdn_norm, bn, seq)
    o_b = _gla(p, small, w_a2, b_a2, gla_norm, bn, seq)
    return o_a, o_b


def kernel(x, c, ln_mix, ln_ffn, ln_final, w_ada, b_ada, e_w_in, e_conv_w, e_a_log, e_dt_bias, e_gdn_norm, e_gla_w_a2, e_gla_b_a2, e_gla_norm, e_w_out, o_w_in, o_conv_w, o_conv_b, o_w_ra, o_b_ra, o_w_ri, o_b_ri, o_lam, o_w_out, router_w, router_b, exp_w_gu, exp_b_gu, exp_w_dn, exp_b_dn):
    bn, seq, d = x.shape
    depth = w_ada.shape[0]
    mods = _ada_mod(c, w_ada, b_ada)
    xf = x.reshape(bn * seq, d)
    w_gu = exp_w_gu.astype(BF16).reshape((-1,) + exp_w_gu.shape[2:])
    w_dn = exp_w_dn.astype(BF16).reshape((-1,) + exp_w_dn.shape[2:])
    b_gu = exp_b_gu.reshape((-1,) + exp_b_gu.shape[2:])
    b_dn = exp_b_dn.reshape((-1,) + exp_b_dn.shape[2:])
    for layer in range(depth):
        mod = mods[layer]
        j = layer // 2
        if layer % 2 == 0:
            o_a, o_b = _even_mixer(xf, mod, ln_mix[layer], e_w_in[j], e_conv_w[j], e_a_log[j], e_dt_bias[j],
                                   e_gdn_norm[j], e_gla_w_a2[j], e_gla_b_a2[j], e_gla_norm[j], bn, seq)
            split = o_a.shape[1]
            acts = [o_a, o_b]
            weights = [e_w_out[j][:split].astype(BF16), e_w_out[j][split:].astype(BF16)]
        else:
            p = _norm_proj(xf, ln_mix[layer], mod, o_w_in[j].astype(BF16), None, seq)
            acts = [_lru(p, o_conv_w[j], o_conv_b[j], o_w_ra[j], o_b_ra[j], o_w_ri[j], o_b_ri[j], o_lam[j], bn, seq)]
            weights = [o_w_out[j].astype(BF16)]
        xf, hp, ti, tw, tr, cnt = _out_proj(acts, weights, xf, mod, ln_ffn[layer], router_w[layer], router_b[layer], seq)
        xf = _ffn(xf, hp, ti, tw, tr, cnt, mod, layer, w_gu, b_gu, w_dn, b_dn,
                  ln_final, seq, final=(layer == depth - 1))
    return xf.reshape(bn, seq, d)
```

```python
import functools

import jax
import jax.numpy as jnp
from jax import lax
from jax.experimental import pallas as pl
from jax.experimental.pallas import tpu as pltpu

F32 = jnp.float32
BF16 = jnp.bfloat16
HIGHEST = lax.Precision.HIGHEST

EPS = 1e-6
CONV_W = 4
CHUNK = 64
GDN_HEADS = 8
GDN_DK = 128
GDN_DV = 128
GLA_HEADS = 8
GLA_DK = 64
GLA_DV = 128
GLA_RANK = 16
GLA_GATE_NORM = 16.0
LRU_BLOCKS = 8
LRU_C = 8.0
N_EXPERTS = 32
TOP_K = 4
SWIGLU_LIMIT = 7.0
SWIGLU_ALPHA = 1.702
MOE_BLOCK = 512

LANES = 128
SUBLANES = 8
VMEM_LIMIT = 56 * 1024 * 1024


def _dot(a, b, precision=None):
    return jnp.dot(a, b, preferred_element_type=F32, precision=precision)


def _dot_nt(a, b, precision=None):
    return lax.dot_general(a, b, (((1,), (1,)), ((), ())), preferred_element_type=F32, precision=precision)


def _dot_tn(a, b):
    return lax.dot_general(a, b, (((0,), (0,)), ((), ())), preferred_element_type=F32)


def _dot_split(a, b):
    a_hi = a.astype(BF16)
    b_hi = b.astype(BF16)
    a_lo = (a - a_hi.astype(F32)).astype(BF16)
    b_lo = (b - b_hi.astype(F32)).astype(BF16)
    return _dot(a_hi, b_hi) + (_dot(a_hi, b_lo) + _dot(a_lo, b_hi))


def _sigmoid(x):
    return 1.0 / (1.0 + jnp.exp(-x))


def _softplus(x):
    return jnp.maximum(x, 0.0) + jnp.log(1.0 + jnp.exp(-jnp.abs(x)))


def _silu(x):
    return x * _sigmoid(x)


def _params(*sem):
    return pltpu.CompilerParams(dimension_semantics=sem, vmem_limit_bytes=VMEM_LIMIT)


def _ada_kernel(c_ref, w_ref, b_ref, o_ref):
    c = c_ref[...]
    o_ref[0] = _dot(_silu(c), w_ref[0], precision=HIGHEST) + b_ref[0]


def _ada_mod(c, w_ada, b_ada):
    depth, d, n6 = w_ada.shape
    bn = c.shape[0]
    rows = -(-bn // SUBLANES) * SUBLANES
    cp = jnp.zeros((rows, d), F32).at[:bn].set(c)
    tn = 768
    out = pl.pallas_call(
        _ada_kernel,
        grid=(depth, n6 // tn),
        in_specs=[
            pl.BlockSpec((rows, d), lambda l, j: (0, 0)),
            pl.BlockSpec((1, d, tn), lambda l, j: (l, 0, j)),
            pl.BlockSpec((1, 1, tn), lambda l, j: (l, 0, j)),
        ],
        out_specs=pl.BlockSpec((1, rows, tn), lambda l, j: (l, 0, j)),
        out_shape=jax.ShapeDtypeStruct((depth, rows, n6), F32),
        compiler_params=_params("arbitrary", "arbitrary"),
        name="ada_mod",
    )(cp, w_ada, b_ada.reshape(depth, 1, n6))
    return out[:, :bn].reshape(depth, bn, 6, d)


def _norm_mod(x, ln, shift, scale):
    y = x * lax.rsqrt(jnp.mean(x * x, axis=-1, keepdims=True) + EPS) * ln
    return y * (1.0 + scale) + shift


def _norm_proj_kernel(x_ref, ln_ref, mod_ref, w_ref, *rest, has_small):
    if has_small:
        ws_ref, o_ref, os_ref, h_scr = rest
    else:
        o_ref, h_scr = rest
    j = pl.program_id(1)

    @pl.when(j == 0)
    def _():
        m = mod_ref[0]
        h = _norm_mod(x_ref[...], ln_ref[...], m[0:1], m[1:2]).astype(BF16)
        h_scr[...] = h
        if has_small:
            os_ref[...] = _dot(h, ws_ref[...])

    o_ref[...] = _dot(h_scr[...], w_ref[...]).astype(o_ref.dtype)


def _norm_proj(x, ln, mod, w, w_small, seq):
    t, d = x.shape
    n = w.shape[1]
    tm = 512
    tn = max(c for c in range(LANES, 2048 + 1, LANES) if n % c == 0)
    per_b = seq // tm
    has_small = w_small is not None
    in_specs = [
        pl.BlockSpec((tm, d), lambda i, j: (i, 0)),
        pl.BlockSpec((1, d), lambda i, j: (0, 0)),
        pl.BlockSpec((1, 6, d), lambda i, j: (i // per_b, 0, 0)),
        pl.BlockSpec((d, tn), lambda i, j: (0, j)),
    ]
    out_specs = [pl.BlockSpec((tm, tn), lambda i, j: (i, j))]
    out_shape = [jax.ShapeDtypeStruct((t, n), BF16)]
    args = [x, ln.reshape(1, d), mod, w]
    if has_small:
        in_specs.append(pl.BlockSpec((d, LANES), lambda i, j: (0, 0)))
        out_specs.append(pl.BlockSpec((tm, LANES), lambda i, j: (i, 0)))
        out_shape.append(jax.ShapeDtypeStruct((t, LANES), F32))
        args.append(w_small)
    outs = pl.pallas_call(
        functools.partial(_norm_proj_kernel, has_small=has_small),
        grid=(t // tm, n // tn),
        in_specs=in_specs,
        out_specs=out_specs,
        out_shape=out_shape,
        scratch_shapes=[pltpu.VMEM((tm, d), BF16)],
        compiler_params=_params("arbitrary", "arbitrary"),
        name="norm_proj",
    )(*args)
    return outs if has_small else outs[0]


def _gdn_gate_kernel(ga_ref, gb_ref, alog_ref, dtb_ref, gc_ref, beta_ref, eg_ref, ek_ref):
    g = -jnp.exp(alog_ref[...]) * _softplus(ga_ref[...] + dtb_ref[...])
    ci = lax.broadcasted_iota(jnp.int32, (LANES, LANES), 0)
    cj = lax.broadcasted_iota(jnp.int32, (LANES, LANES), 1)
    same = (ci // CHUNK) == (cj // CHUNK)
    upper = jnp.where(same & (ci <= cj), 1.0, 0.0).astype(F32)
    allc = jnp.where(same, 1.0, 0.0).astype(F32)
    gc = _dot(g, upper, precision=HIGHEST)
    g_last = _dot(g, allc, precision=HIGHEST)
    gc_ref[...] = gc
    beta_ref[...] = _sigmoid(gb_ref[...])
    eg_ref[...] = jnp.exp(gc)
    ek_ref[...] = jnp.exp(g_last - gc)


def _gdn_gates(small, a_log, dt_bias):
    t = small.shape[0]
    nb = t // LANES
    rows = GDN_HEADS * nb
    gb_t = small[:, 0:GDN_HEADS].T.reshape(rows, LANES)
    ga_t = small[:, GDN_HEADS:2 * GDN_HEADS].T.reshape(rows, LANES)
    alog_c = jnp.repeat(a_log.astype(F32), nb).reshape(rows, 1)
    dtb_c = jnp.repeat(dt_bias.astype(F32), nb).reshape(rows, 1)
    full = pl.BlockSpec((rows, LANES), lambda: (0, 0))
    col1 = pl.BlockSpec((rows, 1), lambda: (0, 0))
    gc, beta, eg, ek = pl.pallas_call(
        _gdn_gate_kernel,
        in_specs=[full, full, col1, col1],
        out_specs=[full] * 4,
        out_shape=[jax.ShapeDtypeStruct((rows, LANES), F32)] * 4,
        compiler_params=pltpu.CompilerParams(vmem_limit_bytes=VMEM_LIMIT),
        name="gdn_gates",
    )(ga_t, gb_t, alog_c, dtb_c)
    to_col = lambda a: a.reshape(GDN_HEADS, t).T
    col = jnp.concatenate([to_col(beta), to_col(gc), to_col(eg), to_col(ek)], axis=1)
    row = gc.reshape(GDN_HEADS, t // CHUNK, CHUNK).transpose(1, 0, 2)
    return col, row


def _cast_specs(w, layer, n_steps, step_of):
    _, n_exp, rows, width = w.shape
    per_step = n_exp * rows // n_steps
    assert per_step * n_steps == n_exp * rows
    if per_step >= rows:
        assert per_step % rows == 0
        e_blk, r_blk = per_step // rows, rows
    else:
        assert rows % per_step == 0
        e_blk, r_blk = 1, per_step
    r_steps = rows // r_blk
    in_spec = pl.BlockSpec((1, e_blk, r_blk, width),
                           lambda *g: (layer, step_of(*g) // r_steps, step_of(*g) % r_steps, 0))
    out_spec = pl.BlockSpec((e_blk, r_blk, width), lambda *g: (step_of(*g) // r_steps, step_of(*g) % r_steps, 0))
    return in_spec, out_spec, jax.ShapeDtypeStruct((n_exp, rows, width), BF16)


def _cast_blocks(in_refs, out_refs):
    for i_ref, o_ref in zip(in_refs, out_refs):
        o_ref[...] = i_ref[0].astype(o_ref.dtype)


def _gdn_kernel(q_ref, k_ref, v_ref, z_ref, col_ref, row_ref, cw_ref, nw_ref, *rest, ts, n_cast):
    cast_in = rest[:n_cast]
    o_ref = rest[n_cast]
    cast_out = rest[n_cast + 1:2 * n_cast + 1]
    s_scr, tail_scr, qkv_scr = rest[2 * n_cast + 1:]
    width = GDN_HEADS * GDN_DK

    @pl.when(pl.program_id(1) == 0)
    def _():
        s_scr[...] = jnp.zeros_like(s_scr)
        tail_scr[...] = jnp.zeros_like(tail_scr)

    strip = 256
    for part, ref in enumerate((q_ref, k_ref, v_ref)):
        for s0 in range(0, width, strip):
            c0 = part * width + s0
            cur = ref[:, s0:s0 + strip].astype(F32)
            full = jnp.concatenate([tail_scr[:, c0:c0 + strip], cur], axis=0)
            w = cw_ref[:, c0:c0 + strip]
            acc = full * w[CONV_W - 1:CONV_W]
            for i in range(CONV_W - 1):
                acc = acc + pltpu.roll(full, CONV_W - 1 - i, axis=0) * w[i:i + 1]
            y = acc[SUBLANES:]
            qkv_scr[:, c0:c0 + strip] = _silu(y)
            tail_scr[:, c0:c0 + strip] = cur[ts - SUBLANES:]

    ri = lax.broadcasted_iota(jnp.int32, (CHUNK, CHUNK), 0)
    rj = lax.broadcasted_iota(jnp.int32, (CHUNK, CHUNK), 1)
    incl = ri >= rj
    strict = ri > rj
    eye = jnp.where(ri == rj, 1.0, 0.0).astype(F32)
    nw = nw_ref[...]

    def chunk_body(c, carry):
        r0 = pl.multiple_of(c * CHUNK, CHUNK)
        colc = col_ref[pl.ds(r0, CHUNK), :]
        rowc = row_ref[c]
        heads = range(GDN_HEADS)
        sl = [slice(h * GDN_DK, (h + 1) * GDN_DK) for h in heads]
        q = [qkv_scr[pl.ds(r0, CHUNK), sl[h]] for h in heads]
        k = [qkv_scr[pl.ds(r0, CHUNK), width + h * GDN_DK:width + (h + 1) * GDN_DK] for h in heads]
        v = [qkv_scr[pl.ds(r0, CHUNK), 2 * width + h * GDN_DK:2 * width + (h + 1) * GDN_DK] for h in heads]
        q = [a * lax.rsqrt(jnp.sum(a * a, axis=-1, keepdims=True) + EPS) * (GDN_DK ** -0.5) for a in q]
        k = [a * lax.rsqrt(jnp.sum(a * a, axis=-1, keepdims=True) + EPS) for a in k]
        beta = [colc[:, h:h + 1] for h in heads]
        gc = [colc[:, 8 + h:9 + h] for h in heads]
        eg = [colc[:, 16 + h:17 + h] for h in heads]
        ek = [colc[:, 24 + h:25 + h] for h in heads]
        decay = [jnp.exp(jnp.where(incl, gc[h] - rowc[h:h + 1, :], -jnp.inf)) for h in heads]
        kb = [a.astype(BF16) for a in k]
        qb = [a.astype(BF16) for a in q]
        kk = [_dot_nt(kb[h], kb[h]) for h in heads]
        qk = [_dot_nt(qb[h], kb[h]) for h in heads]
        npow = [jnp.where(strict, -(beta[h] * kk[h] * decay[h]), 0.0) for h in heads]
        ainv = [eye + npow[h] for h in heads]
        for _ in range(5):
            npow = [_dot_split(npow[h], npow[h]) for h in heads]
            ainv = [ainv[h] + _dot_split(ainv[h], npow[h]) for h in heads]
        ab = [a.astype(BF16) for a in ainv]
        u = [_dot(ab[h], (v[h] * beta[h]).astype(BF16)) for h in heads]
        w = [_dot(ab[h], (k[h] * (beta[h] * eg[h])).astype(BF16)) for h in heads]
        qk = [jnp.where(incl, qk[h] * decay[h], 0.0).astype(BF16) for h in heads]
        q_s = [(q[h] * eg[h]).astype(BF16) for h in heads]
        k_s = [(k[h] * ek[h]).astype(BF16) for h in heads]
        state = [s_scr[h] for h in heads]
        sb = [a.astype(BF16) for a in state]
        v_new = [(u[h] - _dot(w[h].astype(BF16), sb[h])).astype(BF16) for h in heads]
        o = [_dot(q_s[h], sb[h]) + _dot(qk[h], v_new[h]) for h in heads]
        for h in heads:
            s_scr[h] = state[h] * eg[h][CHUNK - 1:CHUNK, :] + _dot_tn(k_s[h], v_new[h])
        for h in heads:
            oh = o[h] * lax.rsqrt(jnp.mean(o[h] * o[h], axis=-1, keepdims=True) + EPS) * nw
            z = z_ref[pl.ds(r0, CHUNK), sl[h]].astype(F32)
            o_ref[pl.ds(r0, CHUNK), sl[h]] = (oh * _silu(z)).astype(o_ref.dtype)
        return carry

    lax.fori_loop(0, ts // CHUNK, chunk_body, 0)
    _cast_blocks(cast_in, cast_out)


def _gdn(p, col, row, conv_w, norm_w, bn, seq, cast_weights, cast_layer):
    t = p.shape[0]
    ts = 256
    per_b = seq // ts
    width = GDN_HEADS * GDN_DK
    tok = lambda b, s: b * per_b + s
    pspec = lambda cb: pl.BlockSpec((ts, width), lambda b, s: (tok(b, s), cb))
    casts = [_cast_specs(w, cast_layer, bn * per_b, tok) for w in cast_weights]
    return pl.pallas_call(
        functools.partial(_gdn_kernel, ts=ts, n_cast=len(casts)),
        grid=(bn, per_b),
        in_specs=[
            pspec(0), pspec(1), pspec(2), pspec(3),
            pl.BlockSpec((ts, 4 * GDN_HEADS), lambda b, s: (tok(b, s), 0)),
            pl.BlockSpec((ts // CHUNK, GDN_HEADS, CHUNK), lambda b, s: (tok(b, s), 0, 0)),
            pl.BlockSpec((CONV_W, 3 * width), lambda b, s: (0, 0)),
            pl.BlockSpec((1, GDN_DV), lambda b, s: (0, 0)),
        ] + [c[0] for c in casts],
        out_specs=[pl.BlockSpec((ts, width), lambda b, s: (tok(b, s), 0))] + [c[1] for c in casts],
        out_shape=[jax.ShapeDtypeStruct((t, width), BF16)] + [c[2] for c in casts],
        scratch_shapes=[
            pltpu.VMEM((GDN_HEADS, GDN_DK, GDN_DV), F32),
            pltpu.VMEM((SUBLANES, 3 * width), F32),
            pltpu.VMEM((ts, 3 * width), F32),
        ],
        compiler_params=_params("arbitrary", "arbitrary"),
        name="gdn",
    )(p, p, p, p, col, row, conv_w, norm_w.reshape(1, GDN_DV), *cast_weights)


def _gla_kernel(q_ref, k_ref, v_ref, r_ref, sm_ref, wa_ref, ba_ref, nw_ref, o_ref, st_scr, g_scr, *, ts):
    pair_w = 2 * GLA_DK
    n_pairs = GLA_HEADS // 2

    @pl.when(pl.program_id(1) == 0)
    def _():
        st_scr[...] = jnp.zeros_like(st_scr)

    logits = _dot(sm_ref[...], wa_ref[...], precision=HIGHEST) + ba_ref[...]
    g_scr[...] = -_softplus(-logits) / GLA_GATE_NORM

    ri = lax.broadcasted_iota(jnp.int32, (CHUNK, CHUNK), 0)
    rj = lax.broadcasted_iota(jnp.int32, (CHUNK, CHUNK), 1)
    incl = ri >= rj
    lower = jnp.where(incl, 1.0, 0.0).astype(BF16)
    lane = lax.broadcasted_iota(jnp.int32, (CHUNK, pair_w), 1)
    sr = lax.broadcasted_iota(jnp.int32, (2 * GLA_DV, pair_w), 0)
    sc = lax.broadcasted_iota(jnp.int32, (2 * GLA_DV, pair_w), 1)
    live = (sr // GLA_DV) == (sc // GLA_DK)
    nw = nw_ref[...]

    def chunk_body(c, carry):
        r0 = pl.multiple_of(c * CHUNK, CHUNK)
        pairs = range(n_pairs)
        heads = range(GLA_HEADS)
        psl = [slice(p * pair_w, (p + 1) * pair_w) for p in pairs]
        hsl = [slice(h * GLA_DV, (h + 1) * GLA_DV) for h in heads]
        g = g_scr[pl.ds(r0, CHUNK), :]
        g1 = g.astype(BF16)
        rem = g - g1.astype(F32)
        g2 = rem.astype(BF16)
        g3 = (rem - g2.astype(F32)).astype(BF16)
        b_all = _dot(lower, g1) + (_dot(lower, g2) + _dot(lower, g3))
        b = [b_all[:, psl[p]] for p in pairs]
        b_mid = [a[CHUNK // 2:CHUNK // 2 + 1, :] for a in b]
        b_last = [a[CHUNK - 1:CHUNK, :] for a in b]
        q = [q_ref[pl.ds(r0, CHUNK), psl[p]].astype(F32) * (GLA_DK ** -0.5) for p in pairs]
        k = [k_ref[pl.ds(r0, CHUNK), psl[p]].astype(F32) for p in pairs]
        q_i = [q[p] * jnp.exp(b[p] - b_mid[p]) for p in pairs]
        k_i = [(k[p] * jnp.exp(b_mid[p] - b[p])).astype(BF16) for p in pairs]
        q_s = [(q[p] * jnp.exp(b[p])).astype(BF16) for p in pairs]
        k_s = [(k[p] * jnp.exp(b_last[p] - b[p])).astype(BF16) for p in pairs]
        vp = [v_ref[pl.ds(r0, CHUNK), 2 * p * GLA_DV:(2 * p + 2) * GLA_DV] for p in pairs]
        st = [st_scr[p] for p in pairs]
        o_inter = [_dot_nt(q_s[p], st[p].astype(BF16)) for p in pairs]
        upd = [_dot_tn(vp[p], k_s[p]) for p in pairs]
        for p in pairs:
            st_scr[p] = st[p] * jnp.exp(b_last[p]) + jnp.where(live, upd[p], 0.0)
        q_h = [jnp.where((lane // GLA_DK) == (h % 2), q_i[h // 2], 0.0).astype(BF16) for h in heads]
        att = [jnp.where(incl, _dot_nt(q_h[h], k_i[h // 2]), 0.0).astype(BF16) for h in heads]
        o = [_dot(att[h], vp[h // 2][:, (h % 2) * GLA_DV:(h % 2 + 1) * GLA_DV])
             + o_inter[h // 2][:, (h % 2) * GLA_DV:(h % 2 + 1) * GLA_DV] for h in heads]
        for h in heads:
            oh = o[h] * lax.rsqrt(jnp.mean(o[h] * o[h], axis=-1, keepdims=True) + EPS) * nw
            rg = r_ref[pl.ds(r0, CHUNK), hsl[h]].astype(F32)
            o_ref[pl.ds(r0, CHUNK), hsl[h]] = (oh * _silu(rg)).astype(o_ref.dtype)
        return carry

    lax.fori_loop(0, ts // CHUNK, chunk_body, 0)


def _gla(p, small, w_a2, b_a2, norm_w, bn, seq):
    t = p.shape[0]
    ts = 256
    per_b = seq // ts
    qk_w = GLA_HEADS * GLA_DK
    v_w = GLA_HEADS * GLA_DV
    gdn_cols = 4 * GDN_HEADS * GDN_DK
    tok = lambda b, s: b * per_b + s
    wa_pad = jnp.zeros((LANES, qk_w), F32).at[2 * GDN_HEADS:2 * GDN_HEADS + GLA_RANK].set(w_a2.astype(F32))
    return pl.pallas_call(
        functools.partial(_gla_kernel, ts=ts),
        grid=(bn, per_b),
        in_specs=[
            pl.BlockSpec((ts, qk_w), lambda b, s: (tok(b, s), gdn_cols // qk_w)),
            pl.BlockSpec((ts, qk_w), lambda b, s: (tok(b, s), gdn_cols // qk_w + 1)),
            pl.BlockSpec((ts, v_w), lambda b, s: (tok(b, s), (gdn_cols + 2 * qk_w) // v_w)),
            pl.BlockSpec((ts, v_w), lambda b, s: (tok(b, s), (gdn_cols + 2 * qk_w) // v_w + 1)),
            pl.BlockSpec((ts, LANES), lambda b, s: (tok(b, s), 0)),
            pl.BlockSpec((LANES, qk_w), lambda b, s: (0, 0)),
            pl.BlockSpec((1, qk_w), lambda b, s: (0, 0)),
            pl.BlockSpec((1, GLA_DV), lambda b, s: (0, 0)),
        ],
        out_specs=pl.BlockSpec((ts, v_w), lambda b, s: (tok(b, s), 0)),
        out_shape=jax.ShapeDtypeStruct((t, v_w), BF16),
        scratch_shapes=[
            pltpu.VMEM((GLA_HEADS // 2, 2 * GLA_DV, 2 * GLA_DK), F32),
            pltpu.VMEM((ts, qk_w), F32),
        ],
        compiler_params=_params("arbitrary", "arbitrary"),
        name="gla",
    )(p, p, p, p, small, wa_pad, b_a2.reshape(1, qk_w).astype(F32), norm_w.reshape(1, GLA_DV))


def _gelu_tanh(x):
    return 0.5 * x * (1.0 + jnp.tanh(0.7978845608028654 * (x + 0.044715 * x * x * x)))


def _lru_kernel(x_ref, g_ref, cw_ref, cb_ref, wra_ref, bra_ref, wri_ref, bri_ref, lam_ref, *rest, ts, n_cast):
    cast_in = rest[:n_cast]
    o_ref = rest[n_cast]
    cast_out = rest[n_cast + 1:2 * n_cast + 1]
    tail_scr, h_scr, a_scr, u_scr = rest[2 * n_cast + 1:]
    width = x_ref.shape[1]
    blk = width // LRU_BLOCKS

    @pl.when(pl.program_id(1) == 0)
    def _():
        tail_scr[...] = jnp.zeros_like(tail_scr)
        h_scr[...] = jnp.zeros_like(h_scr)

    for gi in range(LRU_BLOCKS):
        lo, hi = gi * blk, (gi + 1) * blk
        cur = x_ref[:, lo:hi].astype(F32)
        full = jnp.concatenate([tail_scr[:, lo:hi], cur], axis=0)
        w = cw_ref[:, lo:hi]
        acc = full * w[CONV_W - 1:CONV_W]
        for i in range(CONV_W - 1):
            acc = acc + pltpu.roll(full, CONV_W - 1 - i, axis=0) * w[i:i + 1]
        xc = acc[SUBLANES:] + cb_ref[:, lo:hi]
        tail_scr[:, lo:hi] = cur[ts - SUBLANES:]
        xb = xc.astype(BF16)
        r = _sigmoid(_dot(xb, wra_ref[gi]) + bra_ref[:, lo:hi])
        ig = _sigmoid(_dot(xb, wri_ref[gi]) + bri_ref[:, lo:hi])
        log_a = -LRU_C * r * _softplus(-lam_ref[:, lo:hi])
        a = jnp.exp(log_a)
        a_scr[:, lo:hi] = a
        u_scr[:, lo:hi] = jnp.sqrt(1.0 - a * a) * (ig * xc)

    row = lax.broadcasted_iota(jnp.int32, (SUBLANES, width), 0)

    def scan_body(rb, h_prev):
        r0 = pl.multiple_of(rb * SUBLANES, SUBLANES)
        a = a_scr[pl.ds(r0, SUBLANES), :]
        u = u_scr[pl.ds(r0, SUBLANES), :]
        for d in (1, 2, 4):
            a_sh = jnp.where(row >= d, pltpu.roll(a, d, axis=0), 1.0)
            u_sh = jnp.where(row >= d, pltpu.roll(u, d, axis=0), 0.0)
            u = a * u_sh + u
            a = a * a_sh
        h = u + a * h_prev
        gate = g_ref[pl.ds(r0, SUBLANES), :].astype(F32)
        u_scr[pl.ds(r0, SUBLANES), :] = _gelu_tanh(gate) * h
        return h[SUBLANES - 1:SUBLANES, :]

    h_last = lax.fori_loop(0, ts // SUBLANES, scan_body, h_scr[...])
    h_scr[...] = h_last
    o_ref[...] = u_scr[...].astype(o_ref.dtype)
    _cast_blocks(cast_in, cast_out)


def _lru(p, conv_w, conv_b, w_ra, b_ra, w_ri, b_ri, lam, bn, seq, cast_weights, cast_layer):
    t = p.shape[0]
    width = p.shape[1] // 2
    blk = width // LRU_BLOCKS
    ts = 256
    per_b = seq // ts
    tok = lambda b, s: b * per_b + s
    vec = lambda a: a.reshape(1, width).astype(F32)
    vspec = pl.BlockSpec((1, width), lambda b, s: (0, 0))
    wspec = pl.BlockSpec((LRU_BLOCKS, blk, blk), lambda b, s: (0, 0, 0))
    casts = [_cast_specs(w, cast_layer, bn * per_b, tok) for w in cast_weights]
    return pl.pallas_call(
        functools.partial(_lru_kernel, ts=ts, n_cast=len(casts)),
        grid=(bn, per_b),
        in_specs=[
            pl.BlockSpec((ts, width), lambda b, s: (tok(b, s), 0)),
            pl.BlockSpec((ts, width), lambda b, s: (tok(b, s), 1)),
            pl.BlockSpec((CONV_W, width), lambda b, s: (0, 0)),
            vspec, wspec, vspec, wspec, vspec, vspec,
        ] + [c[0] for c in casts],
        out_specs=[pl.BlockSpec((ts, width), lambda b, s: (tok(b, s), 0))] + [c[1] for c in casts],
        out_shape=[jax.ShapeDtypeStruct((t, width), BF16)] + [c[2] for c in casts],
        scratch_shapes=[
            pltpu.VMEM((SUBLANES, width), F32),
            pltpu.VMEM((1, width), F32),
            pltpu.VMEM((ts, width), F32),
            pltpu.VMEM((ts, width), F32),
        ],
        compiler_params=_params("arbitrary", "arbitrary"),
        name="lru",
    )(p, p, conv_w, vec(conv_b), w_ra.astype(BF16), vec(b_ra), w_ri.astype(BF16), vec(b_ri), vec(lam),
      *cast_weights)


def _pack_bf16_pairs(h):
    half = h.shape[1] // 2
    lo = lax.bitcast_convert_type(h[:, :half].astype(BF16).astype(F32), jnp.uint32)
    hi = lax.bitcast_convert_type(h[:, half:].astype(BF16).astype(F32), jnp.uint32)
    return (lo >> 16) | (hi & jnp.uint32(0xFFFF0000))


def _unpack_bf16_pairs(w):
    lo = lax.bitcast_convert_type(w << 16, F32).astype(BF16)
    hi = lax.bitcast_convert_type(w & jnp.uint32(0xFFFF0000), F32).astype(BF16)
    return lo, hi


def _out_proj_kernel(*refs, n_in):
    a_refs = refs[:n_in]
    w_refs = refs[n_in:2 * n_in]
    (x_ref, mod_ref, ln_ref, rwh_ref, rwl_ref, rb_ref,
     xo_ref, hp_ref, ti_ref, tw_ref, tr_ref, cnt_ref, cnt_scr) = refs[2 * n_in:]

    @pl.when(pl.program_id(0) == 0)
    def _():
        cnt_scr[...] = jnp.zeros_like(cnt_scr)

    y = _dot(a_refs[0][...], w_refs[0][...])
    for a_ref, w_ref in zip(a_refs[1:], w_refs[1:]):
        y = y + _dot(a_ref[...], w_ref[...])
    m = mod_ref[0]
    xn = x_ref[...] + m[2:3] * y
    xo_ref[...] = xn
    h = _norm_mod(xn, ln_ref[...], m[3:4], m[4:5])
    hp_ref[...] = _pack_bf16_pairs(h).reshape(hp_ref.shape)
    h_hi = h.astype(BF16)
    h_lo = (h - h_hi.astype(F32)).astype(BF16)
    logits = _dot(h_hi, rwh_ref[...]) + (_dot(h_hi, rwl_ref[...]) + _dot(h_lo, rwh_ref[...])) + rb_ref[...]
    lane = lax.broadcasted_iota(jnp.int32, logits.shape, 1)
    cur = jnp.where(lane < N_EXPERTS, logits, -jnp.inf)
    vals, idxs = [], []
    for _ in range(TOP_K):
        mx = jnp.max(cur, axis=-1, keepdims=True)
        ix = jnp.min(jnp.where(cur == mx, lane, LANES), axis=-1, keepdims=True)
        vals.append(mx)
        idxs.append(ix)
        cur = jnp.where(lane == ix, -jnp.inf, cur)
    es = [jnp.exp(v - vals[0]) for v in vals]
    den = es[0]
    for e in es[1:]:
        den = den + e
    tm = logits.shape[0]
    ri = lax.broadcasted_iota(jnp.int32, (tm, tm), 0)
    rj = lax.broadcasted_iota(jnp.int32, (tm, tm), 1)
    tri = jnp.where(rj < ri, 1.0, 0.0).astype(BF16)
    base = cnt_scr[...]
    ti = jnp.zeros(logits.shape, jnp.int32)
    tw = jnp.zeros(logits.shape, F32)
    tr = jnp.zeros(logits.shape, jnp.int32)
    for kk in range(TOP_K):
        sel = lane == idxs[kk]
        onehot = jnp.where(sel, 1.0, 0.0)
        before = _dot(tri, onehot.astype(BF16)) + base
        rank = jnp.sum(jnp.where(sel, before, 0.0), axis=-1, keepdims=True)
        base = base + jnp.sum(onehot, axis=0, keepdims=True)
        ti = jnp.where(lane == kk, idxs[kk], ti)
        tw = jnp.where(lane == kk, es[kk] / den, tw)
        tr = jnp.where(lane == kk, rank.astype(jnp.int32), tr)
    cnt_scr[...] = base
    ti_ref[...] = ti
    tw_ref[...] = tw
    tr_ref[...] = tr
    cnt_ref[...] = base.astype(jnp.int32)


def _out_proj(acts, weights, x, mod, ln_ffn, router_w, router_b, seq):
    t, d = x.shape
    tm = 256
    per_b = seq // tm
    n_in = len(acts)
    rw = jnp.zeros((d, LANES), F32).at[:, :N_EXPERTS].set(router_w.astype(F32))
    rw_hi = rw.astype(BF16)
    rw_lo = (rw - rw_hi.astype(F32)).astype(BF16)
    rb = jnp.zeros((1, LANES), F32).at[0, :N_EXPERTS].set(router_b.astype(F32))
    tiles = d // 2 // LANES
    in_specs = [pl.BlockSpec((tm, a.shape[1]), lambda i: (i, 0)) for a in acts]
    in_specs += [pl.BlockSpec(w.shape, lambda i: (0, 0)) for w in weights]
    in_specs += [
        pl.BlockSpec((tm, d), lambda i: (i, 0)),
        pl.BlockSpec((1, 6, d), lambda i: (i // per_b, 0, 0)),
        pl.BlockSpec((1, d), lambda i: (0, 0)),
        pl.BlockSpec((d, LANES), lambda i: (0, 0)),
        pl.BlockSpec((d, LANES), lambda i: (0, 0)),
        pl.BlockSpec((1, LANES), lambda i: (0, 0)),
    ]
    row_d = pl.BlockSpec((tm, d), lambda i: (i, 0))
    row_h = pl.BlockSpec((tm, tiles, LANES), lambda i: (i, 0, 0))
    row_l = pl.BlockSpec((tm, LANES), lambda i: (i, 0))
    return pl.pallas_call(
        functools.partial(_out_proj_kernel, n_in=n_in),
        grid=(t // tm,),
        in_specs=in_specs,
        out_specs=[row_d, row_h, row_l, row_l, row_l, pl.BlockSpec((1, LANES), lambda i: (0, 0))],
        out_shape=[
            jax.ShapeDtypeStruct((t, d), F32),
            jax.ShapeDtypeStruct((t, tiles, LANES), jnp.uint32),
            jax.ShapeDtypeStruct((t, LANES), jnp.int32),
            jax.ShapeDtypeStruct((t, LANES), F32),
            jax.ShapeDtypeStruct((t, LANES), jnp.int32),
            jax.ShapeDtypeStruct((1, LANES), jnp.int32),
        ],
        scratch_shapes=[pltpu.VMEM((1, LANES), F32)],
        compiler_params=_params("arbitrary"),
        name="out_proj",
    )(*acts, *weights, x, mod, ln_ffn.reshape(1, d), rw_hi, rw_lo, rb)


def _moe_routing(ti, tr, cnt, n_tok):
    n_blk = (n_tok * TOP_K) // MOE_BLOCK + N_EXPERTS
    counts = cnt[0, :N_EXPERTS]
    padded = ((counts + MOE_BLOCK - 1) // MOE_BLOCK) * MOE_BLOCK
    pend = jnp.cumsum(padded)
    pstart = pend - padded
    e = ti[:, :TOP_K]
    hit = e[:, :, None] == jnp.arange(N_EXPERTS, dtype=jnp.int32)
    dest = tr[:, :TOP_K] + jnp.sum(jnp.where(hit, pstart, 0), axis=-1)
    blk_start = jnp.arange(n_blk, dtype=jnp.int32) * MOE_BLOCK
    blk_e = jnp.minimum(jnp.sum(pend[None, :] <= blk_start[:, None], axis=1), N_EXPERTS - 1).astype(jnp.int32)
    i = jnp.arange(MOE_BLOCK, dtype=jnp.int32)
    tail = (pstart + counts)[:, None] + i[None, :]
    is_tail = i[None, :] < (padded - counts)[:, None]
    spare = (jnp.cumsum(jnp.logical_not(is_tail).reshape(-1).astype(jnp.int32)) - 1).reshape(is_tail.shape)
    pad_slots = jnp.where(is_tail, tail, pend[-1] + spare).reshape(-1)
    n_used = (pend[-1:] // MOE_BLOCK).astype(jnp.int32)
    return dest.astype(jnp.int32), pad_slots.astype(jnp.int32), blk_e, n_used


DISPATCH_TOKENS = 512
COMBINE_TOKENS = 256
ISSUE_TOKENS = 4
ISSUE_ROWS = ISSUE_TOKENS * TOP_K


def _dispatch_kernel(dest_ref, pad_ref, hp_ref, xs_hbm, zero_scr, sem):
    n_pad = pad_ref.shape[2]
    zero_scr[...] = jnp.zeros_like(zero_scr)

    def issue(g, c):
        t0 = g * ISSUE_TOKENS
        dsts = [dest_ref[0, 0, t0 * TOP_K + j] for j in range(ISSUE_ROWS)]
        for j, dst in enumerate(dsts):
            pltpu.make_async_copy(hp_ref.at[t0 + j // TOP_K], xs_hbm.at[dst], sem.at[0]).start(priority=j % 2)
        return c

    lax.fori_loop(0, DISPATCH_TOKENS // ISSUE_TOKENS, issue, 0)

    def issue_zero(g, c):
        dsts = [pad_ref[0, 0, g * ISSUE_ROWS + j] for j in range(ISSUE_ROWS)]
        for j, dst in enumerate(dsts):
            pltpu.make_async_copy(zero_scr, xs_hbm.at[dst], sem.at[0]).start(priority=j % 2)
        return c

    lax.fori_loop(0, n_pad // ISSUE_ROWS, issue_zero, 0)

    rows = DISPATCH_TOKENS * TOP_K + n_pad
    for _ in range(rows // DISPATCH_TOKENS):
        pltpu.make_async_copy(hp_ref, xs_hbm.at[pl.ds(0, DISPATCH_TOKENS)], sem.at[0]).wait()
    rem = rows % DISPATCH_TOKENS
    if rem:
        pltpu.make_async_copy(hp_ref.at[pl.ds(0, rem)], xs_hbm.at[pl.ds(0, rem)], sem.at[0]).wait()


def _dispatch(hp, dest, pad_slots, n_buf):
    t = hp.shape[0]
    n_steps = t // DISPATCH_TOKENS
    n_pad = pad_slots.shape[0] // n_steps
    assert n_pad * n_steps == pad_slots.shape[0] and n_pad % ISSUE_ROWS == 0
    smem = lambda n: pl.BlockSpec((1, 1, n), lambda i: (i, 0, 0), memory_space=pltpu.SMEM)
    return pl.pallas_call(
        _dispatch_kernel,
        grid=(n_steps,),
        in_specs=[
            smem(DISPATCH_TOKENS * TOP_K),
            smem(n_pad),
            pl.BlockSpec((DISPATCH_TOKENS,) + hp.shape[1:], lambda i: (i, 0, 0)),
        ],
        out_specs=pl.BlockSpec(memory_space=pl.ANY),
        out_shape=jax.ShapeDtypeStruct((n_buf,) + hp.shape[1:], jnp.uint32),
        scratch_shapes=[pltpu.VMEM(hp.shape[1:], jnp.uint32), pltpu.SemaphoreType.DMA((1,))],
        compiler_params=_params("arbitrary"),
        name="moe_dispatch",
    )(dest.reshape(n_steps, 1, DISPATCH_TOKENS * TOP_K), pad_slots.reshape(n_steps, 1, n_pad), hp)


def _moe_kernel(blk_e_ref, n_used_ref, xs_ref, wgu_ref, bgu_ref, wdn_ref, bdn_ref, y_ref):
    del blk_e_ref
    half = xs_ref.shape[1] * xs_ref.shape[2]
    d_ff = wdn_ref.shape[1]
    used = pl.program_id(0) < n_used_ref[0]

    @pl.when(used)
    def _():
        lo, hi = _unpack_bf16_pairs(xs_ref[...].reshape(MOE_BLOCK, half))
        gu = _dot(lo, wgu_ref[0, :half, :]) + _dot(hi, wgu_ref[0, half:, :]) + bgu_ref[0]
        gate = jnp.minimum(gu[:, :d_ff], SWIGLU_LIMIT)
        up = jnp.clip(gu[:, d_ff:], -SWIGLU_LIMIT, SWIGLU_LIMIT)
        act = (up + 1.0) * gate * _sigmoid(SWIGLU_ALPHA * gate)
        y = _dot(act.astype(BF16), wdn_ref[0]) + bdn_ref[0]
        y_ref[...] = y.reshape(y_ref.shape)

    @pl.when(jnp.logical_not(used))
    def _():
        y_ref[...] = jnp.zeros_like(y_ref)


def _moe_experts(xs, blk_e, n_used, w_gu, b_gu, w_dn, b_dn):
    n_buf, tiles, _ = xs.shape
    d = 2 * tiles * LANES
    d_ff = w_dn.shape[1]
    n_w = w_gu.shape[0]
    grid_spec = pltpu.PrefetchScalarGridSpec(
        num_scalar_prefetch=2,
        grid=(n_buf // MOE_BLOCK,),
        in_specs=[
            pl.BlockSpec((MOE_BLOCK, tiles, LANES), lambda b, e, u: (b, 0, 0)),
            pl.BlockSpec((1, d, 2 * d_ff), lambda b, e, u: (e[b], 0, 0)),
            pl.BlockSpec((1, 1, 2 * d_ff), lambda b, e, u: (e[b], 0, 0)),
            pl.BlockSpec((1, d_ff, d), lambda b, e, u: (e[b], 0, 0)),
            pl.BlockSpec((1, 1, d), lambda b, e, u: (e[b], 0, 0)),
        ],
        out_specs=pl.BlockSpec((MOE_BLOCK, d // LANES, LANES), lambda b, e, u: (b, 0, 0)),
    )
    return pl.pallas_call(
        _moe_kernel,
        grid_spec=grid_spec,
        out_shape=jax.ShapeDtypeStruct((n_buf, d // LANES, LANES), F32),
        compiler_params=_params("arbitrary"),
        name="moe_experts",
    )(blk_e, n_used, xs, w_gu, b_gu.reshape(n_w, 1, 2 * d_ff), w_dn, b_dn.reshape(n_w, 1, d))


def _combine_kernel(dest_ref, dest_next_ref, x_ref, tw_ref, mod_ref, lnf_ref, y_hbm, o_ref, ybuf, sem, *, final):
    i = pl.program_id(0)
    n = pl.num_programs(0)
    slot = i % 2
    tm = COMBINE_TOKENS

    def issue(ids_ref, s):
        def body(g, c):
            t0 = g * ISSUE_TOKENS
            srcs = [ids_ref[0, 0, t0 * TOP_K + j] for j in range(ISSUE_TOKENS * TOP_K)]
            for j, src in enumerate(srcs):
                pltpu.make_async_copy(y_hbm.at[src], ybuf.at[s, (j % TOP_K) * tm + t0 + j // TOP_K],
                                      sem.at[s]).start(priority=j % 2)
            return c
        lax.fori_loop(0, tm // ISSUE_TOKENS, body, 0)

    @pl.when(i == 0)
    def _():
        issue(dest_ref, 0)

    @pl.when(i + 1 < n)
    def _():
        issue(dest_next_ref, 1 - slot)

    pltpu.make_async_copy(y_hbm.at[pl.ds(0, TOP_K * tm)], ybuf.at[slot], sem.at[slot]).wait()
    tw = tw_ref[...]
    acc = None
    for kk in range(TOP_K):
        gate = jnp.broadcast_to(tw[:, kk:kk + 1], (tm, LANES)).reshape(tm, 1, LANES)
        term = gate * ybuf[slot, kk * tm:(kk + 1) * tm]
        acc = term if acc is None else acc + term
    xn = x_ref[...] + mod_ref[0][5:6] * acc.reshape(x_ref.shape)
    if final:
        xn = xn * lax.rsqrt(jnp.mean(xn * xn, axis=-1, keepdims=True) + EPS) * lnf_ref[...]
    o_ref[...] = xn


def _combine(x, y, dest, tw, mod, ln_final, seq, final):
    t, d = x.shape
    tm = COMBINE_TOKENS
    per_b = seq // tm
    nt = t // tm
    ids = dest.reshape(nt, 1, tm * TOP_K)
    id_spec = lambda f: pl.BlockSpec((1, 1, tm * TOP_K), f, memory_space=pltpu.SMEM)
    return pl.pallas_call(
        functools.partial(_combine_kernel, final=final),
        grid=(nt,),
        in_specs=[
            id_spec(lambda i: (i, 0, 0)),
            id_spec(lambda i: (jnp.minimum(i + 1, nt - 1), 0, 0)),
            pl.BlockSpec((tm, d), lambda i: (i, 0)),
            pl.BlockSpec((tm, LANES), lambda i: (i, 0)),
            pl.BlockSpec((1, 6, d), lambda i: (i // per_b, 0, 0)),
            pl.BlockSpec((1, d), lambda i: (0, 0)),
            pl.BlockSpec(memory_space=pl.ANY),
        ],
        out_specs=pl.BlockSpec((tm, d), lambda i: (i, 0)),
        out_shape=jax.ShapeDtypeStruct((t, d), F32),
        scratch_shapes=[pltpu.VMEM((2, TOP_K * tm, d // LANES, LANES), F32), pltpu.SemaphoreType.DMA((2,))],
        compiler_params=_params("arbitrary"),
        name="moe_combine",
    )(ids, ids, x, tw, mod, ln_final.reshape(1, d), y)


def _ffn(x, hp, ti, tw, tr, cnt, mod, w_gu, b_gu, w_dn, b_dn, ln_final, seq, final):
    t = x.shape[0]
    dest, pad_slots, blk_e, n_used = _moe_routing(ti, tr, cnt, t)
    n_buf = blk_e.shape[0] * MOE_BLOCK
    xs = _dispatch(hp, dest, pad_slots, n_buf)
    y = _moe_experts(xs, blk_e, n_used, w_gu, b_gu, w_dn, b_dn)
    return _combine(x, y, dest, tw, mod, ln_final, seq, final)


def _even_inproj_weights(w_in):
    gdn_cols = 4 * GDN_HEADS * GDN_DK
    small0 = gdn_cols
    gla0 = small0 + 2 * GDN_HEADS
    gla_cols = 2 * GLA_HEADS * GLA_DK + 2 * GLA_HEADS * GLA_DV
    lg0 = gla0 + gla_cols
    w_big = jnp.concatenate([w_in[:, :gdn_cols], w_in[:, gla0:lg0]], axis=1).astype(BF16)
    w_small = jnp.zeros((w_in.shape[0], LANES), F32)
    w_small = w_small.at[:, :2 * GDN_HEADS].set(w_in[:, small0:gla0])
    w_small = w_small.at[:, 2 * GDN_HEADS:2 * GDN_HEADS + GLA_RANK].set(w_in[:, lg0:lg0 + GLA_RANK])
    return w_big, w_small.astype(BF16)


def _even_mixer(x, mod, ln_mix, w_in, conv_w, a_log, dt_bias, gdn_norm, w_a2, b_a2, gla_norm, bn, seq,
                cast_weights, cast_layer):
    w_big, w_small = _even_inproj_weights(w_in)
    p, small = _norm_proj(x, ln_mix, mod, w_big, w_small, seq)
    col, row = _gdn_gates(small, a_log, dt_bias)
    o_a, *w_bf16 = _gdn(p, col, row, conv_w, gdn_norm, bn, seq, cast_weights, cast_layer)
    o_b = _gla(p, small, w_a2, b_a2, gla_norm, bn, seq)
    return o_a, o_b, w_bf16


def kernel(x, c, ln_mix, ln_ffn, ln_final, w_ada, b_ada, e_w_in, e_conv_w, e_a_log, e_dt_bias, e_gdn_norm, e_gla_w_a2, e_gla_b_a2, e_gla_norm, e_w_out, o_w_in, o_conv_w, o_conv_b, o_w_ra, o_b_ra, o_w_ri, o_b_ri, o_lam, o_w_out, router_w, router_b, exp_w_gu, exp_b_gu, exp_w_dn, exp_b_dn):
    bn, seq, d = x.shape
    depth = w_ada.shape[0]
    mods = _ada_mod(c, w_ada, b_ada)
    xf = x.reshape(bn * seq, d)
    experts = [exp_w_gu, exp_w_dn]
    for layer in range(depth):
        mod = mods[layer]
        j = layer // 2
        if layer % 2 == 0:
            o_a, o_b, (w_gu, w_dn) = _even_mixer(xf, mod, ln_mix[layer], e_w_in[j], e_conv_w[j], e_a_log[j],
                                                 e_dt_bias[j], e_gdn_norm[j], e_gla_w_a2[j], e_gla_b_a2[j],
                                                 e_gla_norm[j], bn, seq, experts, layer)
            split = o_a.shape[1]
            acts = [o_a, o_b]
            weights = [e_w_out[j][:split].astype(BF16), e_w_out[j][split:].astype(BF16)]
        else:
            p = _norm_proj(xf, ln_mix[layer], mod, o_w_in[j].astype(BF16), None, seq)
            o_c, w_gu, w_dn = _lru(p, o_conv_w[j], o_conv_b[j], o_w_ra[j], o_b_ra[j], o_w_ri[j], o_b_ri[j], o_lam[j],
                                   bn, seq, experts, layer)
            acts = [o_c]
            weights = [o_w_out[j].astype(BF16)]
        xf, hp, ti, tw, tr, cnt = _out_proj(acts, weights, xf, mod, ln_ffn[layer], router_w[layer], router_b[layer], seq)
        xf = _ffn(xf, hp, ti, tw, tr, cnt, mod, w_gu, exp_b_gu[layer], w_dn, exp_b_dn[layer],
                  ln_final, seq, final=(layer == depth - 1))
    return xf.reshape(bn, seq, d)
```

```python
import functools

import jax
import jax.numpy as jnp
from jax import lax
from jax.experimental import pallas as pl
from jax.experimental.pallas import tpu as pltpu

F32 = jnp.float32
BF16 = jnp.bfloat16
HIGHEST = lax.Precision.HIGHEST

EPS = 1e-6
CONV_W = 4
CHUNK = 64
GDN_HEADS = 8
GDN_DK = 128
GDN_DV = 128
GLA_HEADS = 8
GLA_DK = 64
GLA_DV = 128
GLA_RANK = 16
GLA_GATE_NORM = 16.0
LRU_BLOCKS = 8
LRU_C = 8.0
N_EXPERTS = 32
TOP_K = 4
SWIGLU_LIMIT = 7.0
SWIGLU_ALPHA = 1.702
MOE_BLOCK = 512

LANES = 128
SUBLANES = 8
VMEM_LIMIT = 56 * 1024 * 1024


def _dot(a, b, precision=None):
    return jnp.dot(a, b, preferred_element_type=F32, precision=precision)


def _dot_nt(a, b, precision=None):
    return lax.dot_general(a, b, (((1,), (1,)), ((), ())), preferred_element_type=F32, precision=precision)


def _dot_tn(a, b):
    return lax.dot_general(a, b, (((0,), (0,)), ((), ())), preferred_element_type=F32)


def _dot_split(a, b):
    a_hi = a.astype(BF16)
    b_hi = b.astype(BF16)
    a_lo = (a - a_hi.astype(F32)).astype(BF16)
    b_lo = (b - b_hi.astype(F32)).astype(BF16)
    return _dot(a_hi, b_hi) + (_dot(a_hi, b_lo) + _dot(a_lo, b_hi))


def _sigmoid(x):
    return 1.0 / (1.0 + jnp.exp(-x))


def _softplus(x):
    return jnp.maximum(x, 0.0) + jnp.log(1.0 + jnp.exp(-jnp.abs(x)))


def _silu(x):
    return x * _sigmoid(x)


def _params(*sem):
    return pltpu.CompilerParams(dimension_semantics=sem, vmem_limit_bytes=VMEM_LIMIT)


def _ada_kernel(c_ref, w_ref, b_ref, o_ref):
    c = c_ref[...]
    o_ref[0] = _dot(_silu(c), w_ref[0], precision=HIGHEST) + b_ref[0]


def _ada_mod(c, w_ada, b_ada):
    depth, d, n6 = w_ada.shape
    bn = c.shape[0]
    rows = -(-bn // SUBLANES) * SUBLANES
    cp = jnp.zeros((rows, d), F32).at[:bn].set(c)
    tn = 768
    out = pl.pallas_call(
        _ada_kernel,
        grid=(depth, n6 // tn),
        in_specs=[
            pl.BlockSpec((rows, d), lambda l, j: (0, 0)),
            pl.BlockSpec((1, d, tn), lambda l, j: (l, 0, j)),
            pl.BlockSpec((1, 1, tn), lambda l, j: (l, 0, j)),
        ],
        out_specs=pl.BlockSpec((1, rows, tn), lambda l, j: (l, 0, j)),
        out_shape=jax.ShapeDtypeStruct((depth, rows, n6), F32),
        compiler_params=_params("arbitrary", "arbitrary"),
        name="ada_mod",
    )(cp, w_ada, b_ada.reshape(depth, 1, n6))
    return out[:, :bn].reshape(depth, bn, 6, d)


def _norm_mod(x, ln, shift, scale):
    y = x * lax.rsqrt(jnp.mean(x * x, axis=-1, keepdims=True) + EPS) * ln
    return y * (1.0 + scale) + shift


def _norm_proj_kernel(x_ref, ln_ref, mod_ref, w_ref, *rest, has_small):
    if has_small:
        ws_ref, o_ref, os_ref, h_scr = rest
    else:
        o_ref, h_scr = rest
    j = pl.program_id(1)

    @pl.when(j == 0)
    def _():
        m = mod_ref[0]
        h = _norm_mod(x_ref[...], ln_ref[...], m[0:1], m[1:2]).astype(BF16)
        h_scr[...] = h
        if has_small:
            os_ref[...] = _dot(h, ws_ref[...])

    o_ref[...] = _dot(h_scr[...], w_ref[...]).astype(o_ref.dtype)


def _norm_proj(x, ln, mod, w, w_small, seq):
    t, d = x.shape
    n = w.shape[1]
    tm = 512
    tn = max(c for c in range(LANES, 2048 + 1, LANES) if n % c == 0)
    per_b = seq // tm
    has_small = w_small is not None
    in_specs = [
        pl.BlockSpec((tm, d), lambda i, j: (i, 0)),
        pl.BlockSpec((1, d), lambda i, j: (0, 0)),
        pl.BlockSpec((1, 6, d), lambda i, j: (i // per_b, 0, 0)),
        pl.BlockSpec((d, tn), lambda i, j: (0, j)),
    ]
    out_specs = [pl.BlockSpec((tm, tn), lambda i, j: (i, j))]
    out_shape = [jax.ShapeDtypeStruct((t, n), BF16)]
    args = [x, ln.reshape(1, d), mod, w]
    if has_small:
        in_specs.append(pl.BlockSpec((d, LANES), lambda i, j: (0, 0)))
        out_specs.append(pl.BlockSpec((tm, LANES), lambda i, j: (i, 0)))
        out_shape.append(jax.ShapeDtypeStruct((t, LANES), F32))
        args.append(w_small)
    outs = pl.pallas_call(
        functools.partial(_norm_proj_kernel, has_small=has_small),
        grid=(t // tm, n // tn),
        in_specs=in_specs,
        out_specs=out_specs,
        out_shape=out_shape,
        scratch_shapes=[pltpu.VMEM((tm, d), BF16)],
        compiler_params=_params("arbitrary", "arbitrary"),
        name="norm_proj",
    )(*args)
    return outs if has_small else outs[0]


def _gdn_gate_kernel(ga_ref, gb_ref, alog_ref, dtb_ref, gc_ref, beta_ref, eg_ref, ek_ref):
    g = -jnp.exp(alog_ref[...]) * _softplus(ga_ref[...] + dtb_ref[...])
    ci = lax.broadcasted_iota(jnp.int32, (LANES, LANES), 0)
    cj = lax.broadcasted_iota(jnp.int32, (LANES, LANES), 1)
    same = (ci // CHUNK) == (cj // CHUNK)
    upper = jnp.where(same & (ci <= cj), 1.0, 0.0).astype(F32)
    allc = jnp.where(same, 1.0, 0.0).astype(F32)
    gc = _dot(g, upper, precision=HIGHEST)
    g_last = _dot(g, allc, precision=HIGHEST)
    gc_ref[...] = gc
    beta_ref[...] = _sigmoid(gb_ref[...])
    eg_ref[...] = jnp.exp(gc)
    ek_ref[...] = jnp.exp(g_last - gc)


def _gdn_gates(small, a_log, dt_bias):
    t = small.shape[0]
    nb = t // LANES
    rows = GDN_HEADS * nb
    gb_t = small[:, 0:GDN_HEADS].T.reshape(rows, LANES)
    ga_t = small[:, GDN_HEADS:2 * GDN_HEADS].T.reshape(rows, LANES)
    alog_c = jnp.repeat(a_log.astype(F32), nb).reshape(rows, 1)
    dtb_c = jnp.repeat(dt_bias.astype(F32), nb).reshape(rows, 1)
    full = pl.BlockSpec((rows, LANES), lambda: (0, 0))
    col1 = pl.BlockSpec((rows, 1), lambda: (0, 0))
    gc, beta, eg, ek = pl.pallas_call(
        _gdn_gate_kernel,
        in_specs=[full, full, col1, col1],
        out_specs=[full] * 4,
        out_shape=[jax.ShapeDtypeStruct((rows, LANES), F32)] * 4,
        compiler_params=pltpu.CompilerParams(vmem_limit_bytes=VMEM_LIMIT),
        name="gdn_gates",
    )(ga_t, gb_t, alog_c, dtb_c)
    to_col = lambda a: a.reshape(GDN_HEADS, t).T
    col = jnp.concatenate([to_col(beta), to_col(gc), to_col(eg), to_col(ek)], axis=1)
    row = gc.reshape(GDN_HEADS, t // CHUNK, CHUNK).transpose(1, 0, 2)
    return col, row


def _cast_specs(w, layer, n_steps, step_of):
    _, n_exp, rows, width = w.shape
    per_step = n_exp * rows // n_steps
    assert per_step * n_steps == n_exp * rows
    if per_step >= rows:
        assert per_step % rows == 0
        e_blk, r_blk = per_step // rows, rows
    else:
        assert rows % per_step == 0
        e_blk, r_blk = 1, per_step
    r_steps = rows // r_blk
    in_spec = pl.BlockSpec((1, e_blk, r_blk, width),
                           lambda *g: (layer, step_of(*g) // r_steps, step_of(*g) % r_steps, 0))
    out_spec = pl.BlockSpec((e_blk, r_blk, width), lambda *g: (step_of(*g) // r_steps, step_of(*g) % r_steps, 0))
    return in_spec, out_spec, jax.ShapeDtypeStruct((n_exp, rows, width), BF16)


def _cast_blocks(in_refs, out_refs):
    for i_ref, o_ref in zip(in_refs, out_refs):
        o_ref[...] = i_ref[0].astype(o_ref.dtype)


def _gdn_kernel(q_ref, k_ref, v_ref, z_ref, col_ref, row_ref, cw_ref, nw_ref, *rest, ts, n_cast):
    cast_in = rest[:n_cast]
    o_ref = rest[n_cast]
    cast_out = rest[n_cast + 1:2 * n_cast + 1]
    s_scr, tail_scr, qkv_scr = rest[2 * n_cast + 1:]
    width = GDN_HEADS * GDN_DK

    @pl.when(pl.program_id(1) == 0)
    def _():
        s_scr[...] = jnp.zeros_like(s_scr)
        tail_scr[...] = jnp.zeros_like(tail_scr)

    strip = 256
    for part, ref in enumerate((q_ref, k_ref, v_ref)):
        for s0 in range(0, width, strip):
            c0 = part * width + s0
            cur = ref[:, s0:s0 + strip].astype(F32)
            full = jnp.concatenate([tail_scr[:, c0:c0 + strip], cur], axis=0)
            w = cw_ref[:, c0:c0 + strip]
            acc = full * w[CONV_W - 1:CONV_W]
            for i in range(CONV_W - 1):
                acc = acc + pltpu.roll(full, CONV_W - 1 - i, axis=0) * w[i:i + 1]
            y = acc[SUBLANES:]
            qkv_scr[:, c0:c0 + strip] = _silu(y)
            tail_scr[:, c0:c0 + strip] = cur[ts - SUBLANES:]

    ri = lax.broadcasted_iota(jnp.int32, (CHUNK, CHUNK), 0)
    rj = lax.broadcasted_iota(jnp.int32, (CHUNK, CHUNK), 1)
    incl = ri >= rj
    strict = ri > rj
    eye = jnp.where(ri == rj, 1.0, 0.0).astype(F32)
    nw = nw_ref[...]

    def chunk_body(c, carry):
        r0 = pl.multiple_of(c * CHUNK, CHUNK)
        colc = col_ref[pl.ds(r0, CHUNK), :]
        rowc = row_ref[c]
        heads = range(GDN_HEADS)
        sl = [slice(h * GDN_DK, (h + 1) * GDN_DK) for h in heads]
        q = [qkv_scr[pl.ds(r0, CHUNK), sl[h]] for h in heads]
        k = [qkv_scr[pl.ds(r0, CHUNK), width + h * GDN_DK:width + (h + 1) * GDN_DK] for h in heads]
        v = [qkv_scr[pl.ds(r0, CHUNK), 2 * width + h * GDN_DK:2 * width + (h + 1) * GDN_DK] for h in heads]
        q = [a * lax.rsqrt(jnp.sum(a * a, axis=-1, keepdims=True) + EPS) * (GDN_DK ** -0.5) for a in q]
        k = [a * lax.rsqrt(jnp.sum(a * a, axis=-1, keepdims=True) + EPS) for a in k]
        beta = [colc[:, h:h + 1] for h in heads]
        gc = [colc[:, 8 + h:9 + h] for h in heads]
        eg = [colc[:, 16 + h:17 + h] for h in heads]
        ek = [colc[:, 24 + h:25 + h] for h in heads]
        decay = [jnp.exp(jnp.where(incl, gc[h] - rowc[h:h + 1, :], -jnp.inf)) for h in heads]
        kb = [a.astype(BF16) for a in k]
        qb = [a.astype(BF16) for a in q]
        kk = [_dot_nt(kb[h], kb[h]) for h in heads]
        qk = [_dot_nt(qb[h], kb[h]) for h in heads]
        npow = [jnp.where(strict, -(beta[h] * kk[h] * decay[h]), 0.0) for h in heads]
        ainv = [eye + npow[h] for h in heads]
        for _ in range(5):
            npow = [_dot_split(npow[h], npow[h]) for h in heads]
            ainv = [ainv[h] + _dot_split(ainv[h], npow[h]) for h in heads]
        ab = [a.astype(BF16) for a in ainv]
        u = [_dot(ab[h], (v[h] * beta[h]).astype(BF16)) for h in heads]
        w = [_dot(ab[h], (k[h] * (beta[h] * eg[h])).astype(BF16)) for h in heads]
        qk = [jnp.where(incl, qk[h] * decay[h], 0.0).astype(BF16) for h in heads]
        q_s = [(q[h] * eg[h]).astype(BF16) for h in heads]
        k_s = [(k[h] * ek[h]).astype(BF16) for h in heads]
        state = [s_scr[h] for h in heads]
        sb = [a.astype(BF16) for a in state]
        v_new = [(u[h] - _dot(w[h].astype(BF16), sb[h])).astype(BF16) for h in heads]
        o = [_dot(q_s[h], sb[h]) + _dot(qk[h], v_new[h]) for h in heads]
        for h in heads:
            s_scr[h] = state[h] * eg[h][CHUNK - 1:CHUNK, :] + _dot_tn(k_s[h], v_new[h])
        for h in heads:
            oh = o[h] * lax.rsqrt(jnp.mean(o[h] * o[h], axis=-1, keepdims=True) + EPS) * nw
            z = z_ref[pl.ds(r0, CHUNK), sl[h]].astype(F32)
            o_ref[pl.ds(r0, CHUNK), sl[h]] = (oh * _silu(z)).astype(o_ref.dtype)
        return carry

    lax.fori_loop(0, ts // CHUNK, chunk_body, 0)
    _cast_blocks(cast_in, cast_out)


def _gdn(p, col, row, conv_w, norm_w, bn, seq, cast_weights, cast_layer):
    t = p.shape[0]
    ts = 256
    per_b = seq // ts
    width = GDN_HEADS * GDN_DK
    tok = lambda b, s: b * per_b + s
    pspec = lambda cb: pl.BlockSpec((ts, width), lambda b, s: (tok(b, s), cb))
    casts = [_cast_specs(w, cast_layer, bn * per_b, tok) for w in cast_weights]
    return pl.pallas_call(
        functools.partial(_gdn_kernel, ts=ts, n_cast=len(casts)),
        grid=(bn, per_b),
        in_specs=[
            pspec(0), pspec(1), pspec(2), pspec(3),
            pl.BlockSpec((ts, 4 * GDN_HEADS), lambda b, s: (tok(b, s), 0)),
            pl.BlockSpec((ts // CHUNK, GDN_HEADS, CHUNK), lambda b, s: (tok(b, s), 0, 0)),
            pl.BlockSpec((CONV_W, 3 * width), lambda b, s: (0, 0)),
            pl.BlockSpec((1, GDN_DV), lambda b, s: (0, 0)),
        ] + [c[0] for c in casts],
        out_specs=[pl.BlockSpec((ts, width), lambda b, s: (tok(b, s), 0))] + [c[1] for c in casts],
        out_shape=[jax.ShapeDtypeStruct((t, width), BF16)] + [c[2] for c in casts],
        scratch_shapes=[
            pltpu.VMEM((GDN_HEADS, GDN_DK, GDN_DV), F32),
            pltpu.VMEM((SUBLANES, 3 * width), F32),
            pltpu.VMEM((ts, 3 * width), F32),
        ],
        compiler_params=_params("arbitrary", "arbitrary"),
        name="gdn",
    )(p, p, p, p, col, row, conv_w, norm_w.reshape(1, GDN_DV), *cast_weights)


def _gla_kernel(q_ref, k_ref, v_ref, r_ref, sm_ref, wa_ref, ba_ref, nw_ref, o_ref, st_scr, g_scr, *, ts):
    pair_w = 2 * GLA_DK
    n_pairs = GLA_HEADS // 2

    @pl.when(pl.program_id(1) == 0)
    def _():
        st_scr[...] = jnp.zeros_like(st_scr)

    logits = _dot(sm_ref[...], wa_ref[...], precision=HIGHEST) + ba_ref[...]
    g_scr[...] = -_softplus(-logits) / GLA_GATE_NORM

    ri = lax.broadcasted_iota(jnp.int32, (CHUNK, CHUNK), 0)
    rj = lax.broadcasted_iota(jnp.int32, (CHUNK, CHUNK), 1)
    incl = ri >= rj
    lower = jnp.where(incl, 1.0, 0.0).astype(BF16)
    lane = lax.broadcasted_iota(jnp.int32, (CHUNK, pair_w), 1)
    sr = lax.broadcasted_iota(jnp.int32, (2 * GLA_DV, pair_w), 0)
    sc = lax.broadcasted_iota(jnp.int32, (2 * GLA_DV, pair_w), 1)
    live = (sr // GLA_DV) == (sc // GLA_DK)
    nw = nw_ref[...]

    def chunk_body(c, carry):
        r0 = pl.multiple_of(c * CHUNK, CHUNK)
        pairs = range(n_pairs)
        heads = range(GLA_HEADS)
        psl = [slice(p * pair_w, (p + 1) * pair_w) for p in pairs]
        hsl = [slice(h * GLA_DV, (h + 1) * GLA_DV) for h in heads]
        g = g_scr[pl.ds(r0, CHUNK), :]
        g1 = g.astype(BF16)
        rem = g - g1.astype(F32)
        g2 = rem.astype(BF16)
        g3 = (rem - g2.astype(F32)).astype(BF16)
        b_all = _dot(lower, g1) + (_dot(lower, g2) + _dot(lower, g3))
        b = [b_all[:, psl[p]] for p in pairs]
        b_mid = [a[CHUNK // 2:CHUNK // 2 + 1, :] for a in b]
        b_last = [a[CHUNK - 1:CHUNK, :] for a in b]
        q = [q_ref[pl.ds(r0, CHUNK), psl[p]].astype(F32) * (GLA_DK ** -0.5) for p in pairs]
        k = [k_ref[pl.ds(r0, CHUNK), psl[p]].astype(F32) for p in pairs]
        q_i = [q[p] * jnp.exp(b[p] - b_mid[p]) for p in pairs]
        k_i = [(k[p] * jnp.exp(b_mid[p] - b[p])).astype(BF16) for p in pairs]
        q_s = [(q[p] * jnp.exp(b[p])).astype(BF16) for p in pairs]
        k_s = [(k[p] * jnp.exp(b_last[p] - b[p])).astype(BF16) for p in pairs]
        vp = [v_ref[pl.ds(r0, CHUNK), 2 * p * GLA_DV:(2 * p + 2) * GLA_DV] for p in pairs]
        st = [st_scr[p] for p in pairs]
        o_inter = [_dot_nt(q_s[p], st[p].astype(BF16)) for p in pairs]
        upd = [_dot_tn(vp[p], k_s[p]) for p in pairs]
        for p in pairs:
            st_scr[p] = st[p] * jnp.exp(b_last[p]) + jnp.where(live, upd[p], 0.0)
        q_h = [jnp.where((lane // GLA_DK) == (h % 2), q_i[h // 2], 0.0).astype(BF16) for h in heads]
        att = [jnp.where(incl, _dot_nt(q_h[h], k_i[h // 2]), 0.0).astype(BF16) for h in heads]
        o = [_dot(att[h], vp[h // 2][:, (h % 2) * GLA_DV:(h % 2 + 1) * GLA_DV])
             + o_inter[h // 2][:, (h % 2) * GLA_DV:(h % 2 + 1) * GLA_DV] for h in heads]
        for h in heads:
            oh = o[h] * lax.rsqrt(jnp.mean(o[h] * o[h], axis=-1, keepdims=True) + EPS) * nw
            rg = r_ref[pl.ds(r0, CHUNK), hsl[h]].astype(F32)
            o_ref[pl.ds(r0, CHUNK), hsl[h]] = (oh * _silu(rg)).astype(o_ref.dtype)
        return carry

    lax.fori_loop(0, ts // CHUNK, chunk_body, 0)


def _gla(p, small, w_a2, b_a2, norm_w, bn, seq):
    t = p.shape[0]
    ts = 256
    per_b = seq // ts
    qk_w = GLA_HEADS * GLA_DK
    v_w = GLA_HEADS * GLA_DV
    gdn_cols = 4 * GDN_HEADS * GDN_DK
    tok = lambda b, s: b * per_b + s
    wa_pad = jnp.zeros((LANES, qk_w), F32).at[2 * GDN_HEADS:2 * GDN_HEADS + GLA_RANK].set(w_a2.astype(F32))
    return pl.pallas_call(
        functools.partial(_gla_kernel, ts=ts),
        grid=(bn, per_b),
        in_specs=[
            pl.BlockSpec((ts, qk_w), lambda b, s: (tok(b, s), gdn_cols // qk_w)),
            pl.BlockSpec((ts, qk_w), lambda b, s: (tok(b, s), gdn_cols // qk_w + 1)),
            pl.BlockSpec((ts, v_w), lambda b, s: (tok(b, s), (gdn_cols + 2 * qk_w) // v_w)),
            pl.BlockSpec((ts, v_w), lambda b, s: (tok(b, s), (gdn_cols + 2 * qk_w) // v_w + 1)),
            pl.BlockSpec((ts, LANES), lambda b, s: (tok(b, s), 0)),
            pl.BlockSpec((LANES, qk_w), lambda b, s: (0, 0)),
            pl.BlockSpec((1, qk_w), lambda b, s: (0, 0)),
            pl.BlockSpec((1, GLA_DV), lambda b, s: (0, 0)),
        ],
        out_specs=pl.BlockSpec((ts, v_w), lambda b, s: (tok(b, s), 0)),
        out_shape=jax.ShapeDtypeStruct((t, v_w), BF16),
        scratch_shapes=[
            pltpu.VMEM((GLA_HEADS // 2, 2 * GLA_DV, 2 * GLA_DK), F32),
            pltpu.VMEM((ts, qk_w), F32),
        ],
        compiler_params=_params("arbitrary", "arbitrary"),
        name="gla",
    )(p, p, p, p, small, wa_pad, b_a2.reshape(1, qk_w).astype(F32), norm_w.reshape(1, GLA_DV))


def _gelu_tanh(x):
    return 0.5 * x * (1.0 + jnp.tanh(0.7978845608028654 * (x + 0.044715 * x * x * x)))


def _lru_kernel(x_ref, g_ref, cw_ref, cb_ref, wra_ref, bra_ref, wri_ref, bri_ref, lam_ref, *rest, ts, n_cast):
    cast_in = rest[:n_cast]
    o_ref = rest[n_cast]
    cast_out = rest[n_cast + 1:2 * n_cast + 1]
    tail_scr, h_scr, a_scr, u_scr = rest[2 * n_cast + 1:]
    width = x_ref.shape[1]
    blk = width // LRU_BLOCKS

    @pl.when(pl.program_id(1) == 0)
    def _():
        tail_scr[...] = jnp.zeros_like(tail_scr)
        h_scr[...] = jnp.zeros_like(h_scr)

    for gi in range(LRU_BLOCKS):
        lo, hi = gi * blk, (gi + 1) * blk
        cur = x_ref[:, lo:hi].astype(F32)
        full = jnp.concatenate([tail_scr[:, lo:hi], cur], axis=0)
        w = cw_ref[:, lo:hi]
        acc = full * w[CONV_W - 1:CONV_W]
        for i in range(CONV_W - 1):
            acc = acc + pltpu.roll(full, CONV_W - 1 - i, axis=0) * w[i:i + 1]
        xc = acc[SUBLANES:] + cb_ref[:, lo:hi]
        tail_scr[:, lo:hi] = cur[ts - SUBLANES:]
        xb = xc.astype(BF16)
        r = _sigmoid(_dot(xb, wra_ref[gi]) + bra_ref[:, lo:hi])
        ig = _sigmoid(_dot(xb, wri_ref[gi]) + bri_ref[:, lo:hi])
        log_a = -LRU_C * r * _softplus(-lam_ref[:, lo:hi])
        a = jnp.exp(log_a)
        a_scr[:, lo:hi] = a
        u_scr[:, lo:hi] = jnp.sqrt(1.0 - a * a) * (ig * xc)

    row = lax.broadcasted_iota(jnp.int32, (SUBLANES, width), 0)

    def scan_body(rb, h_prev):
        r0 = pl.multiple_of(rb * SUBLANES, SUBLANES)
        a = a_scr[pl.ds(r0, SUBLANES), :]
        u = u_scr[pl.ds(r0, SUBLANES), :]
        for d in (1, 2, 4):
            a_sh = jnp.where(row >= d, pltpu.roll(a, d, axis=0), 1.0)
            u_sh = jnp.where(row >= d, pltpu.roll(u, d, axis=0), 0.0)
            u = a * u_sh + u
            a = a * a_sh
        h = u + a * h_prev
        gate = g_ref[pl.ds(r0, SUBLANES), :].astype(F32)
        u_scr[pl.ds(r0, SUBLANES), :] = _gelu_tanh(gate) * h
        return h[SUBLANES - 1:SUBLANES, :]

    h_last = lax.fori_loop(0, ts // SUBLANES, scan_body, h_scr[...])
    h_scr[...] = h_last
    o_ref[...] = u_scr[...].astype(o_ref.dtype)
    _cast_blocks(cast_in, cast_out)


def _lru(p, conv_w, conv_b, w_ra, b_ra, w_ri, b_ri, lam, bn, seq, cast_weights, cast_layer):
    t = p.shape[0]
    width = p.shape[1] // 2
    blk = width // LRU_BLOCKS
    ts = 256
    per_b = seq // ts
    tok = lambda b, s: b * per_b + s
    vec = lambda a: a.reshape(1, width).astype(F32)
    vspec = pl.BlockSpec((1, width), lambda b, s: (0, 0))
    wspec = pl.BlockSpec((LRU_BLOCKS, blk, blk), lambda b, s: (0, 0, 0))
    casts = [_cast_specs(w, cast_layer, bn * per_b, tok) for w in cast_weights]
    return pl.pallas_call(
        functools.partial(_lru_kernel, ts=ts, n_cast=len(casts)),
        grid=(bn, per_b),
        in_specs=[
            pl.BlockSpec((ts, width), lambda b, s: (tok(b, s), 0)),
            pl.BlockSpec((ts, width), lambda b, s: (tok(b, s), 1)),
            pl.BlockSpec((CONV_W, width), lambda b, s: (0, 0)),
            vspec, wspec, vspec, wspec, vspec, vspec,
        ] + [c[0] for c in casts],
        out_specs=[pl.BlockSpec((ts, width), lambda b, s: (tok(b, s), 0))] + [c[1] for c in casts],
        out_shape=[jax.ShapeDtypeStruct((t, width), BF16)] + [c[2] for c in casts],
        scratch_shapes=[
            pltpu.VMEM((SUBLANES, width), F32),
            pltpu.VMEM((1, width), F32),
            pltpu.VMEM((ts, width), F32),
            pltpu.VMEM((ts, width), F32),
        ],
        compiler_params=_params("arbitrary", "arbitrary"),
        name="lru",
    )(p, p, conv_w, vec(conv_b), w_ra.astype(BF16), vec(b_ra), w_ri.astype(BF16), vec(b_ri), vec(lam),
      *cast_weights)


def _pack_bf16_pairs(h):
    half = h.shape[1] // 2
    lo = lax.bitcast_convert_type(h[:, :half].astype(BF16).astype(F32), jnp.uint32)
    hi = lax.bitcast_convert_type(h[:, half:].astype(BF16).astype(F32), jnp.uint32)
    return (lo >> 16) | (hi & jnp.uint32(0xFFFF0000))


def _unpack_bf16_pairs(w):
    lo = lax.bitcast_convert_type(w << 16, F32).astype(BF16)
    hi = lax.bitcast_convert_type(w & jnp.uint32(0xFFFF0000), F32).astype(BF16)
    return lo, hi


def _out_proj_kernel(*refs, n_in):
    a_refs = refs[:n_in]
    w_refs = refs[n_in:2 * n_in]
    (x_ref, mod_ref, ln_ref, rwh_ref, rwl_ref, rb_ref,
     xo_ref, hp_ref, ti_ref, tw_ref, tr_ref, cnt_ref, cnt_scr) = refs[2 * n_in:]

    @pl.when(pl.program_id(0) == 0)
    def _():
        cnt_scr[...] = jnp.zeros_like(cnt_scr)

    y = _dot(a_refs[0][...], w_refs[0][...])
    for a_ref, w_ref in zip(a_refs[1:], w_refs[1:]):
        y = y + _dot(a_ref[...], w_ref[...])
    m = mod_ref[0]
    xn = x_ref[...] + m[2:3] * y
    xo_ref[...] = xn
    h = _norm_mod(xn, ln_ref[...], m[3:4], m[4:5])
    hp_ref[...] = _pack_bf16_pairs(h).reshape(hp_ref.shape)
    h_hi = h.astype(BF16)
    h_lo = (h - h_hi.astype(F32)).astype(BF16)
    hh = _dot(h_hi, rwl_ref[...])
    logits = hh[:, :LANES] + (hh[:, LANES:] + _dot(h_lo, rwh_ref[...])) + rb_ref[...]
    lane = lax.broadcasted_iota(jnp.int32, logits.shape, 1)
    cur = jnp.where(lane < N_EXPERTS, logits, -jnp.inf)
    vals, idxs = [], []
    for _ in range(TOP_K):
        mx = jnp.max(cur, axis=-1, keepdims=True)
        ix = jnp.min(jnp.where(cur == mx, lane, LANES), axis=-1, keepdims=True)
        vals.append(mx)
        idxs.append(ix)
        cur = jnp.where(lane == ix, -jnp.inf, cur)
    es = [jnp.exp(v - vals[0]) for v in vals]
    den = es[0]
    for e in es[1:]:
        den = den + e
    tm = logits.shape[0]
    ri = lax.broadcasted_iota(jnp.int32, (tm, tm), 0)
    rj = lax.broadcasted_iota(jnp.int32, (tm, tm), 1)
    tri = jnp.where(rj < ri, 1.0, 0.0).astype(BF16)
    base = cnt_scr[...]
    ti = jnp.zeros(logits.shape, jnp.int32)
    tw = jnp.zeros(logits.shape, F32)
    tr = jnp.zeros(logits.shape, jnp.int32)
    for kk in range(TOP_K):
        sel = lane == idxs[kk]
        onehot = jnp.where(sel, 1.0, 0.0)
        before = _dot(tri, onehot.astype(BF16)) + base
        rank = jnp.sum(jnp.where(sel, before, 0.0), axis=-1, keepdims=True)
        base = base + jnp.sum(onehot, axis=0, keepdims=True)
        ti = jnp.where(lane == kk, idxs[kk], ti)
        tw = jnp.where(lane == kk, es[kk] / den, tw)
        tr = jnp.where(lane == kk, rank.astype(jnp.int32), tr)
    cnt_scr[...] = base
    ti_ref[...] = ti
    tw_ref[...] = tw
    tr_ref[...] = tr
    cnt_ref[...] = base.astype(jnp.int32)


def _out_proj(acts, weights, x, mod, ln_ffn, router_w, router_b, seq):
    t, d = x.shape
    tm = 256
    per_b = seq // tm
    n_in = len(acts)
    rw = jnp.zeros((d, LANES), F32).at[:, :N_EXPERTS].set(router_w.astype(F32))
    rw_hi = rw.astype(BF16)
    rw_lo = jnp.concatenate([rw_hi, (rw - rw_hi.astype(F32)).astype(BF16)], axis=1)
    rb = jnp.zeros((1, LANES), F32).at[0, :N_EXPERTS].set(router_b.astype(F32))
    tiles = d // 2 // LANES
    in_specs = [pl.BlockSpec((tm, a.shape[1]), lambda i: (i, 0)) for a in acts]
    in_specs += [pl.BlockSpec(w.shape, lambda i: (0, 0)) for w in weights]
    in_specs += [
        pl.BlockSpec((tm, d), lambda i: (i, 0)),
        pl.BlockSpec((1, 6, d), lambda i: (i // per_b, 0, 0)),
        pl.BlockSpec((1, d), lambda i: (0, 0)),
        pl.BlockSpec((d, LANES), lambda i: (0, 0)),
        pl.BlockSpec((d, 2 * LANES), lambda i: (0, 0)),
        pl.BlockSpec((1, LANES), lambda i: (0, 0)),
    ]
    row_d = pl.BlockSpec((tm, d), lambda i: (i, 0))
    row_h = pl.BlockSpec((tm, tiles, LANES), lambda i: (i, 0, 0))
    row_l = pl.BlockSpec((tm, LANES), lambda i: (i, 0))
    return pl.pallas_call(
        functools.partial(_out_proj_kernel, n_in=n_in),
        grid=(t // tm,),
        in_specs=in_specs,
        out_specs=[row_d, row_h, row_l, row_l, row_l, pl.BlockSpec((1, LANES), lambda i: (0, 0))],
        out_shape=[
            jax.ShapeDtypeStruct((t, d), F32),
            jax.ShapeDtypeStruct((t, tiles, LANES), jnp.uint32),
            jax.ShapeDtypeStruct((t, LANES), jnp.int32),
            jax.ShapeDtypeStruct((t, LANES), F32),
            jax.ShapeDtypeStruct((t, LANES), jnp.int32),
            jax.ShapeDtypeStruct((1, LANES), jnp.int32),
        ],
        scratch_shapes=[pltpu.VMEM((1, LANES), F32)],
        compiler_params=_params("arbitrary"),
        name="out_proj",
    )(*acts, *weights, x, mod, ln_ffn.reshape(1, d), rw_hi, rw_lo, rb)


def _moe_routing(ti, tr, cnt, n_tok):
    n_blk = (n_tok * TOP_K) // MOE_BLOCK + N_EXPERTS
    counts = cnt[0, :N_EXPERTS]
    padded = ((counts + MOE_BLOCK - 1) // MOE_BLOCK) * MOE_BLOCK
    pend = jnp.cumsum(padded)
    pstart = pend - padded
    e = ti[:, :TOP_K]
    hit = e[:, :, None] == jnp.arange(N_EXPERTS, dtype=jnp.int32)
    dest = tr[:, :TOP_K] + jnp.sum(jnp.where(hit, pstart, 0), axis=-1)
    blk_start = jnp.arange(n_blk, dtype=jnp.int32) * MOE_BLOCK
    blk_e = jnp.minimum(jnp.sum(pend[None, :] <= blk_start[:, None], axis=1), N_EXPERTS - 1).astype(jnp.int32)
    i = jnp.arange(MOE_BLOCK, dtype=jnp.int32)
    tail = (pstart + counts)[:, None] + i[None, :]
    is_tail = i[None, :] < (padded - counts)[:, None]
    spare = (jnp.cumsum(jnp.logical_not(is_tail).reshape(-1).astype(jnp.int32)) - 1).reshape(is_tail.shape)
    pad_slots = jnp.where(is_tail, tail, pend[-1] + spare).reshape(-1)
    n_used = (pend[-1:] // MOE_BLOCK).astype(jnp.int32)
    return dest.astype(jnp.int32), pad_slots.astype(jnp.int32), blk_e, n_used


DISPATCH_TOKENS = 512
COMBINE_TOKENS = 256
ISSUE_TOKENS = 4
ISSUE_ROWS = ISSUE_TOKENS * TOP_K


def _dispatch_kernel(dest_ref, pad_ref, hp_ref, xs_hbm, zero_scr, sem):
    n_pad = pad_ref.shape[2]
    zero_scr[...] = jnp.zeros_like(zero_scr)

    def issue(g, c):
        t0 = g * ISSUE_TOKENS
        dsts = [dest_ref[0, 0, t0 * TOP_K + j] for j in range(ISSUE_ROWS)]
        for j, dst in enumerate(dsts):
            pltpu.make_async_copy(hp_ref.at[t0 + j // TOP_K], xs_hbm.at[dst], sem.at[0]).start(priority=j % 2)
        return c

    lax.fori_loop(0, DISPATCH_TOKENS // ISSUE_TOKENS, issue, 0)

    def issue_zero(g, c):
        dsts = [pad_ref[0, 0, g * ISSUE_ROWS + j] for j in range(ISSUE_ROWS)]
        for j, dst in enumerate(dsts):
            pltpu.make_async_copy(zero_scr, xs_hbm.at[dst], sem.at[0]).start(priority=j % 2)
        return c

    lax.fori_loop(0, n_pad // ISSUE_ROWS, issue_zero, 0)

    rows = DISPATCH_TOKENS * TOP_K + n_pad
    for _ in range(rows // DISPATCH_TOKENS):
        pltpu.make_async_copy(hp_ref, xs_hbm.at[pl.ds(0, DISPATCH_TOKENS)], sem.at[0]).wait()
    rem = rows % DISPATCH_TOKENS
    if rem:
        pltpu.make_async_copy(hp_ref.at[pl.ds(0, rem)], xs_hbm.at[pl.ds(0, rem)], sem.at[0]).wait()


def _dispatch(hp, dest, pad_slots, n_buf):
    t = hp.shape[0]
    n_steps = t // DISPATCH_TOKENS
    n_pad = pad_slots.shape[0] // n_steps
    assert n_pad * n_steps == pad_slots.shape[0] and n_pad % ISSUE_ROWS == 0
    smem = lambda n: pl.BlockSpec((1, 1, n), lambda i: (i, 0, 0), memory_space=pltpu.SMEM)
    return pl.pallas_call(
        _dispatch_kernel,
        grid=(n_steps,),
        in_specs=[
            smem(DISPATCH_TOKENS * TOP_K),
            smem(n_pad),
            pl.BlockSpec((DISPATCH_TOKENS,) + hp.shape[1:], lambda i: (i, 0, 0)),
        ],
        out_specs=pl.BlockSpec(memory_space=pl.ANY),
        out_shape=jax.ShapeDtypeStruct((n_buf,) + hp.shape[1:], jnp.uint32),
        scratch_shapes=[pltpu.VMEM(hp.shape[1:], jnp.uint32), pltpu.SemaphoreType.DMA((1,))],
        compiler_params=_params("arbitrary"),
        name="moe_dispatch",
    )(dest.reshape(n_steps, 1, DISPATCH_TOKENS * TOP_K), pad_slots.reshape(n_steps, 1, n_pad), hp)


def _moe_kernel(blk_e_ref, n_used_ref, xs_ref, wgu_ref, bgu_ref, wdn_ref, bdn_ref, y_ref):
    del blk_e_ref
    half = xs_ref.shape[1] * xs_ref.shape[2]
    d_ff = wdn_ref.shape[1]
    used = pl.program_id(0) < n_used_ref[0]

    @pl.when(used)
    def _():
        lo, hi = _unpack_bf16_pairs(xs_ref[...].reshape(MOE_BLOCK, half))
        gu = _dot(lo, wgu_ref[0, :half, :]) + _dot(hi, wgu_ref[0, half:, :]) + bgu_ref[0]
        gate = jnp.minimum(gu[:, :d_ff], SWIGLU_LIMIT)
        up = jnp.clip(gu[:, d_ff:], -SWIGLU_LIMIT, SWIGLU_LIMIT)
        act = (up + 1.0) * gate * _sigmoid(SWIGLU_ALPHA * gate)
        y = _dot(act.astype(BF16), wdn_ref[0]) + bdn_ref[0]
        y_ref[...] = y.reshape(y_ref.shape)

    @pl.when(jnp.logical_not(used))
    def _():
        y_ref[...] = jnp.zeros_like(y_ref)


def _moe_experts(xs, blk_e, n_used, w_gu, b_gu, w_dn, b_dn):
    n_buf, tiles, _ = xs.shape
    d = 2 * tiles * LANES
    d_ff = w_dn.shape[1]
    n_w = w_gu.shape[0]
    grid_spec = pltpu.PrefetchScalarGridSpec(
        num_scalar_prefetch=2,
        grid=(n_buf // MOE_BLOCK,),
        in_specs=[
            pl.BlockSpec((MOE_BLOCK, tiles, LANES), lambda b, e, u: (b, 0, 0)),
            pl.BlockSpec((1, d, 2 * d_ff), lambda b, e, u: (e[b], 0, 0)),
            pl.BlockSpec((1, 1, 2 * d_ff), lambda b, e, u: (e[b], 0, 0)),
            pl.BlockSpec((1, d_ff, d), lambda b, e, u: (e[b], 0, 0)),
            pl.BlockSpec((1, 1, d), lambda b, e, u: (e[b], 0, 0)),
        ],
        out_specs=pl.BlockSpec((MOE_BLOCK, d // LANES, LANES), lambda b, e, u: (b, 0, 0)),
    )
    return pl.pallas_call(
        _moe_kernel,
        grid_spec=grid_spec,
        out_shape=jax.ShapeDtypeStruct((n_buf, d // LANES, LANES), F32),
        compiler_params=_params("arbitrary"),
        name="moe_experts",
    )(blk_e, n_used, xs, w_gu, b_gu.reshape(n_w, 1, 2 * d_ff), w_dn, b_dn.reshape(n_w, 1, d))


def _combine_kernel(dest_ref, dest_next_ref, x_ref, tw_ref, mod_ref, lnf_ref, y_hbm, o_ref, ybuf, sem, *, final):
    i = pl.program_id(0)
    n = pl.num_programs(0)
    slot = i % 2
    tm = COMBINE_TOKENS

    def issue(ids_ref, s):
        def body(g, c):
            t0 = g * ISSUE_TOKENS
            srcs = [ids_ref[0, 0, t0 * TOP_K + j] for j in range(ISSUE_TOKENS * TOP_K)]
            for j, src in enumerate(srcs):
                pltpu.make_async_copy(y_hbm.at[src], ybuf.at[s, (j % TOP_K) * tm + t0 + j // TOP_K],
                                      sem.at[s]).start(priority=j % 2)
            return c
        lax.fori_loop(0, tm // ISSUE_TOKENS, body, 0)

    @pl.when(i == 0)
    def _():
        issue(dest_ref, 0)

    @pl.when(i + 1 < n)
    def _():
        issue(dest_next_ref, 1 - slot)

    pltpu.make_async_copy(y_hbm.at[pl.ds(0, TOP_K * tm)], ybuf.at[slot], sem.at[slot]).wait()
    tw = tw_ref[...]
    acc = None
    for kk in range(TOP_K):
        gate = jnp.broadcast_to(tw[:, kk:kk + 1], (tm, LANES)).reshape(tm, 1, LANES)
        term = gate * ybuf[slot, kk * tm:(kk + 1) * tm]
        acc = term if acc is None else acc + term
    xn = x_ref[...] + mod_ref[0][5:6] * acc.reshape(x_ref.shape)
    if final:
        xn = xn * lax.rsqrt(jnp.mean(xn * xn, axis=-1, keepdims=True) + EPS) * lnf_ref[...]
    o_ref[...] = xn


def _combine(x, y, dest, tw, mod, ln_final, seq, final):
    t, d = x.shape
    tm = COMBINE_TOKENS
    per_b = seq // tm
    nt = t // tm
    ids = dest.reshape(nt, 1, tm * TOP_K)
    id_spec = lambda f: pl.BlockSpec((1, 1, tm * TOP_K), f, memory_space=pltpu.SMEM)
    return pl.pallas_call(
        functools.partial(_combine_kernel, final=final),
        grid=(nt,),
        in_specs=[
            id_spec(lambda i: (i, 0, 0)),
            id_spec(lambda i: (jnp.minimum(i + 1, nt - 1), 0, 0)),
            pl.BlockSpec((tm, d), lambda i: (i, 0)),
            pl.BlockSpec((tm, LANES), lambda i: (i, 0)),
            pl.BlockSpec((1, 6, d), lambda i: (i // per_b, 0, 0)),
            pl.BlockSpec((1, d), lambda i: (0, 0)),
            pl.BlockSpec(memory_space=pl.ANY),
        ],
        out_specs=pl.BlockSpec((tm, d), lambda i: (i, 0)),
        out_shape=jax.ShapeDtypeStruct((t, d), F32),
        scratch_shapes=[pltpu.VMEM((2, TOP_K * tm, d // LANES, LANES), F32), pltpu.SemaphoreType.DMA((2,))],
        compiler_params=_params("arbitrary"),
        name="moe_combine",
    )(ids, ids, x, tw, mod, ln_final.reshape(1, d), y)


def _ffn(x, hp, ti, tw, tr, cnt, mod, w_gu, b_gu, w_dn, b_dn, ln_final, seq, final):
    t = x.shape[0]
    dest, pad_slots, blk_e, n_used = _moe_routing(ti, tr, cnt, t)
    n_buf = blk_e.shape[0] * MOE_BLOCK
    xs = _dispatch(hp, dest, pad_slots, n_buf)
    y = _moe_experts(xs, blk_e, n_used, w_gu, b_gu, w_dn, b_dn)
    return _combine(x, y, dest, tw, mod, ln_final, seq, final)


def _even_inproj_weights(w_in):
    gdn_cols = 4 * GDN_HEADS * GDN_DK
    small0 = gdn_cols
    gla0 = small0 + 2 * GDN_HEADS
    gla_cols = 2 * GLA_HEADS * GLA_DK + 2 * GLA_HEADS * GLA_DV
    lg0 = gla0 + gla_cols
    w_big = jnp.concatenate([w_in[:, :gdn_cols], w_in[:, gla0:lg0]], axis=1).astype(BF16)
    w_small = jnp.zeros((w_in.shape[0], LANES), F32)
    w_small = w_small.at[:, :2 * GDN_HEADS].set(w_in[:, small0:gla0])
    w_small = w_small.at[:, 2 * GDN_HEADS:2 * GDN_HEADS + GLA_RANK].set(w_in[:, lg0:lg0 + GLA_RANK])
    return w_big, w_small.astype(BF16)


def _even_mixer(x, mod, ln_mix, w_in, conv_w, a_log, dt_bias, gdn_norm, w_a2, b_a2, gla_norm, bn, seq,
                cast_weights, cast_layer):
    w_big, w_small = _even_inproj_weights(w_in)
    p, small = _norm_proj(x, ln_mix, mod, w_big, w_small, seq)
    col, row = _gdn_gates(small, a_log, dt_bias)
    o_a, *w_bf16 = _gdn(p, col, row, conv_w, gdn_norm, bn, seq, cast_weights, cast_layer)
    o_b = _gla(p, small, w_a2, b_a2, gla_norm, bn, seq)
    return o_a, o_b, w_bf16


def kernel(x, c, ln_mix, ln_ffn, ln_final, w_ada, b_ada, e_w_in, e_conv_w, e_a_log, e_dt_bias, e_gdn_norm, e_gla_w_a2, e_gla_b_a2, e_gla_norm, e_w_out, o_w_in, o_conv_w, o_conv_b, o_w_ra, o_b_ra, o_w_ri, o_b_ri, o_lam, o_w_out, router_w, router_b, exp_w_gu, exp_b_gu, exp_w_dn, exp_b_dn):
    bn, seq, d = x.shape
    depth = w_ada.shape[0]
    mods = _ada_mod(c, w_ada, b_ada)
    xf = x.reshape(bn * seq, d)
    experts = [exp_w_gu, exp_w_dn]
    for layer in range(depth):
        mod = mods[layer]
        j = layer // 2
        if layer % 2 == 0:
            o_a, o_b, (w_gu, w_dn) = _even_mixer(xf, mod, ln_mix[layer], e_w_in[j], e_conv_w[j], e_a_log[j],
                                                 e_dt_bias[j], e_gdn_norm[j], e_gla_w_a2[j], e_gla_b_a2[j],
                                                 e_gla_norm[j], bn, seq, experts, layer)
            split = o_a.shape[1]
            acts = [o_a, o_b]
            weights = [e_w_out[j][:split].astype(BF16), e_w_out[j][split:].astype(BF16)]
        else:
            p = _norm_proj(xf, ln_mix[layer], mod, o_w_in[j].astype(BF16), None, seq)
            o_c, w_gu, w_dn = _lru(p, o_conv_w[j], o_conv_b[j], o_w_ra[j], o_b_ra[j], o_w_ri[j], o_b_ri[j], o_lam[j],
                                   bn, seq, experts, layer)
            acts = [o_c]
            weights = [o_w_out[j].astype(BF16)]
        xf, hp, ti, tw, tr, cnt = _out_proj(acts, weights, xf, mod, ln_ffn[layer], router_w[layer], router_b[layer], seq)
        xf = _ffn(xf, hp, ti, tw, tr, cnt, mod, w_gu, exp_b_gu[layer], w_dn, exp_b_dn[layer],
                  ln_final, seq, final=(layer == depth - 1))
    return xf.reshape(bn, seq, d)
```
